```python
import jax, jax.numpy as jnp
from jax import lax
import numpy as np

D_MODEL = 1024
BATCH = 1
SEQ = 16384
DEPTH = 2
DEC_BATCH = 32
DEC_SEQ = 2048
PAST_LEN = 128

GROUP_CH = 64
CONV_CH = D_MODEL // 2
FNET_CH = D_MODEL // 2
N_FNET_GROUPS = FNET_CH // GROUP_CH
CONV_WIDTH = 31
SC_CH = D_MODEL
SC_WIDTH = 3
N_EXPERT_GROUPS = 4
EXPERTS_PER_GROUP = 8
N_EXPERTS = N_EXPERT_GROUPS * EXPERTS_PER_GROUP
TOP_K = 2
D_EXPERT = D_MODEL // 2
ROUTE_BLOCK = 128
RMS_EPS = 1e-6
LN_EPS = 1e-5

kernel_name = 'hybrid_conformerconv_fnet_shortconv_hmoe_encoder'


def rms_norm(x, g):
    xf = x.astype(jnp.float32)
    y = xf * lax.rsqrt(jnp.mean(xf * xf, axis=-1, keepdims=True) + RMS_EPS)
    return (y * g.astype(jnp.float32)).astype(x.dtype)


def layer_norm(x, g, b):
    xf = x.astype(jnp.float32)
    mu = jnp.mean(xf, axis=-1, keepdims=True)
    xc = xf - mu
    var = jnp.mean(xc * xc, axis=-1, keepdims=True)
    y = xc * lax.rsqrt(var + LN_EPS) * g.astype(jnp.float32) + b.astype(jnp.float32)
    return y.astype(x.dtype)


def depthwise_conv(x, w):
    c = x.shape[-1]
    return lax.conv_general_dilated(
        x, w[:, None, :].astype(x.dtype), window_strides=(1,), padding='SAME',
        dimension_numbers=('NWC', 'WIO', 'NWC'), feature_group_count=c)


def mixer_conv_fnet(h, w_in, conv_w, conv_b, ln_g, ln_b, w_out):
    u = h @ w_in.astype(h.dtype)
    a_val, a_gate, f = jnp.split(u, [CONV_CH, 2 * CONV_CH], axis=-1)
    a = a_val * jax.nn.sigmoid(a_gate)
    a = depthwise_conv(a, conv_w) + conv_b.astype(h.dtype)
    a = jax.nn.silu(layer_norm(a, ln_g, ln_b))
    bsz, s, _ = f.shape
    fg = f.astype(jnp.float32).reshape(bsz, s, N_FNET_GROUPS, GROUP_CH)
    fo = jnp.fft.fft2(fg, axes=(1, 3), norm='ortho').real
    fo = fo.reshape(bsz, s, FNET_CH).astype(h.dtype)
    return jnp.concatenate([a, fo], axis=-1) @ w_out.astype(h.dtype)


def mixer_short_conv(h, w_in, conv_w, w_out):
    u = h @ w_in.astype(h.dtype)
    gate_b, gate_c, v = jnp.split(u, 3, axis=-1)
    y = gate_b * depthwise_conv(gate_c * v, conv_w)
    return y @ w_out.astype(h.dtype)


def hier_moe(h, w_coarse, b_coarse, w_fine, b_fine, w_gate, w_up, w_down):
    bsz, s, d = h.shape
    t = h.reshape(-1, d)
    n_tok = t.shape[0]
    tf = t.astype(jnp.float32)
    logit_c = tf @ w_coarse.astype(jnp.float32) + b_coarse.astype(jnp.float32)
    p_c = jax.nn.softmax(logit_c, axis=-1)
    grp = jnp.argmax(logit_c, axis=-1)
    p_grp = jnp.take_along_axis(p_c, grp[:, None], axis=1)
    logit_f = jnp.einsum('td,dge->tge', tf, w_fine.astype(jnp.float32)) + b_fine.astype(jnp.float32)
    logit_f = jnp.take_along_axis(logit_f, grp[:, None, None], axis=1)[:, 0]
    top_v, top_i = lax.top_k(logit_f, TOP_K)
    gates = p_grp * jax.nn.softmax(top_v, axis=-1)
    expert = grp[:, None].astype(jnp.int32) * EXPERTS_PER_GROUP + top_i.astype(jnp.int32)

    n_asg = n_tok * TOP_K
    flat_e = expert.reshape(-1)
    order = jnp.argsort(flat_e).astype(jnp.int32)
    sorted_e = flat_e[order]
    counts = jnp.bincount(flat_e, length=N_EXPERTS).astype(jnp.int32)
    padded = (counts + ROUTE_BLOCK - 1) // ROUTE_BLOCK * ROUTE_BLOCK
    pad_end = jnp.cumsum(padded)
    pad_start = pad_end - padded
    start = jnp.cumsum(counts) - counts
    dest = pad_start[sorted_e] + jnp.arange(n_asg, dtype=jnp.int32) - start[sorted_e]
    n_blocks = -(-n_asg // ROUTE_BLOCK) + N_EXPERTS
    n_rows = n_blocks * ROUTE_BLOCK
    row_token = jnp.zeros((n_rows,), jnp.int32).at[dest].set(order // TOP_K)
    block_start = jnp.arange(n_blocks, dtype=jnp.int32) * ROUTE_BLOCK
    block_expert = jnp.minimum(jnp.searchsorted(pad_end, block_start, side='right'),
                               N_EXPERTS - 1).astype(jnp.int32)
    xb = t[row_token].reshape(n_blocks, ROUTE_BLOCK, d)

    def expert_block(args):
        xblk, e = args
        g = xblk @ w_gate[e].astype(xblk.dtype)
        u = xblk @ w_up[e].astype(xblk.dtype)
        return (jax.nn.silu(g) * u) @ w_down[e].astype(xblk.dtype)

    yb = lax.map(expert_block, (xb, block_expert)).reshape(n_rows, d)
    row_of_asg = jnp.zeros((n_asg,), jnp.int32).at[order].set(dest).reshape(n_tok, TOP_K)
    y = jnp.einsum('tkd,tk->td', yb[row_of_asg], gates.astype(h.dtype))
    return y.reshape(bsz, s, d)


def trunk(x, layers, final_norm):
    for i in range(DEPTH):
        p = layers[i]
        h = rms_norm(x, p['norm_mix'])
        if i % 2 == 0:
            x = x + mixer_conv_fnet(h, *p['mix'])
        else:
            x = x + mixer_short_conv(h, *p['mix'])
        x = x + hier_moe(rms_norm(x, p['norm_ffn']), *p['moe'])
    return rms_norm(x, final_norm)


def setup_inputs(seed: int = 0) -> dict:
    key = jax.random.key(seed)
    ks = iter(jax.random.split(key, 48))
    nrm = lambda shape, scale: jax.random.normal(next(ks), shape, jnp.float32) * scale
    d = D_MODEL

    def moe(prefix):
        return {
            prefix + 'norm_ffn': 1.0 + nrm((d,), 0.01),
            prefix + 'w_coarse': nrm((d, N_EXPERT_GROUPS), d ** -0.5),
            prefix + 'b_coarse': nrm((N_EXPERT_GROUPS,), 0.01),
            prefix + 'w_fine': nrm((d, N_EXPERT_GROUPS, EXPERTS_PER_GROUP), d ** -0.5),
            prefix + 'b_fine': nrm((N_EXPERT_GROUPS, EXPERTS_PER_GROUP), 0.01),
            prefix + 'w_gate': nrm((N_EXPERTS, d, D_EXPERT), d ** -0.5),
            prefix + 'w_up': nrm((N_EXPERTS, d, D_EXPERT), d ** -0.5),
            prefix + 'w_down': nrm((N_EXPERTS, D_EXPERT, d), D_EXPERT ** -0.5),
        }

    out = {
        'x_prompt': nrm((BATCH, SEQ, d), 1.0),
        'x_sample': nrm((DEC_BATCH, DEC_SEQ, d), 1.0),
        'l0_norm_mix': 1.0 + nrm((d,), 0.01),
        'l0_w_in': nrm((d, 2 * CONV_CH + FNET_CH), d ** -0.5),
        'l0_conv_w': nrm((CONV_WIDTH, CONV_CH), CONV_WIDTH ** -0.5),
        'l0_conv_b': nrm((CONV_CH,), 0.01),
        'l0_ln_g': 1.0 + nrm((CONV_CH,), 0.01),
        'l0_ln_b': nrm((CONV_CH,), 0.01),
        'l0_w_out': nrm((CONV_CH + FNET_CH, d), (CONV_CH + FNET_CH) ** -0.5),
    }
    out.update(moe('l0_'))
    out.update({
        'l1_norm_mix': 1.0 + nrm((d,), 0.01),
        'l1_w_in': nrm((d, 3 * SC_CH), d ** -0.5),
        'l1_conv_w': nrm((SC_WIDTH, SC_CH), SC_WIDTH ** -0.5),
        'l1_w_out': nrm((SC_CH, d), SC_CH ** -0.5),
    })
    out.update(moe('l1_'))
    out['final_norm'] = 1.0 + nrm((d,), 0.01)
    return out


def reference(x_prompt, x_sample,
              l0_norm_mix, l0_w_in, l0_conv_w, l0_conv_b, l0_ln_g, l0_ln_b, l0_w_out,
              l0_norm_ffn, l0_w_coarse, l0_b_coarse, l0_w_fine, l0_b_fine, l0_w_gate, l0_w_up, l0_w_down,
              l1_norm_mix, l1_w_in, l1_conv_w, l1_w_out,
              l1_norm_ffn, l1_w_coarse, l1_b_coarse, l1_w_fine, l1_b_fine, l1_w_gate, l1_w_up, l1_w_down,
              final_norm):
    layers = [
        {'norm_mix': l0_norm_mix,
         'mix': (l0_w_in, l0_conv_w, l0_conv_b, l0_ln_g, l0_ln_b, l0_w_out),
         'norm_ffn': l0_norm_ffn,
         'moe': (l0_w_coarse, l0_b_coarse, l0_w_fine, l0_b_fine, l0_w_gate, l0_w_up, l0_w_down)},
        {'norm_mix': l1_norm_mix,
         'mix': (l1_w_in, l1_conv_w, l1_w_out),
         'norm_ffn': l1_norm_ffn,
         'moe': (l1_w_coarse, l1_b_coarse, l1_w_fine, l1_b_fine, l1_w_gate, l1_w_up, l1_w_down)},
    ]
    y_prompt = trunk(x_prompt, layers, final_norm)
    y_sample = trunk(x_sample, layers, final_norm)
    return (y_prompt, y_sample)
```

```python
import functools
import math

import jax
import jax.numpy as jnp
import numpy as np
from jax import lax
from jax.experimental import pallas as pl
from jax.experimental.pallas import tpu as pltpu

D = 1024
T_PROMPT = 16384
SEQ_SAMPLE = 2048
T_SAMPLE = 32 * SEQ_SAMPLE
T_ALL = T_PROMPT + T_SAMPLE
HALF = 512
GROUP_CH = 64
CONV_W = 31
N_GROUPS = 4
EPG = 8
N_PAIRS = EPG * (EPG - 1) // 2
N_BUCKETS = N_GROUPS * N_PAIRS
D_EXPERT = 512
RMS_EPS = 1e-6
LN_EPS = 1e-5

TB = 512
N_TILES = T_ALL // TB
N_PROMPT_TILES = T_PROMPT // TB
HALO = 16
CONV_ROWS = 64
MOE_ROWS = 128
N_MOE_BLOCKS = T_ALL // MOE_ROWS + N_BUCKETS
META = 128
ROW_W = D + META
FFT_N2 = 128
FFT_KB = 4

BF16 = jnp.bfloat16
F32 = jnp.float32


def _rms(x, g):
    return x * lax.rsqrt(jnp.mean(x * x, axis=-1, keepdims=True) + RMS_EPS) * g


def _dot(a, b):
    return jnp.dot(a, b, preferred_element_type=F32)


def _seq_edges(i):
    r0 = i * TB
    r1 = r0 + TB
    in_prompt = r0 < T_PROMPT
    starts = jnp.where(in_prompt, r0 == 0, (r0 - T_PROMPT) % SEQ_SAMPLE == 0)
    ends = jnp.where(in_prompt, r1 == T_PROMPT, (r1 - T_PROMPT) % SEQ_SAMPLE == 0)
    return starts, ends


def _tile_spec(width):
    return pl.BlockSpec((TB, width), lambda i: (i, 0))


def _full_spec(shape):
    return pl.BlockSpec(shape, lambda *_: (0,) * len(shape))


def _prompt_spec(width=D):
    return pl.BlockSpec((TB, width), lambda i: (jnp.minimum(i, N_PROMPT_TILES - 1), 0))


def _sample_spec(width=D):
    return pl.BlockSpec((TB, width), lambda i: (jnp.maximum(i - N_PROMPT_TILES, 0), 0))


def _halo_specs(width):
    per_tile = TB // HALO
    last = T_ALL // HALO - 1
    prev = pl.BlockSpec((HALO, width), lambda i: (jnp.maximum(i * per_tile - 1, 0), 0))
    nxt = pl.BlockSpec((HALO, width), lambda i: (jnp.minimum((i + 1) * per_tile, last), 0))
    return prev, nxt


def _l0_in_body(xp_ref, xs_ref, g_ref, w_ref, a_ref, f_ref):
    i = pl.program_id(0)
    x = jnp.where(i < N_PROMPT_TILES, xp_ref[...], xs_ref[...])
    h = _rms(x, g_ref[...])
    u = _dot(h.astype(BF16), w_ref[...])
    a_ref[...] = u[:, :HALF] * jax.nn.sigmoid(u[:, HALF:2 * HALF])
    f_ref[...] = u[:, 2 * HALF:].astype(BF16)


def _l0_in(xp, xs, g, w_in):
    return pl.pallas_call(
        _l0_in_body, name="l0_in",
        grid=(N_TILES,),
        in_specs=[_prompt_spec(), _sample_spec(), _full_spec((1, D)), _full_spec((D, 3 * HALF))],
        out_specs=[_tile_spec(HALF), _tile_spec(HALF)],
        out_shape=[jax.ShapeDtypeStruct((T_ALL, HALF), F32), jax.ShapeDtypeStruct((T_ALL, HALF), BF16)],
    )(xp, xs, g, w_in)


def _l0_conv_body(prev_ref, cur_ref, next_ref, w_ref, b_ref, g_ref, beta_ref, o_ref, buf):
    starts, ends = _seq_edges(pl.program_id(0))
    buf[0:HALO, :] = jnp.where(starts, 0.0, prev_ref[...])
    buf[HALO:HALO + TB, :] = cur_ref[...]
    buf[HALO + TB:, :] = jnp.where(ends, 0.0, next_ref[...])
    first = HALO - CONV_W // 2
    for r0 in range(0, TB, CONV_ROWS):
        acc = jnp.broadcast_to(b_ref[...], (CONV_ROWS, HALF))
        for k in range(CONV_W):
            acc = acc + w_ref[k:k + 1, :] * buf[r0 + first + k:r0 + first + k + CONV_ROWS, :]
        mu = jnp.mean(acc, axis=-1, keepdims=True)
        xc = acc - mu
        var = jnp.mean(xc * xc, axis=-1, keepdims=True)
        y = xc * lax.rsqrt(var + LN_EPS) * g_ref[...] + beta_ref[...]
        o_ref[r0:r0 + CONV_ROWS, :] = (y * jax.nn.sigmoid(y)).astype(BF16)


def _l0_conv(a, w, b, g, beta):
    prev, nxt = _halo_specs(HALF)
    return pl.pallas_call(
        _l0_conv_body, name="l0_conv",
        grid=(N_TILES,),
        in_specs=[prev, _tile_spec(HALF), nxt, _full_spec((CONV_W + 1, HALF)),
                  _full_spec((1, HALF)), _full_spec((1, HALF)), _full_spec((1, HALF))],
        out_specs=_tile_spec(HALF),
        out_shape=jax.ShapeDtypeStruct((T_ALL, HALF), BF16),
        scratch_shapes=[pltpu.VMEM((TB + 2 * HALO, HALF), F32)],
    )(a, a, a, w, b, g, beta)


def _dft1_body(d_ref, x_ref, y_ref, *, n1):
    y = _dot(d_ref[...], x_ref[0])
    y_ref[0] = y.reshape(2, n1, y.shape[-1]).astype(BF16)


def _dft1(f_all, d1, *, n1, batches, chunk0, colblk):
    cols = FFT_N2 * HALF
    x = f_all.reshape(T_ALL // (n1 * FFT_N2), n1, cols)
    return pl.pallas_call(
        functools.partial(_dft1_body, n1=n1), name=f"dft1_n{n1}",
        grid=(batches, cols // colblk),
        in_specs=[_full_spec((2 * n1, n1)),
                  pl.BlockSpec((1, n1, colblk), lambda b, j: (b + chunk0, 0, j))],
        out_specs=pl.BlockSpec((1, 2, n1, colblk), lambda b, j: (b, 0, 0, j)),
        out_shape=jax.ShapeDtypeStruct((batches, 2, n1, cols), BF16),
    )(d1, x)


def _dft2_body(l_ref, y_ref, c_ref, s_ref):
    for q in range(FFT_KB):
        yq = y_ref[0, :, q].reshape(2 * FFT_N2, HALF)
        r = _dot(l_ref[q], yq)
        c_ref[0, :, q * HALF:(q + 1) * HALF] = r[:FFT_N2].astype(BF16)
        s_ref[0, :, q * HALF:(q + 1) * HALF] = r[FFT_N2:].astype(BF16)


def _dft2(y, l2, *, n1, batches):
    y5 = y.reshape(batches, 2, n1, FFT_N2, HALF)
    out_sds = jax.ShapeDtypeStruct((batches, FFT_N2, n1 * HALF), BF16)
    out_spec = pl.BlockSpec((1, FFT_N2, FFT_KB * HALF), lambda b, k: (b, 0, k))
    c, s = pl.pallas_call(
        _dft2_body, name=f"dft2_n{n1}",
        grid=(batches, n1 // FFT_KB),
        in_specs=[pl.BlockSpec((FFT_KB, 2 * FFT_N2, 2 * FFT_N2), lambda b, k: (k, 0, 0)),
                  pl.BlockSpec((1, 2, FFT_KB, FFT_N2, HALF), lambda b, k: (b, 0, k, 0, 0))],
        out_specs=[out_spec, out_spec],
        out_shape=[out_sds, out_sds],
    )(l2, y5)
    tokens = batches * n1 * FFT_N2
    return c.reshape(tokens, HALF), s.reshape(tokens, HALF)


def _dft_tables(n1):
    n = n1 * FFT_N2
    k1 = jnp.arange(n1, dtype=jnp.int32)
    ang1 = (2.0 * math.pi / n1) * ((k1[:, None] * k1[None, :]) % n1).astype(F32)
    d1 = jnp.concatenate([jnp.cos(ang1), -jnp.sin(ang1)], axis=0).astype(BF16)
    k2 = jnp.arange(FFT_N2, dtype=jnp.int32)
    k = k1[:, None, None] + n1 * k2[None, :, None]
    ang2 = (2.0 * math.pi / n) * ((k * k2[None, None, :]) % n).astype(F32)
    cr = jnp.cos(ang2) * (n ** -0.5)
    sr = jnp.sin(ang2) * (n ** -0.5)
    top = jnp.concatenate([cr, sr], axis=2)
    bot = jnp.concatenate([sr, -cr], axis=2)
    l2 = jnp.concatenate([top, bot], axis=1).astype(BF16)
    return d1, l2


def _fold_body(m_ref, w_ref, o_ref):
    o_ref[...] = jnp.dot(m_ref[...], w_ref[...], preferred_element_type=F32,
                         precision=lax.Precision.HIGHEST).astype(BF16)


def _fold_channel_dft(w_out_f):
    c = jnp.arange(HALF, dtype=jnp.int32)
    same = (c[:, None] // GROUP_CH) == (c[None, :] // GROUP_CH)
    ang = (2.0 * math.pi / GROUP_CH) * ((c[:, None] * c[None, :]) % GROUP_CH).astype(F32)
    scale = GROUP_CH ** -0.5
    cbd = jnp.where(same, jnp.cos(ang), 0.0) * scale
    sbd = jnp.where(same, jnp.sin(ang), 0.0) * scale
    m = jnp.concatenate([cbd, -sbd], axis=0)
    return pl.pallas_call(
        _fold_body, name="fold_channel_dft",
        out_shape=jax.ShapeDtypeStruct((2 * HALF, D), BF16),
    )(m, w_out_f)


def _route_and_store(x_new, g_ref, wr_ref, br_ref, x_out_ref, row_ref):
    x_out_ref[...] = x_new
    h = _rms(x_new, g_ref[...])
    h_hi = h.astype(BF16)
    h_lo = (h - h_hi.astype(F32)).astype(BF16)
    p = _dot(h_hi, wr_ref[...])
    q = _dot(h_lo, wr_ref[...])
    lg = p[:, :META] + p[:, META:] + q[:, :META] + q[:, META:] + br_ref[...]
    lane = lax.broadcasted_iota(jnp.int32, lg.shape, 1).astype(F32)
    neg = -jnp.inf
    lc = jnp.where(lane < N_GROUPS, lg, neg)
    mc = jnp.max(lc, axis=-1, keepdims=True)
    grp = jnp.min(jnp.where(lc == mc, lane, META), axis=-1, keepdims=True)
    p_grp = 1.0 / jnp.sum(jnp.exp(lc - mc), axis=-1, keepdims=True)
    lo = N_GROUPS + EPG * grp
    lf = jnp.where((lane >= lo) & (lane < lo + EPG), lg, neg)
    m1 = jnp.max(lf, axis=-1, keepdims=True)
    i1 = jnp.min(jnp.where(lf == m1, lane, META), axis=-1, keepdims=True)
    lf2 = jnp.where(lane == i1, neg, lf)
    m2 = jnp.max(lf2, axis=-1, keepdims=True)
    i2 = jnp.min(jnp.where(lf2 == m2, lane, META), axis=-1, keepdims=True)
    e = jnp.exp(m2 - m1)
    gate1 = p_grp / (1.0 + e)
    gate2 = p_grp * e / (1.0 + e)
    j1 = i1 - lo
    j2 = i2 - lo
    ja = jnp.minimum(j1, j2)
    jb = jnp.maximum(j1, j2)
    bucket = grp * N_PAIRS + (ja * (2 * EPG - 1 - ja)) * 0.5 + (jb - ja - 1)
    gate_a = jnp.where(j1 < j2, gate1, gate2)
    gate_b = jnp.where(j1 < j2, gate2, gate1)
    meta = jnp.where(lane == 0, gate_a,
                     jnp.where(lane == 1, gate_b,
                               jnp.where(lane == 2, bucket, 0.0)))
    row_ref[:, :D] = h
    row_ref[:, D:] = meta


def _router_operands(w_coarse, b_coarse, w_fine, b_fine):
    w = jnp.concatenate([w_coarse, w_fine.reshape(D, N_GROUPS * EPG)], axis=1)
    w = jnp.pad(w, ((0, 0), (0, META - w.shape[1])))
    w_hi = w.astype(BF16)
    w_lo = (w - w_hi.astype(F32)).astype(BF16)
    b = jnp.concatenate([b_coarse, b_fine.reshape(-1)])
    b = jnp.pad(b, (0, META - b.shape[0])).reshape(1, META)
    return jnp.concatenate([w_hi, w_lo], axis=1), b


def _l0_out_body(a_ref, cp_ref, sp_ref, cs_ref, ss_ref, xp_ref, xs_ref, wa_ref, wc_ref, ws_ref,
                 g_ref, wr_ref, br_ref, x_out_ref, row_ref):
    in_prompt = pl.program_id(0) < N_PROMPT_TILES
    x = jnp.where(in_prompt, xp_ref[...], xs_ref[...])
    c = jnp.where(in_prompt, cp_ref[...], cs_ref[...])
    s = jnp.where(in_prompt, sp_ref[...], ss_ref[...])
    mix = _dot(a_ref[...], wa_ref[...]) + _dot(c, wc_ref[...]) + _dot(s, ws_ref[...])
    _route_and_store(x + mix, g_ref, wr_ref, br_ref, x_out_ref, row_ref)


def _l0_out(a2, cs_prompt, cs_sample, xp, xs, wa, wc, ws, g, wr, br):
    return pl.pallas_call(
        _l0_out_body, name="l0_out",
        grid=(N_TILES,),
        in_specs=[_tile_spec(HALF), _prompt_spec(HALF), _prompt_spec(HALF), _sample_spec(HALF), _sample_spec(HALF),
                  _prompt_spec(), _sample_spec(),
                  _full_spec((HALF, D)), _full_spec((HALF, D)), _full_spec((HALF, D)),
                  _full_spec((1, D)), _full_spec((D, 2 * META)), _full_spec((1, META))],
        out_specs=[_tile_spec(D), _tile_spec(ROW_W)],
        out_shape=[jax.ShapeDtypeStruct((T_ALL, D), F32), jax.ShapeDtypeStruct((T_ALL, ROW_W), F32)],
    )(a2, *cs_prompt, *cs_sample, xp, xs, wa, wc, ws, g, wr, br)


def _moe_plan(bucket):
    order = jnp.argsort(bucket, stable=True).astype(jnp.int32)
    sorted_b = bucket[order]
    starts = jnp.searchsorted(sorted_b, jnp.arange(N_BUCKETS + 1, dtype=jnp.int32), side='left').astype(jnp.int32)
    counts = starts[1:] - starts[:-1]
    nblk = (counts + MOE_ROWS - 1) // MOE_ROWS
    cum = jnp.cumsum(nblk)
    first = cum - nblk
    j = jnp.arange(N_MOE_BLOCKS, dtype=jnp.int32)
    valid = j < cum[-1]
    j_eff = jnp.minimum(j, cum[-1] - 1)
    bj = jnp.minimum(jnp.searchsorted(cum, j_eff, side='right'), N_BUCKETS - 1).astype(jnp.int32)
    off = (j_eff - first[bj]) * MOE_ROWS
    cnt = jnp.where(valid, jnp.clip(counts[bj] - off, 0, MOE_ROWS), 0).astype(jnp.int32)
    pos = jnp.clip(starts[bj][:, None] + off[:, None] + jnp.arange(MOE_ROWS, dtype=jnp.int32)[None, :], 0, T_ALL - 1)
    idx = order[pos].reshape(N_MOE_BLOCKS, 1, MOE_ROWS)
    pa, pb = np.triu_indices(EPG, k=1)
    grp = bj // N_PAIRS
    ea = grp * EPG + jnp.asarray(pa, jnp.int32)[bj % N_PAIRS]
    eb = grp * EPG + jnp.asarray(pb, jnp.int32)[bj % N_PAIRS]
    return ea.astype(jnp.int32), eb.astype(jnp.int32), cnt, idx


def _moe_body(ea_ref, eb_ref, cnt_ref, idx_ref, idx_next_ref, rows_hbm,
              wga_ref, wua_ref, wda_ref, wgb_ref, wub_ref, wdb_ref,
              y_hbm, xbuf, ybuf, gsem, ssem):
    j = pl.program_id(0)
    nb = pl.num_programs(0)
    slot = j % 2
    other = 1 - slot
    n = cnt_ref[j]

    def row_in(tok, r, s):
        return pltpu.make_async_copy(rows_hbm.at[pl.ds(tok, 1)], xbuf.at[s, pl.ds(r, 1)], gsem.at[s])

    def row_out(tok, r, s):
        return pltpu.make_async_copy(ybuf.at[s, pl.ds(r, 1)], y_hbm.at[pl.ds(tok, 1)], ssem.at[s])

    def start_gather(ids_ref, count, s):
        def body(r, carry):
            row_in(ids_ref[0, 0, r], r, s).start()
            return carry
        lax.fori_loop(0, count, body, 0)

    def wait_rows(make, count, s):
        def body(r, carry):
            make(0, 0, s).wait()
            return carry
        lax.fori_loop(0, count, body, 0)

    @pl.when(j == 0)
    def _():
        xbuf[...] = jnp.zeros(xbuf.shape, xbuf.dtype)
        start_gather(idx_ref, n, slot)

    @pl.when(j + 1 < nb)
    def _():
        start_gather(idx_next_ref, cnt_ref[jnp.minimum(j + 1, nb - 1)], other)

    wait_rows(row_in, n, slot)

    @pl.when(j >= 2)
    def _():
        wait_rows(row_out, cnt_ref[jnp.maximum(j - 2, 0)], slot)

    @pl.when(n > 0)
    def _():
        xb = xbuf[slot]
        x = xb[:, :D].astype(BF16)
        gate_a = xb[:, D:D + 1]
        gate_b = xb[:, D + 1:D + 2]

        def hidden(wg_ref, wu_ref, gate):
            g = _dot(x, wg_ref[0])
            u = _dot(x, wu_ref[0])
            return (g * jax.nn.sigmoid(g) * u * gate).astype(BF16)

        y = _dot(hidden(wga_ref, wua_ref, gate_a), wda_ref[0]) + _dot(hidden(wgb_ref, wub_ref, gate_b), wdb_ref[0])
        ybuf[slot] = y

        def body(r, carry):
            row_out(idx_ref[0, 0, r], r, slot).start()
            return carry
        lax.fori_loop(0, n, body, 0)

    @pl.when(j == nb - 1)
    def _():
        wait_rows(row_out, cnt_ref[jnp.maximum(j - 1, 0)], other)
        wait_rows(row_out, n, slot)


def _moe(rows, plan, w_gate, w_up, w_down):
    ea, eb, cnt, idx = plan
    idx_spec = pl.BlockSpec((1, 1, MOE_ROWS), lambda j, *_: (j, 0, 0), memory_space=pltpu.SMEM)
    idx_next_spec = pl.BlockSpec((1, 1, MOE_ROWS), lambda j, *_: (jnp.minimum(j + 1, N_MOE_BLOCKS - 1), 0, 0),
                                 memory_space=pltpu.SMEM)
    up_a = pl.BlockSpec((1, D, D_EXPERT), lambda j, ea, eb, cnt: (ea[j], 0, 0))
    up_b = pl.BlockSpec((1, D, D_EXPERT), lambda j, ea, eb, cnt: (eb[j], 0, 0))
    down_a = pl.BlockSpec((1, D_EXPERT, D), lambda j, ea, eb, cnt: (ea[j], 0, 0))
    down_b = pl.BlockSpec((1, D_EXPERT, D), lambda j, ea, eb, cnt: (eb[j], 0, 0))
    grid_spec = pltpu.PrefetchScalarGridSpec(
        num_scalar_prefetch=3,
        grid=(N_MOE_BLOCKS,),
        in_specs=[idx_spec, idx_next_spec, pl.BlockSpec(memory_space=pl.ANY),
                  up_a, up_a, down_a, up_b, up_b, down_b],
        out_specs=pl.BlockSpec(memory_space=pl.ANY),
        scratch_shapes=[pltpu.VMEM((2, MOE_ROWS, ROW_W), F32), pltpu.VMEM((2, MOE_ROWS, D), F32),
                        pltpu.SemaphoreType.DMA((2,)), pltpu.SemaphoreType.DMA((2,))],
    )
    return pl.pallas_call(
        _moe_body, name="moe",
        grid_spec=grid_spec,
        out_shape=jax.ShapeDtypeStruct((T_ALL, D), F32),
        compiler_params=pltpu.CompilerParams(dimension_semantics=("arbitrary",)),
    )(ea, eb, cnt, idx, idx, rows, w_gate, w_up, w_down, w_gate, w_up, w_down)


def _l1_in_body(x_ref, y_ref, g_ref, w_ref, x_out_ref, gate_ref, cv_ref):
    x = x_ref[...] + y_ref[...]
    x_out_ref[...] = x
    u = _dot(_rms(x, g_ref[...]).astype(BF16), w_ref[...])
    gate_ref[...] = u[:, :D].astype(BF16)
    cv_ref[...] = (u[:, D:2 * D] * u[:, 2 * D:]).astype(BF16)


def _l1_in(x, y, g, w_in):
    return pl.pallas_call(
        _l1_in_body, name="l1_in",
        grid=(N_TILES,),
        in_specs=[_tile_spec(D), _tile_spec(D), _full_spec((1, D)), _full_spec((D, 3 * D))],
        out_specs=[_tile_spec(D), _tile_spec(D), _tile_spec(D)],
        out_shape=[jax.ShapeDtypeStruct((T_ALL, D), F32), jax.ShapeDtypeStruct((T_ALL, D), BF16),
                   jax.ShapeDtypeStruct((T_ALL, D), BF16)],
    )(x, y, g, w_in)


def _l1_out_body(prev_ref, cur_ref, next_ref, gate_ref, x_ref, cw_ref, wo_ref, g_ref, wr_ref, br_ref,
                 x_out_ref, row_ref, buf, ybuf):
    starts, ends = _seq_edges(pl.program_id(0))
    buf[0:HALO, :] = jnp.where(starts, 0.0, prev_ref[...].astype(F32))
    buf[HALO:HALO + TB, :] = cur_ref[...].astype(F32)
    buf[HALO + TB:, :] = jnp.where(ends, 0.0, next_ref[...].astype(F32))
    rows = CONV_ROWS // 2
    for r0 in range(0, TB, rows):
        conv = (cw_ref[0:1, :] * buf[r0 + HALO - 1:r0 + HALO - 1 + rows, :]
                + cw_ref[1:2, :] * buf[r0 + HALO:r0 + HALO + rows, :]
                + cw_ref[2:3, :] * buf[r0 + HALO + 1:r0 + HALO + 1 + rows, :])
        ybuf[r0:r0 + rows, :] = (gate_ref[r0:r0 + rows, :].astype(F32) * conv).astype(BF16)
    _route_and_store(x_ref[...] + _dot(ybuf[...], wo_ref[...]), g_ref, wr_ref, br_ref, x_out_ref, row_ref)


def _l1_out(cv, gate, x, conv_w, w_out, g, wr, br):
    prev, nxt = _halo_specs(D)
    return pl.pallas_call(
        _l1_out_body, name="l1_out",
        grid=(N_TILES,),
        in_specs=[prev, _tile_spec(D), nxt, _tile_spec(D), _tile_spec(D), _full_spec((8, D)),
                  _full_spec((D, D)), _full_spec((1, D)), _full_spec((D, 2 * META)), _full_spec((1, META))],
        out_specs=[_tile_spec(D), _tile_spec(ROW_W)],
        out_shape=[jax.ShapeDtypeStruct((T_ALL, D), F32), jax.ShapeDtypeStruct((T_ALL, ROW_W), F32)],
        scratch_shapes=[pltpu.VMEM((TB + 2 * HALO, D), F32), pltpu.VMEM((TB, D), BF16)],
    )(cv, cv, cv, gate, x, conv_w, w_out, g, wr, br)


def _final_body(x_ref, y_ref, g_ref, o_ref):
    o_ref[...] = _rms(x_ref[...] + y_ref[...], g_ref[...])


def _final(x, y, g, *, tile0, tiles):
    in_tile = pl.BlockSpec((TB, D), lambda i: (i + tile0, 0))
    return pl.pallas_call(
        _final_body, name="final_norm",
        grid=(tiles,),
        in_specs=[in_tile, in_tile, _full_spec((1, D))],
        out_specs=_tile_spec(D),
        out_shape=jax.ShapeDtypeStruct((tiles * TB, D), F32),
    )(x, y, g)


def kernel(x_prompt, x_sample, l0_norm_mix, l0_w_in, l0_conv_w, l0_conv_b, l0_ln_g, l0_ln_b, l0_w_out, l0_norm_ffn, l0_w_coarse, l0_b_coarse, l0_w_fine, l0_b_fine, l0_w_gate, l0_w_up, l0_w_down, l1_norm_mix, l1_w_in, l1_conv_w, l1_w_out, l1_norm_ffn, l1_w_coarse, l1_b_coarse, l1_w_fine, l1_b_fine, l1_w_gate, l1_w_up, l1_w_down, final_norm):
    xp = x_prompt.reshape(T_PROMPT, D)
    xs = x_sample.reshape(T_SAMPLE, D)
    row = lambda v: v.reshape(1, -1)

    a, f = _l0_in(xp, xs, row(l0_norm_mix), l0_w_in.astype(BF16))
    conv_w = jnp.pad(l0_conv_w, ((0, 1), (0, 0)))
    a2 = _l0_conv(a, conv_w, row(l0_conv_b), row(l0_ln_g), row(l0_ln_b))
    n1_prompt = T_PROMPT // FFT_N2
    n1_sample = SEQ_SAMPLE // FFT_N2
    d1p, l2p = _dft_tables(n1_prompt)
    d1s, l2s = _dft_tables(n1_sample)
    yp = _dft1(f, d1p, n1=n1_prompt, batches=1, chunk0=0, colblk=4096)
    ys = _dft1(f, d1s, n1=n1_sample, batches=T_SAMPLE // SEQ_SAMPLE, chunk0=T_PROMPT // SEQ_SAMPLE, colblk=16384)
    cs_prompt = _dft2(yp, l2p, n1=n1_prompt, batches=1)
    cs_sample = _dft2(ys, l2s, n1=n1_sample, batches=T_SAMPLE // SEQ_SAMPLE)
    w_cs = _fold_channel_dft(l0_w_out[HALF:])
    wr0, br0 = _router_operands(l0_w_coarse, l0_b_coarse, l0_w_fine, l0_b_fine)
    x1, rows0 = _l0_out(a2, cs_prompt, cs_sample, xp, xs, l0_w_out[:HALF].astype(BF16), w_cs[:HALF], w_cs[HALF:],
                        row(l0_norm_ffn), wr0, br0)

    plan0 = _moe_plan(rows0[:, D + 2].astype(jnp.int32))
    y0 = _moe(rows0, plan0, l0_w_gate.astype(BF16), l0_w_up.astype(BF16), l0_w_down.astype(BF16))

    x2, gate, cv = _l1_in(x1, y0, row(l1_norm_mix), l1_w_in.astype(BF16))
    wr1, br1 = _router_operands(l1_w_coarse, l1_b_coarse, l1_w_fine, l1_b_fine)
    x3, rows1 = _l1_out(cv, gate, x2, jnp.pad(l1_conv_w, ((0, 5), (0, 0))), l1_w_out.astype(BF16),
                        row(l1_norm_ffn), wr1, br1)

    plan1 = _moe_plan(rows1[:, D + 2].astype(jnp.int32))
    y1 = _moe(rows1, plan1, l1_w_gate.astype(BF16), l1_w_up.astype(BF16), l1_w_down.astype(BF16))

    out_p = _final(x3, y1, row(final_norm), tile0=0, tiles=N_PROMPT_TILES)
    out_s = _final(x3, y1, row(final_norm), tile0=N_PROMPT_TILES, tiles=N_TILES - N_PROMPT_TILES)
    return out_p.reshape(x_prompt.shape), out_s.reshape(x_sample.shape)
```

```python
import functools
import math

import jax
import jax.numpy as jnp
import numpy as np
from jax import lax
from jax.experimental import pallas as pl
from jax.experimental.pallas import tpu as pltpu

D = 1024
T_PROMPT = 16384
SEQ_SAMPLE = 2048
T_SAMPLE = 32 * SEQ_SAMPLE
T_ALL = T_PROMPT + T_SAMPLE
HALF = 512
GROUP_CH = 64
CONV_W = 31
N_GROUPS = 4
EPG = 8
N_PAIRS = EPG * (EPG - 1) // 2
N_BUCKETS = N_GROUPS * N_PAIRS
D_EXPERT = 512
RMS_EPS = 1e-6
LN_EPS = 1e-5

TB = 512
N_TILES = T_ALL // TB
N_PROMPT_TILES = T_PROMPT // TB
HALO = 16
CONV_ROWS = 64
MOE_ROWS = 128
N_MOE_BLOCKS = T_ALL // MOE_ROWS + N_BUCKETS + 1
META = 128
ROW_W = D + META
FFT_N2 = 128
FFT_KB = 4

BF16 = jnp.bfloat16
F32 = jnp.float32


def _rms(x, g):
    return x * lax.rsqrt(jnp.mean(x * x, axis=-1, keepdims=True) + RMS_EPS) * g


def _dot(a, b):
    return jnp.dot(a, b, preferred_element_type=F32)


def _seq_edges(i):
    r0 = i * TB
    r1 = r0 + TB
    in_prompt = r0 < T_PROMPT
    starts = jnp.where(in_prompt, r0 == 0, (r0 - T_PROMPT) % SEQ_SAMPLE == 0)
    ends = jnp.where(in_prompt, r1 == T_PROMPT, (r1 - T_PROMPT) % SEQ_SAMPLE == 0)
    return starts, ends


def _tile_spec(width):
    return pl.BlockSpec((TB, width), lambda i: (i, 0))


def _full_spec(shape):
    return pl.BlockSpec(shape, lambda *_: (0,) * len(shape))


def _prompt_spec(width=D):
    return pl.BlockSpec((TB, width), lambda i: (jnp.minimum(i, N_PROMPT_TILES - 1), 0))


def _sample_spec(width=D):
    return pl.BlockSpec((TB, width), lambda i: (jnp.maximum(i - N_PROMPT_TILES, 0), 0))


def _halo_specs(width):
    per_tile = TB // HALO
    last = T_ALL // HALO - 1
    prev = pl.BlockSpec((HALO, width), lambda i: (jnp.maximum(i * per_tile - 1, 0), 0))
    nxt = pl.BlockSpec((HALO, width), lambda i: (jnp.minimum((i + 1) * per_tile, last), 0))
    return prev, nxt


def _l0_in_body(xp_ref, xs_ref, g_ref, w_ref, a_ref, f_ref):
    i = pl.program_id(0)
    x = jnp.where(i < N_PROMPT_TILES, xp_ref[...], xs_ref[...])
    h = _rms(x, g_ref[...])
    u = _dot(h.astype(BF16), w_ref[...])
    a_ref[...] = u[:, :HALF] * jax.nn.sigmoid(u[:, HALF:2 * HALF])
    f_ref[...] = u[:, 2 * HALF:].astype(BF16)


def _l0_in(xp, xs, g, w_in):
    return pl.pallas_call(
        _l0_in_body, name="l0_in",
        grid=(N_TILES,),
        in_specs=[_prompt_spec(), _sample_spec(), _full_spec((1, D)), _full_spec((D, 3 * HALF))],
        out_specs=[_tile_spec(HALF), _tile_spec(HALF)],
        out_shape=[jax.ShapeDtypeStruct((T_ALL, HALF), F32), jax.ShapeDtypeStruct((T_ALL, HALF), BF16)],
    )(xp, xs, g, w_in)


def _l0_conv_body(prev_ref, cur_ref, next_ref, w_ref, b_ref, g_ref, beta_ref, o_ref, buf):
    starts, ends = _seq_edges(pl.program_id(0))
    buf[0, 0:HALO, :] = jnp.where(starts, 0.0, prev_ref[...])
    buf[0, HALO:HALO + TB, :] = cur_ref[...]
    buf[0, HALO + TB:, :] = jnp.where(ends, 0.0, next_ref[...])
    rows = TB + 2 * HALO - 8
    for s in range(1, 8):
        for c0 in range(0, rows, 128):
            c1 = min(c0 + 128, rows)
            buf[s, c0:c1, :] = buf[0, c0 + s:c1 + s, :]
    first = HALO - CONV_W // 2
    for r0 in range(0, TB, CONV_ROWS):
        acc = jnp.broadcast_to(b_ref[...], (CONV_ROWS, HALF))
        for k in range(CONV_W):
            s = (first + k) % 8
            base = r0 + first + k - s
            acc = acc + w_ref[k:k + 1, :] * buf[s, base:base + CONV_ROWS, :]
        mu = jnp.mean(acc, axis=-1, keepdims=True)
        xc = acc - mu
        var = jnp.mean(xc * xc, axis=-1, keepdims=True)
        y = xc * lax.rsqrt(var + LN_EPS) * g_ref[...] + beta_ref[...]
        o_ref[r0:r0 + CONV_ROWS, :] = (y * jax.nn.sigmoid(y)).astype(BF16)


def _l0_conv(a, w, b, g, beta):
    prev, nxt = _halo_specs(HALF)
    return pl.pallas_call(
        _l0_conv_body, name="l0_conv",
        grid=(N_TILES,),
        in_specs=[prev, _tile_spec(HALF), nxt, _full_spec((CONV_W + 1, HALF)),
                  _full_spec((1, HALF)), _full_spec((1, HALF)), _full_spec((1, HALF))],
        out_specs=_tile_spec(HALF),
        out_shape=jax.ShapeDtypeStruct((T_ALL, HALF), BF16),
        scratch_shapes=[pltpu.VMEM((8, TB + 2 * HALO, HALF), F32)],
    )(a, a, a, w, b, g, beta)


def _dft1_body(d_ref, x_ref, y_ref, *, n1):
    y = _dot(d_ref[...], x_ref[0])
    y_ref[0] = y.reshape(2, n1, y.shape[-1]).astype(BF16)


def _dft1(f_all, d1, *, n1, batches, chunk0, colblk):
    cols = FFT_N2 * HALF
    x = f_all.reshape(T_ALL // (n1 * FFT_N2), n1, cols)
    return pl.pallas_call(
        functools.partial(_dft1_body, n1=n1), name=f"dft1_n{n1}",
        grid=(batches, cols // colblk),
        in_specs=[_full_spec((2 * n1, n1)),
                  pl.BlockSpec((1, n1, colblk), lambda b, j: (b + chunk0, 0, j))],
        out_specs=pl.BlockSpec((1, 2, n1, colblk), lambda b, j: (b, 0, 0, j)),
        out_shape=jax.ShapeDtypeStruct((batches, 2, n1, cols), BF16),
    )(d1, x)


def _dft2_body(l_ref, y_ref, c_ref, s_ref):
    for q in range(FFT_KB):
        yq = y_ref[0, :, q].reshape(2 * FFT_N2, HALF)
        r = _dot(l_ref[q], yq)
        c_ref[0, :, q * HALF:(q + 1) * HALF] = r[:FFT_N2].astype(BF16)
        s_ref[0, :, q * HALF:(q + 1) * HALF] = r[FFT_N2:].astype(BF16)


def _dft2(y, l2, *, n1, batches):
    y5 = y.reshape(batches, 2, n1, FFT_N2, HALF)
    out_sds = jax.ShapeDtypeStruct((batches, FFT_N2, n1 * HALF), BF16)
    out_spec = pl.BlockSpec((1, FFT_N2, FFT_KB * HALF), lambda b, k: (b, 0, k))
    c, s = pl.pallas_call(
        _dft2_body, name=f"dft2_n{n1}",
        grid=(batches, n1 // FFT_KB),
        in_specs=[pl.BlockSpec((FFT_KB, 2 * FFT_N2, 2 * FFT_N2), lambda b, k: (k, 0, 0)),
                  pl.BlockSpec((1, 2, FFT_KB, FFT_N2, HALF), lambda b, k: (b, 0, k, 0, 0))],
        out_specs=[out_spec, out_spec],
        out_shape=[out_sds, out_sds],
    )(l2, y5)
    tokens = batches * n1 * FFT_N2
    return c.reshape(tokens, HALF), s.reshape(tokens, HALF)


def _dft_tables(n1):
    n = n1 * FFT_N2
    k1 = jnp.arange(n1, dtype=jnp.int32)
    ang1 = (2.0 * math.pi / n1) * ((k1[:, None] * k1[None, :]) % n1).astype(F32)
    d1 = jnp.concatenate([jnp.cos(ang1), -jnp.sin(ang1)], axis=0).astype(BF16)
    k2 = jnp.arange(FFT_N2, dtype=jnp.int32)
    k = k1[:, None, None] + n1 * k2[None, :, None]
    ang2 = (2.0 * math.pi / n) * ((k * k2[None, None, :]) % n).astype(F32)
    cr = jnp.cos(ang2) * (n ** -0.5)
    sr = jnp.sin(ang2) * (n ** -0.5)
    top = jnp.concatenate([cr, sr], axis=2)
    bot = jnp.concatenate([sr, -cr], axis=2)
    l2 = jnp.concatenate([top, bot], axis=1).astype(BF16)
    return d1, l2


def _fold_body(m_ref, w_ref, o_ref):
    o_ref[...] = jnp.dot(m_ref[...], w_ref[...], preferred_element_type=F32,
                         precision=lax.Precision.HIGHEST).astype(BF16)


def _fold_channel_dft(w_out_f):
    c = jnp.arange(HALF, dtype=jnp.int32)
    same = (c[:, None] // GROUP_CH) == (c[None, :] // GROUP_CH)
    ang = (2.0 * math.pi / GROUP_CH) * ((c[:, None] * c[None, :]) % GROUP_CH).astype(F32)
    scale = GROUP_CH ** -0.5
    cbd = jnp.where(same, jnp.cos(ang), 0.0) * scale
    sbd = jnp.where(same, jnp.sin(ang), 0.0) * scale
    m = jnp.concatenate([cbd, -sbd], axis=0)
    return pl.pallas_call(
        _fold_body, name="fold_channel_dft",
        out_shape=jax.ShapeDtypeStruct((2 * HALF, D), BF16),
    )(m, w_out_f)


def _route_and_store(x_new, g_ref, wr_ref, br_ref, x_out_ref, row_ref):
    x_out_ref[...] = x_new
    h = _rms(x_new, g_ref[...])
    h_hi = h.astype(BF16)
    h_lo = (h - h_hi.astype(F32)).astype(BF16)
    p = _dot(h_hi, wr_ref[...])
    q = _dot(h_lo, wr_ref[...])
    lg = p[:, :META] + p[:, META:] + q[:, :META] + q[:, META:] + br_ref[...]
    lane = lax.broadcasted_iota(jnp.int32, lg.shape, 1).astype(F32)
    neg = -jnp.inf
    lc = jnp.where(lane < N_GROUPS, lg, neg)
    mc = jnp.max(lc, axis=-1, keepdims=True)
    grp = jnp.min(jnp.where(lc == mc, lane, META), axis=-1, keepdims=True)
    p_grp = 1.0 / jnp.sum(jnp.exp(lc - mc), axis=-1, keepdims=True)
    lo = N_GROUPS + EPG * grp
    lf = jnp.where((lane >= lo) & (lane < lo + EPG), lg, neg)
    m1 = jnp.max(lf, axis=-1, keepdims=True)
    i1 = jnp.min(jnp.where(lf == m1, lane, META), axis=-1, keepdims=True)
    lf2 = jnp.where(lane == i1, neg, lf)
    m2 = jnp.max(lf2, axis=-1, keepdims=True)
    i2 = jnp.min(jnp.where(lf2 == m2, lane, META), axis=-1, keepdims=True)
    e = jnp.exp(m2 - m1)
    gate1 = p_grp / (1.0 + e)
    gate2 = p_grp * e / (1.0 + e)
    j1 = i1 - lo
    j2 = i2 - lo
    ja = jnp.minimum(j1, j2)
    jb = jnp.maximum(j1, j2)
    bucket = grp * N_PAIRS + (ja * (2 * EPG - 1 - ja)) * 0.5 + (jb - ja - 1)
    gate_a = jnp.where(j1 < j2, gate1, gate2)
    gate_b = jnp.where(j1 < j2, gate2, gate1)
    meta = jnp.where(lane == 0, gate_a,
                     jnp.where(lane == 1, gate_b,
                               jnp.where(lane == 2, bucket, 0.0)))
    row_ref[:, :D] = h
    row_ref[:, D:] = meta


def _router_operands(w_coarse, b_coarse, w_fine, b_fine):
    w = jnp.concatenate([w_coarse, w_fine.reshape(D, N_GROUPS * EPG)], axis=1)
    w = jnp.pad(w, ((0, 0), (0, META - w.shape[1])))
    w_hi = w.astype(BF16)
    w_lo = (w - w_hi.astype(F32)).astype(BF16)
    b = jnp.concatenate([b_coarse, b_fine.reshape(-1)])
    b = jnp.pad(b, (0, META - b.shape[0])).reshape(1, META)
    return jnp.concatenate([w_hi, w_lo], axis=1), b


def _l0_out_body(a_ref, cp_ref, sp_ref, cs_ref, ss_ref, xp_ref, xs_ref, wa_ref, wc_ref, ws_ref,
                 g_ref, wr_ref, br_ref, x_out_ref, row_ref):
    in_prompt = pl.program_id(0) < N_PROMPT_TILES
    x = jnp.where(in_prompt, xp_ref[...], xs_ref[...])
    c = jnp.where(in_prompt, cp_ref[...], cs_ref[...])
    s = jnp.where(in_prompt, sp_ref[...], ss_ref[...])
    mix = _dot(a_ref[...], wa_ref[...]) + _dot(c, wc_ref[...]) + _dot(s, ws_ref[...])
    _route_and_store(x + mix, g_ref, wr_ref, br_ref, x_out_ref, row_ref)


def _l0_out(a2, cs_prompt, cs_sample, xp, xs, wa, wc, ws, g, wr, br):
    return pl.pallas_call(
        _l0_out_body, name="l0_out",
        grid=(N_TILES,),
        in_specs=[_tile_spec(HALF), _prompt_spec(HALF), _prompt_spec(HALF), _sample_spec(HALF), _sample_spec(HALF),
                  _prompt_spec(), _sample_spec(),
                  _full_spec((HALF, D)), _full_spec((HALF, D)), _full_spec((HALF, D)),
                  _full_spec((1, D)), _full_spec((D, 2 * META)), _full_spec((1, META))],
        out_specs=[_tile_spec(D), _tile_spec(ROW_W)],
        out_shape=[jax.ShapeDtypeStruct((T_ALL, D), F32), jax.ShapeDtypeStruct((T_ALL, ROW_W), F32)],
    )(a2, *cs_prompt, *cs_sample, xp, xs, wa, wc, ws, g, wr, br)


def _moe_plan(bucket):
    order = jnp.argsort(bucket, stable=True).astype(jnp.int32)
    sorted_b = bucket[order]
    starts = jnp.searchsorted(sorted_b, jnp.arange(N_BUCKETS + 1, dtype=jnp.int32), side='left').astype(jnp.int32)
    counts = starts[1:] - starts[:-1]
    nblk = (counts + MOE_ROWS - 1) // MOE_ROWS
    cum = jnp.cumsum(nblk)
    first = cum - nblk
    j = jnp.arange(N_MOE_BLOCKS, dtype=jnp.int32)
    valid = j < cum[-1]
    j_eff = jnp.minimum(j, cum[-1] - 1)
    bj = jnp.minimum(jnp.searchsorted(cum, j_eff, side='right'), N_BUCKETS - 1).astype(jnp.int32)
    off = (j_eff - first[bj]) * MOE_ROWS
    cnt = jnp.where(valid, jnp.clip(counts[bj] - off, 0, MOE_ROWS), 0).astype(jnp.int32)
    pos = jnp.clip(starts[bj][:, None] + off[:, None] + jnp.arange(MOE_ROWS, dtype=jnp.int32)[None, :], 0, T_ALL - 1)
    idx = order[pos].reshape(N_MOE_BLOCKS, 1, MOE_ROWS)
    pa, pb = np.triu_indices(EPG, k=1)
    grp = bj // N_PAIRS
    ea = grp * EPG + jnp.asarray(pa, jnp.int32)[bj % N_PAIRS]
    eb = grp * EPG + jnp.asarray(pb, jnp.int32)[bj % N_PAIRS]
    return ea.astype(jnp.int32), eb.astype(jnp.int32), cnt, idx


def _moe_body(ea_ref, eb_ref, cnt_ref, idx_ref, idx_next_ref, rows_hbm,
              wga_ref, wua_ref, wda_ref, wgb_ref, wub_ref, wdb_ref,
              y_hbm, xbuf, ybuf, gsem, ssem):
    j = pl.program_id(0)
    nb = pl.num_programs(0)
    slot = j % 2
    other = 1 - slot
    n = cnt_ref[j]
    n_prev = cnt_ref[jnp.maximum(j - 1, 0)]
    n_prev2 = cnt_ref[jnp.maximum(j - 2, 0)]

    def start_gather(ids_ref, s):
        for r in range(MOE_ROWS):
            pltpu.make_async_copy(rows_hbm.at[pl.ds(ids_ref[0, 0, r], 1)], xbuf.at[s, pl.ds(r, 1)],
                                  gsem.at[s]).start()

    def wait_gather(s):
        pltpu.make_async_copy(rows_hbm.at[pl.ds(0, MOE_ROWS)], xbuf.at[s], gsem.at[s]).wait()

    def wait_scatter(s):
        pltpu.make_async_copy(ybuf.at[s], y_hbm.at[pl.ds(0, MOE_ROWS)], ssem.at[s]).wait()

    @pl.when(j == 0)
    def _():
        start_gather(idx_ref, 0)
        ybuf[...] = jnp.zeros(ybuf.shape, ybuf.dtype)
        for s in range(2):
            clear = pltpu.make_async_copy(ybuf.at[s], y_hbm.at[pl.ds(T_ALL + s * MOE_ROWS, MOE_ROWS)], ssem.at[s])
            clear.start()
            clear.wait()

    @pl.when((j == 0) | (n_prev > 0))
    def _():
        wait_gather(slot)

    @pl.when((j >= 2) & (n_prev2 > 0))
    def _():
        wait_scatter(slot)

    @pl.when(n > 0)
    def _():
        start_gather(idx_next_ref, other)
        xb = xbuf[slot]
        x = xb[:, :D].astype(BF16)
        gate_a = xb[:, D:D + 1]
        gate_b = xb[:, D + 1:D + 2]

        def hidden(wg_ref, wu_ref, gate):
            g = _dot(x, wg_ref[0])
            u = _dot(x, wu_ref[0])
            return (g * jax.nn.sigmoid(g) * u * gate).astype(BF16)

        y = _dot(hidden(wga_ref, wua_ref, gate_a), wda_ref[0]) + _dot(hidden(wgb_ref, wub_ref, gate_b), wdb_ref[0])
        ybuf[slot] = y
        spare = T_ALL + slot * MOE_ROWS
        for r in range(MOE_ROWS):
            dest = jnp.where(r < n, idx_ref[0, 0, r], spare + r)
            pltpu.make_async_copy(ybuf.at[slot, pl.ds(r, 1)], y_hbm.at[pl.ds(dest, 1)], ssem.at[slot]).start()

    @pl.when(j == nb - 1)
    def _():
        @pl.when(n_prev > 0)
        def _():
            wait_scatter(other)

        @pl.when(n > 0)
        def _():
            wait_scatter(slot)


def _moe(rows, plan, w_gate, w_up, w_down):
    ea, eb, cnt, idx = plan
    idx_spec = pl.BlockSpec((1, 1, MOE_ROWS), lambda j, *_: (j, 0, 0), memory_space=pltpu.SMEM)
    idx_next_spec = pl.BlockSpec((1, 1, MOE_ROWS), lambda j, *_: (jnp.minimum(j + 1, N_MOE_BLOCKS - 1), 0, 0),
                                 memory_space=pltpu.SMEM)
    up_a = pl.BlockSpec((1, D, D_EXPERT), lambda j, ea, eb, cnt: (ea[j], 0, 0))
    up_b = pl.BlockSpec((1, D, D_EXPERT), lambda j, ea, eb, cnt: (eb[j], 0, 0))
    down_a = pl.BlockSpec((1, D_EXPERT, D), lambda j, ea, eb, cnt: (ea[j], 0, 0))
    down_b = pl.BlockSpec((1, D_EXPERT, D), lambda j, ea, eb, cnt: (eb[j], 0, 0))
    grid_spec = pltpu.PrefetchScalarGridSpec(
        num_scalar_prefetch=3,
        grid=(N_MOE_BLOCKS,),
        in_specs=[idx_spec, idx_next_spec, pl.BlockSpec(memory_space=pl.ANY),
                  up_a, up_a, down_a, up_b, up_b, down_b],
        out_specs=pl.BlockSpec(memory_space=pl.ANY),
        scratch_shapes=[pltpu.VMEM((2, MOE_ROWS, ROW_W), F32), pltpu.VMEM((2, MOE_ROWS, D), F32),
                        pltpu.SemaphoreType.DMA((2,)), pltpu.SemaphoreType.DMA((2,))],
    )
    return pl.pallas_call(
        _moe_body, name="moe",
        grid_spec=grid_spec,
        out_shape=jax.ShapeDtypeStruct((T_ALL + 2 * MOE_ROWS, D), F32),
        compiler_params=pltpu.CompilerParams(dimension_semantics=("arbitrary",)),
    )(ea, eb, cnt, idx, idx, rows, w_gate, w_up, w_down, w_gate, w_up, w_down)


def _l1_in_body(x_ref, y_ref, g_ref, w_ref, x_out_ref, gate_ref, cv_ref):
    x = x_ref[...] + y_ref[...]
    x_out_ref[...] = x
    u = _dot(_rms(x, g_ref[...]).astype(BF16), w_ref[...])
    gate_ref[...] = u[:, :D].astype(BF16)
    cv_ref[...] = (u[:, D:2 * D] * u[:, 2 * D:]).astype(BF16)


def _l1_in(x, y, g, w_in):
    return pl.pallas_call(
        _l1_in_body, name="l1_in",
        grid=(N_TILES,),
        in_specs=[_tile_spec(D), _tile_spec(D), _full_spec((1, D)), _full_spec((D, 3 * D))],
        out_specs=[_tile_spec(D), _tile_spec(D), _tile_spec(D)],
        out_shape=[jax.ShapeDtypeStruct((T_ALL, D), F32), jax.ShapeDtypeStruct((T_ALL, D), BF16),
                   jax.ShapeDtypeStruct((T_ALL, D), BF16)],
    )(x, y, g, w_in)


def _l1_out_body(prev_ref, cur_ref, next_ref, gate_ref, x_ref, cw_ref, wo_ref, g_ref, wr_ref, br_ref,
                 x_out_ref, row_ref, buf, ybuf):
    starts, ends = _seq_edges(pl.program_id(0))
    buf[0:HALO, :] = jnp.where(starts, 0.0, prev_ref[...].astype(F32))
    buf[HALO:HALO + TB, :] = cur_ref[...].astype(F32)
    buf[HALO + TB:, :] = jnp.where(ends, 0.0, next_ref[...].astype(F32))
    rows = CONV_ROWS // 2
    for r0 in range(0, TB, rows):
        conv = (cw_ref[0:1, :] * buf[r0 + HALO - 1:r0 + HALO - 1 + rows, :]
                + cw_ref[1:2, :] * buf[r0 + HALO:r0 + HALO + rows, :]
                + cw_ref[2:3, :] * buf[r0 + HALO + 1:r0 + HALO + 1 + rows, :])
        ybuf[r0:r0 + rows, :] = (gate_ref[r0:r0 + rows, :].astype(F32) * conv).astype(BF16)
    _route_and_store(x_ref[...] + _dot(ybuf[...], wo_ref[...]), g_ref, wr_ref, br_ref, x_out_ref, row_ref)


def _l1_out(cv, gate, x, conv_w, w_out, g, wr, br):
    prev, nxt = _halo_specs(D)
    return pl.pallas_call(
        _l1_out_body, name="l1_out",
        grid=(N_TILES,),
        in_specs=[prev, _tile_spec(D), nxt, _tile_spec(D), _tile_spec(D), _full_spec((8, D)),
                  _full_spec((D, D)), _full_spec((1, D)), _full_spec((D, 2 * META)), _full_spec((1, META))],
        out_specs=[_tile_spec(D), _tile_spec(ROW_W)],
        out_shape=[jax.ShapeDtypeStruct((T_ALL, D), F32), jax.ShapeDtypeStruct((T_ALL, ROW_W), F32)],
        scratch_shapes=[pltpu.VMEM((TB + 2 * HALO, D), F32), pltpu.VMEM((TB, D), BF16)],
    )(cv, cv, cv, gate, x, conv_w, w_out, g, wr, br)


def _final_body(x_ref, y_ref, g_ref, o_ref):
    o_ref[...] = _rms(x_ref[...] + y_ref[...], g_ref[...])


def _final(x, y, g, *, tile0, tiles):
    in_tile = pl.BlockSpec((TB, D), lambda i: (i + tile0, 0))
    return pl.pallas_call(
        _final_body, name="final_norm",
        grid=(tiles,),
        in_specs=[in_tile, in_tile, _full_spec((1, D))],
        out_specs=_tile_spec(D),
        out_shape=jax.ShapeDtypeStruct((tiles * TB, D), F32),
    )(x, y, g)


def kernel(x_prompt, x_sample, l0_norm_mix, l0_w_in, l0_conv_w, l0_conv_b, l0_ln_g, l0_ln_b, l0_w_out, l0_norm_ffn, l0_w_coarse, l0_b_coarse, l0_w_fine, l0_b_fine, l0_w_gate, l0_w_up, l0_w_down, l1_norm_mix, l1_w_in, l1_conv_w, l1_w_out, l1_norm_ffn, l1_w_coarse, l1_b_coarse, l1_w_fine, l1_b_fine, l1_w_gate, l1_w_up, l1_w_down, final_norm):
    xp = x_prompt.reshape(T_PROMPT, D)
    xs = x_sample.reshape(T_SAMPLE, D)
    row = lambda v: v.reshape(1, -1)

    a, f = _l0_in(xp, xs, row(l0_norm_mix), l0_w_in.astype(BF16))
    conv_w = jnp.pad(l0_conv_w, ((0, 1), (0, 0)))
    a2 = _l0_conv(a, conv_w, row(l0_conv_b), row(l0_ln_g), row(l0_ln_b))
    n1_prompt = T_PROMPT // FFT_N2
    n1_sample = SEQ_SAMPLE // FFT_N2
    d1p, l2p = _dft_tables(n1_prompt)
    d1s, l2s = _dft_tables(n1_sample)
    yp = _dft1(f, d1p, n1=n1_prompt, batches=1, chunk0=0, colblk=4096)
    ys = _dft1(f, d1s, n1=n1_sample, batches=T_SAMPLE // SEQ_SAMPLE, chunk0=T_PROMPT // SEQ_SAMPLE, colblk=16384)
    cs_prompt = _dft2(yp, l2p, n1=n1_prompt, batches=1)
    cs_sample = _dft2(ys, l2s, n1=n1_sample, batches=T_SAMPLE // SEQ_SAMPLE)
    w_cs = _fold_channel_dft(l0_w_out[HALF:])
    wr0, br0 = _router_operands(l0_w_coarse, l0_b_coarse, l0_w_fine, l0_b_fine)
    x1, rows0 = _l0_out(a2, cs_prompt, cs_sample, xp, xs, l0_w_out[:HALF].astype(BF16), w_cs[:HALF], w_cs[HALF:],
                        row(l0_norm_ffn), wr0, br0)

    plan0 = _moe_plan(rows0[:, D + 2].astype(jnp.int32))
    y0 = _moe(rows0, plan0, l0_w_gate.astype(BF16), l0_w_up.astype(BF16), l0_w_down.astype(BF16))

    x2, gate, cv = _l1_in(x1, y0, row(l1_norm_mix), l1_w_in.astype(BF16))
    wr1, br1 = _router_operands(l1_w_coarse, l1_b_coarse, l1_w_fine, l1_b_fine)
    x3, rows1 = _l1_out(cv, gate, x2, jnp.pad(l1_conv_w, ((0, 5), (0, 0))), l1_w_out.astype(BF16),
                        row(l1_norm_ffn), wr1, br1)

    plan1 = _moe_plan(rows1[:, D + 2].astype(jnp.int32))
    y1 = _moe(rows1, plan1, l1_w_gate.astype(BF16), l1_w_up.astype(BF16), l1_w_down.astype(BF16))

    out_p = _final(x3, y1, row(final_norm), tile0=0, tiles=N_PROMPT_TILES)
    out_s = _final(x3, y1, row(final_norm), tile0=N_PROMPT_TILES, tiles=N_TILES - N_PROMPT_TILES)
    return out_p.reshape(x_prompt.shape), out_s.reshape(x_sample.shape)
```

```python
import functools
import math

import jax
import jax.numpy as jnp
import numpy as np
from jax import lax
from jax.experimental import pallas as pl
from jax.experimental.pallas import tpu as pltpu

D = 1024
LANES = 128
T_PROMPT = 16384
SEQ_SAMPLE = 2048
T_SAMPLE = 32 * SEQ_SAMPLE
T_ALL = T_PROMPT + T_SAMPLE
HALF = 512
GROUP_CH = 64
CONV_W = 31
N_GROUPS = 4
EPG = 8
N_PAIRS = EPG * (EPG - 1) // 2
N_BUCKETS = N_GROUPS * N_PAIRS
D_EXPERT = 512
RMS_EPS = 1e-6
LN_EPS = 1e-5

TB = 512
N_TILES = T_ALL // TB
N_PROMPT_TILES = T_PROMPT // TB
HALO = 16
CONV_ROWS = 64
MOE_ROWS = 128
N_MOE_BLOCKS = T_ALL // MOE_ROWS + N_BUCKETS + 1
META = 128
ROW_TILE = D // LANES
FFT_N2 = 128
FFT_N1_PROMPT = T_PROMPT // FFT_N2
FFT_G1 = 16
FFT_KB = 8
DFT_ROWS = 512

BF16 = jnp.bfloat16
F32 = jnp.float32


def _rms(x, g):
    return x * lax.rsqrt(jnp.mean(x * x, axis=-1, keepdims=True) + RMS_EPS) * g


def _dot(a, b):
    return jnp.dot(a, b, preferred_element_type=F32)


def _seq_edges(i):
    r0 = i * TB
    r1 = r0 + TB
    in_prompt = r0 < T_PROMPT
    starts = jnp.where(in_prompt, r0 == 0, (r0 - T_PROMPT) % SEQ_SAMPLE == 0)
    ends = jnp.where(in_prompt, r1 == T_PROMPT, (r1 - T_PROMPT) % SEQ_SAMPLE == 0)
    return starts, ends


def _tile_spec(width):
    return pl.BlockSpec((TB, width), lambda i: (i, 0))


def _row_tiles_spec():
    return pl.BlockSpec((TB * ROW_TILE, LANES), lambda i: (i, 0))


def _read_row_tiles(ref):
    return jnp.concatenate([ref[pl.ds(k, TB, stride=ROW_TILE), :] for k in range(ROW_TILE)], axis=1)


def _full_spec(shape):
    return pl.BlockSpec(shape, lambda *_: (0,) * len(shape))


def _prompt_spec(width=D):
    return pl.BlockSpec((TB, width), lambda i: (jnp.minimum(i, N_PROMPT_TILES - 1), 0))


def _sample_spec(width=D):
    return pl.BlockSpec((TB, width), lambda i: (jnp.maximum(i - N_PROMPT_TILES, 0), 0))


def _halo_specs(width):
    per_tile = TB // HALO
    last = T_ALL // HALO - 1
    prev = pl.BlockSpec((HALO, width), lambda i: (jnp.maximum(i * per_tile - 1, 0), 0))
    nxt = pl.BlockSpec((HALO, width), lambda i: (jnp.minimum((i + 1) * per_tile, last), 0))
    return prev, nxt


def _l0_in_body(xp_ref, xs_ref, g_ref, w_ref, a_ref, f_ref):
    i = pl.program_id(0)
    x = jnp.where(i < N_PROMPT_TILES, xp_ref[...], xs_ref[...])
    h = _rms(x, g_ref[...])
    u = _dot(h.astype(BF16), w_ref[...])
    a_ref[...] = u[:, :HALF] * jax.nn.sigmoid(u[:, HALF:2 * HALF])
    f_ref[...] = u[:, 2 * HALF:].astype(BF16)


def _l0_in(xp, xs, g, w_in):
    return pl.pallas_call(
        _l0_in_body, name="l0_in",
        grid=(N_TILES,),
        in_specs=[_prompt_spec(), _sample_spec(), _full_spec((1, D)), _full_spec((D, 3 * HALF))],
        out_specs=[_tile_spec(HALF), _tile_spec(HALF)],
        out_shape=[jax.ShapeDtypeStruct((T_ALL, HALF), F32), jax.ShapeDtypeStruct((T_ALL, HALF), BF16)],
    )(xp, xs, g, w_in)


def _l0_conv_body(prev_ref, cur_ref, next_ref, w_ref, b_ref, g_ref, beta_ref, o_ref, buf):
    starts, ends = _seq_edges(pl.program_id(0))
    buf[0, 0:HALO, :] = jnp.where(starts, 0.0, prev_ref[...])
    buf[0, HALO:HALO + TB, :] = cur_ref[...]
    buf[0, HALO + TB:, :] = jnp.where(ends, 0.0, next_ref[...])
    rows = TB + 2 * HALO - 8
    for s in range(1, 8):
        for c0 in range(0, rows, 128):
            c1 = min(c0 + 128, rows)
            buf[s, c0:c1, :] = buf[0, c0 + s:c1 + s, :]
    first = HALO - CONV_W // 2
    for r0 in range(0, TB, CONV_ROWS):
        acc = jnp.broadcast_to(b_ref[...], (CONV_ROWS, HALF))
        for k in range(CONV_W):
            s = (first + k) % 8
            base = r0 + first + k - s
            acc = acc + w_ref[k:k + 1, :] * buf[s, base:base + CONV_ROWS, :]
        mu = jnp.mean(acc, axis=-1, keepdims=True)
        xc = acc - mu
        var = jnp.mean(xc * xc, axis=-1, keepdims=True)
        y = xc * lax.rsqrt(var + LN_EPS) * g_ref[...] + beta_ref[...]
        o_ref[r0:r0 + CONV_ROWS, :] = (y * jax.nn.sigmoid(y)).astype(BF16)


def _l0_conv(a, w, b, g, beta):
    prev, nxt = _halo_specs(HALF)
    return pl.pallas_call(
        _l0_conv_body, name="l0_conv",
        grid=(N_TILES,),
        in_specs=[prev, _tile_spec(HALF), nxt, _full_spec((CONV_W + 1, HALF)),
                  _full_spec((1, HALF)), _full_spec((1, HALF)), _full_spec((1, HALF))],
        out_specs=_tile_spec(HALF),
        out_shape=jax.ShapeDtypeStruct((T_ALL, HALF), BF16),
        scratch_shapes=[pltpu.VMEM((8, TB + 2 * HALO, HALF), F32)],
    )(a, a, a, w, b, g, beta)


def _dft1_prompt_body(d_ref, x_ref, y_ref, xs, ys):
    n1 = FFT_N1_PROMPT
    x = x_ref[0].reshape(n1 * FFT_G1, HALF).astype(F32)
    for c in range(HALF // LANES):
        xs[c] = x[:, c * LANES:(c + 1) * LANES]
    for q in range(FFT_G1):
        rows = pl.ds(q, n1, stride=FFT_G1)
        xq = jnp.concatenate([xs[c, rows, :] for c in range(HALF // LANES)], axis=1)
        y = _dot(d_ref[...], xq.astype(BF16))
        for c in range(HALF // LANES):
            ys[0, c, rows, :] = y[:n1, c * LANES:(c + 1) * LANES]
            ys[1, c, rows, :] = y[n1:, c * LANES:(c + 1) * LANES]
    for p in range(2):
        for c in range(HALF // LANES):
            y_ref[p, :, :, c * LANES:(c + 1) * LANES] = ys[p, c].reshape(n1, FFT_G1, LANES)


def _dft1_prompt(f_all, d1):
    n1 = FFT_N1_PROMPT
    x = f_all.reshape(T_ALL // T_PROMPT, n1, FFT_N2, HALF)
    return pl.pallas_call(
        _dft1_prompt_body, name="dft1_prompt",
        grid=(FFT_N2 // FFT_G1,),
        in_specs=[_full_spec((2 * n1, n1)),
                  pl.BlockSpec((1, n1, FFT_G1, HALF), lambda j: (0, 0, j, 0))],
        out_specs=pl.BlockSpec((2, n1, FFT_G1, HALF), lambda j: (0, 0, j, 0)),
        out_shape=jax.ShapeDtypeStruct((2, n1, FFT_N2, HALF), F32),
        scratch_shapes=[pltpu.VMEM((HALF // LANES, n1 * FFT_G1, LANES), F32),
                        pltpu.VMEM((2, HALF // LANES, n1 * FFT_G1, LANES), F32)],
        compiler_params=pltpu.CompilerParams(vmem_limit_bytes=48 * 1024 * 1024),
    )(d1, x)


def _dft2_prompt_body(l_ref, y_ref, c_ref, s_ref, cs, ss):
    for q in range(FFT_KB):
        yq = y_ref[:, q].reshape(2 * FFT_N2, HALF).astype(BF16)
        r = _dot(l_ref[q], yq)
        rows = pl.ds(q, FFT_N2, stride=FFT_KB)
        for c in range(HALF // LANES):
            cs[c, rows, :] = r[:FFT_N2, c * LANES:(c + 1) * LANES]
            ss[c, rows, :] = r[FFT_N2:, c * LANES:(c + 1) * LANES]
    for c in range(HALF // LANES):
        c_ref[:, :, c * LANES:(c + 1) * LANES] = cs[c].reshape(FFT_N2, FFT_KB, LANES)
        s_ref[:, :, c * LANES:(c + 1) * LANES] = ss[c].reshape(FFT_N2, FFT_KB, LANES)


def _dft2_prompt(y, l2):
    n1 = FFT_N1_PROMPT
    out_sds = jax.ShapeDtypeStruct((FFT_N2, n1, HALF), F32)
    out_spec = pl.BlockSpec((FFT_N2, FFT_KB, HALF), lambda k: (0, k, 0))
    c, s = pl.pallas_call(
        _dft2_prompt_body, name="dft2_prompt",
        grid=(n1 // FFT_KB,),
        in_specs=[pl.BlockSpec((FFT_KB, 2 * FFT_N2, 2 * FFT_N2), lambda k: (k, 0, 0)),
                  pl.BlockSpec((2, FFT_KB, FFT_N2, HALF), lambda k: (0, k, 0, 0))],
        out_specs=[out_spec, out_spec],
        out_shape=[out_sds, out_sds],
        scratch_shapes=[pltpu.VMEM((HALF // LANES, FFT_N2 * FFT_KB, LANES), F32),
                        pltpu.VMEM((HALF // LANES, FFT_N2 * FFT_KB, LANES), F32)],
    )(l2, y)
    return c.reshape(T_PROMPT, HALF), s.reshape(T_PROMPT, HALF)


def _dft_sample_body(m_ref, x_ref, o_ref):
    o_ref[0] = _dot(m_ref[...], x_ref[...]).astype(BF16)


def _dft_sample(f_all, m):
    per_half = SEQ_SAMPLE // DFT_ROWS
    first = T_PROMPT // SEQ_SAMPLE
    return pl.pallas_call(
        _dft_sample_body, name="dft_sample",
        grid=(2 * per_half, T_SAMPLE // SEQ_SAMPLE),
        in_specs=[pl.BlockSpec((DFT_ROWS, SEQ_SAMPLE), lambda r, b: (r, 0)),
                  pl.BlockSpec((SEQ_SAMPLE, HALF), lambda r, b: (first + b, 0))],
        out_specs=pl.BlockSpec((1, DFT_ROWS, HALF), lambda r, b: (r // per_half, b * per_half + r % per_half, 0)),
        out_shape=jax.ShapeDtypeStruct((2, T_SAMPLE, HALF), BF16),
    )(m, f_all)


def _dft_sample_table():
    k = jnp.arange(SEQ_SAMPLE, dtype=jnp.int32)
    ang = (2.0 * math.pi / SEQ_SAMPLE) * ((k[:, None] * k[None, :]) % SEQ_SAMPLE).astype(F32)
    scale = SEQ_SAMPLE ** -0.5
    return jnp.concatenate([jnp.cos(ang) * scale, jnp.sin(ang) * scale], axis=0).astype(BF16)


def _dft_prompt_tables():
    n1 = FFT_N1_PROMPT
    n = n1 * FFT_N2
    k1 = jnp.arange(n1, dtype=jnp.int32)
    ang1 = (2.0 * math.pi / n1) * ((k1[:, None] * k1[None, :]) % n1).astype(F32)
    d1 = jnp.concatenate([jnp.cos(ang1), -jnp.sin(ang1)], axis=0).astype(BF16)
    k2 = jnp.arange(FFT_N2, dtype=jnp.int32)
    k = k1[:, None, None] + n1 * k2[None, :, None]
    ang2 = (2.0 * math.pi / n) * ((k * k2[None, None, :]) % n).astype(F32)
    cr = jnp.cos(ang2) * (n ** -0.5)
    sr = jnp.sin(ang2) * (n ** -0.5)
    top = jnp.concatenate([cr, sr], axis=2)
    bot = jnp.concatenate([sr, -cr], axis=2)
    l2 = jnp.concatenate([top, bot], axis=1).astype(BF16)
    return d1, l2


def _fold_body(m_ref, w_ref, o_ref):
    o_ref[...] = jnp.dot(m_ref[...], w_ref[...], preferred_element_type=F32,
                         precision=lax.Precision.HIGHEST).astype(BF16)


def _fold_channel_dft(w_out_f):
    c = jnp.arange(HALF, dtype=jnp.int32)
    same = (c[:, None] // GROUP_CH) == (c[None, :] // GROUP_CH)
    ang = (2.0 * math.pi / GROUP_CH) * ((c[:, None] * c[None, :]) % GROUP_CH).astype(F32)
    scale = GROUP_CH ** -0.5
    cbd = jnp.where(same, jnp.cos(ang), 0.0) * scale
    sbd = jnp.where(same, jnp.sin(ang), 0.0) * scale
    m = jnp.concatenate([cbd, -sbd], axis=0)
    return pl.pallas_call(
        _fold_body, name="fold_channel_dft",
        out_shape=jax.ShapeDtypeStruct((2 * HALF, D), BF16),
    )(m, w_out_f)


def _route_and_store(x_new, g_ref, wr_ref, br_ref, x_out_ref, row_ref, meta_ref):
    x_out_ref[...] = x_new
    h = _rms(x_new, g_ref[...])
    h_hi = h.astype(BF16)
    h_lo = (h - h_hi.astype(F32)).astype(BF16)
    p = _dot(h_hi, wr_ref[...])
    q = _dot(h_lo, wr_ref[...])
    lg = p[:, :META] + p[:, META:] + q[:, :META] + q[:, META:] + br_ref[...]
    lane = lax.broadcasted_iota(jnp.int32, lg.shape, 1).astype(F32)
    neg = -jnp.inf
    lc = jnp.where(lane < N_GROUPS, lg, neg)
    mc = jnp.max(lc, axis=-1, keepdims=True)
    grp = jnp.min(jnp.where(lc == mc, lane, META), axis=-1, keepdims=True)
    p_grp = 1.0 / jnp.sum(jnp.exp(lc - mc), axis=-1, keepdims=True)
    lo = N_GROUPS + EPG * grp
    lf = jnp.where((lane >= lo) & (lane < lo + EPG), lg, neg)
    m1 = jnp.max(lf, axis=-1, keepdims=True)
    i1 = jnp.min(jnp.where(lf == m1, lane, META), axis=-1, keepdims=True)
    lf2 = jnp.where(lane == i1, neg, lf)
    m2 = jnp.max(lf2, axis=-1, keepdims=True)
    i2 = jnp.min(jnp.where(lf2 == m2, lane, META), axis=-1, keepdims=True)
    e = jnp.exp(m2 - m1)
    gate1 = p_grp / (1.0 + e)
    gate2 = p_grp * e / (1.0 + e)
    j1 = i1 - lo
    j2 = i2 - lo
    ja = jnp.minimum(j1, j2)
    jb = jnp.maximum(j1, j2)
    bucket = grp * N_PAIRS + (ja * (2 * EPG - 1 - ja)) * 0.5 + (jb - ja - 1)
    gate_a = jnp.where(j1 < j2, gate1, gate2)
    gate_b = jnp.where(j1 < j2, gate2, gate1)
    meta = jnp.where(lane == 0, gate_a,
                     jnp.where(lane == 1, gate_b,
                               jnp.where(lane == 2, bucket, 0.0)))
    for k in range(ROW_TILE):
        row_ref[pl.ds(k, TB, stride=ROW_TILE), :] = h[:, k * LANES:(k + 1) * LANES]
    meta_ref[...] = meta


def _router_operands(w_coarse, b_coarse, w_fine, b_fine):
    w = jnp.concatenate([w_coarse, w_fine.reshape(D, N_GROUPS * EPG)], axis=1)
    w = jnp.pad(w, ((0, 0), (0, META - w.shape[1])))
    w_hi = w.astype(BF16)
    w_lo = (w - w_hi.astype(F32)).astype(BF16)
    b = jnp.concatenate([b_coarse, b_fine.reshape(-1)])
    b = jnp.pad(b, (0, META - b.shape[0])).reshape(1, META)
    return jnp.concatenate([w_hi, w_lo], axis=1), b


def _l0_out_body(a_ref, cp_ref, sp_ref, cs_ref, ss_ref, xp_ref, xs_ref, wa_ref, wc_ref, ws_ref,
                 g_ref, wr_ref, br_ref, x_out_ref, row_ref, meta_ref):
    in_prompt = pl.program_id(0) < N_PROMPT_TILES
    x = jnp.where(in_prompt, xp_ref[...], xs_ref[...])
    c = jnp.where(in_prompt, cp_ref[...].astype(BF16), cs_ref[...])
    s = jnp.where(in_prompt, sp_ref[...].astype(BF16), ss_ref[...])
    mix = _dot(a_ref[...], wa_ref[...]) + _dot(c, wc_ref[...]) + _dot(s, ws_ref[...])
    _route_and_store(x + mix, g_ref, wr_ref, br_ref, x_out_ref, row_ref, meta_ref)


def _l0_out(a2, cs_prompt, cs_sample, xp, xs, wa, wc, ws, g, wr, br):
    def sample_half(which):
        return pl.BlockSpec((None, TB, HALF), lambda i: (which, jnp.maximum(i - N_PROMPT_TILES, 0), 0))
    return pl.pallas_call(
        _l0_out_body, name="l0_out",
        grid=(N_TILES,),
        in_specs=[_tile_spec(HALF), _prompt_spec(HALF), _prompt_spec(HALF), sample_half(0), sample_half(1),
                  _prompt_spec(), _sample_spec(),
                  _full_spec((HALF, D)), _full_spec((HALF, D)), _full_spec((HALF, D)),
                  _full_spec((1, D)), _full_spec((D, 2 * META)), _full_spec((1, META))],
        out_specs=[_tile_spec(D), _row_tiles_spec(), _tile_spec(META)],
        out_shape=[jax.ShapeDtypeStruct((T_ALL, D), F32), jax.ShapeDtypeStruct((T_ALL * ROW_TILE, LANES), F32),
                   jax.ShapeDtypeStruct((T_ALL, META), F32)],
    )(a2, *cs_prompt, cs_sample, cs_sample, xp, xs, wa, wc, ws, g, wr, br)


def _moe_plan(meta):
    bucket = meta[:, 2].astype(jnp.int32)
    order = jnp.argsort(bucket, stable=True).astype(jnp.int32)
    sorted_b = bucket[order]
    starts = jnp.searchsorted(sorted_b, jnp.arange(N_BUCKETS + 1, dtype=jnp.int32), side='left').astype(jnp.int32)
    counts = starts[1:] - starts[:-1]
    nblk = (counts + MOE_ROWS - 1) // MOE_ROWS
    cum = jnp.cumsum(nblk)
    first = cum - nblk
    j = jnp.arange(N_MOE_BLOCKS, dtype=jnp.int32)
    valid = j < cum[-1]
    j_eff = jnp.minimum(j, cum[-1] - 1)
    bj = jnp.minimum(jnp.searchsorted(cum, j_eff, side='right'), N_BUCKETS - 1).astype(jnp.int32)
    off = (j_eff - first[bj]) * MOE_ROWS
    cnt = jnp.where(valid, jnp.clip(counts[bj] - off, 0, MOE_ROWS), 0).astype(jnp.int32)
    r = jnp.arange(MOE_ROWS, dtype=jnp.int32)[None, :]
    pos = jnp.clip(starts[bj][:, None] + off[:, None] + r, 0, T_ALL - 1)
    tok = order[pos]
    spare = T_ALL + (j % 2)[:, None] * MOE_ROWS + r
    src = (tok * ROW_TILE).reshape(N_MOE_BLOCKS, 1, MOE_ROWS)
    dst = (jnp.where(r < cnt[:, None], tok, spare) * ROW_TILE).reshape(N_MOE_BLOCKS, 1, MOE_ROWS)
    gates = jnp.transpose(meta[:, :2][tok], (0, 2, 1))
    gates = jnp.pad(gates, ((0, 0), (0, ROW_TILE - 2), (0, 0)))
    pa, pb = np.triu_indices(EPG, k=1)
    grp = bj // N_PAIRS
    ea = grp * EPG + jnp.asarray(pa, jnp.int32)[bj % N_PAIRS]
    eb = grp * EPG + jnp.asarray(pb, jnp.int32)[bj % N_PAIRS]
    return ea.astype(jnp.int32), eb.astype(jnp.int32), cnt, src, dst, gates


def _moe_body(ea_ref, eb_ref, cnt_ref, src_ref, dst_ref, gates_ref, rows_hbm,
              wga_ref, wua_ref, wda_ref, wgb_ref, wub_ref, wdb_ref,
              y_hbm, xbuf, ybuf, gsem, ssem):
    s = pl.program_id(0)
    tile = ROW_TILE
    buf_rows = MOE_ROWS * tile

    def rows_in(b):
        inside = (b >= 0) & (b < N_MOE_BLOCKS)
        return jnp.where(inside, cnt_ref[jnp.clip(b, 0, N_MOE_BLOCKS - 1)], 0)

    def wait_fetch(p):
        pltpu.make_async_copy(rows_hbm.at[pl.ds(0, buf_rows)], xbuf.at[p], gsem.at[p]).wait()

    def wait_send(p):
        pltpu.make_async_copy(ybuf.at[p], y_hbm.at[pl.ds(0, buf_rows)], ssem.at[p]).wait()

    @pl.when(s == 0)
    def _():
        ybuf[...] = jnp.zeros(ybuf.shape, ybuf.dtype)
        for p in range(2):
            clear = pltpu.make_async_copy(ybuf.at[p], y_hbm.at[pl.ds((T_ALL + p * MOE_ROWS) * tile, buf_rows)],
                                          ssem.at[p])
            clear.start()
            clear.wait()

    for p in range(2):
        q = 1 - p
        mine = (s % 2) == p

        @pl.when(mine & (rows_in(s - 3) > 0))
        def _():
            wait_send(q)

        @pl.when(mine & (rows_in(s) > 0))
        def _():
            for r in range(MOE_ROWS):
                start = pl.multiple_of(src_ref[0, 0, r], tile)
                pltpu.make_async_copy(rows_hbm.at[pl.ds(start, tile)], xbuf.at[p, pl.ds(r * tile, tile)],
                                      gsem.at[p]).start()

        @pl.when(mine & (rows_in(s - 2) > 0))
        def _():
            for r in range(MOE_ROWS):
                start = pl.multiple_of(dst_ref[0, 0, r], tile)
                pltpu.make_async_copy(ybuf.at[p, pl.ds(r * tile, tile)], y_hbm.at[pl.ds(start, tile)],
                                      ssem.at[p]).start()

        @pl.when(mine & (rows_in(s - 1) > 0))
        def _():
            wait_fetch(q)
            x = jnp.concatenate([xbuf[q, pl.ds(k, MOE_ROWS, stride=tile), :] for k in range(tile)],
                                axis=1).astype(BF16)
            gates = gates_ref[0]
            diag = (lax.broadcasted_iota(jnp.int32, (MOE_ROWS, MOE_ROWS), 0)
                    == lax.broadcasted_iota(jnp.int32, (MOE_ROWS, MOE_ROWS), 1))
            gate_a = jnp.sum(jnp.where(diag, gates[0:1, :], 0.0), axis=1, keepdims=True)
            gate_b = jnp.sum(jnp.where(diag, gates[1:2, :], 0.0), axis=1, keepdims=True)

            def hidden(wg_ref, wu_ref, gate):
                g = _dot(x, wg_ref[0])
                u = _dot(x, wu_ref[0])
                return (g * jax.nn.sigmoid(g) * u * gate).astype(BF16)

            y = (_dot(hidden(wga_ref, wua_ref, gate_a), wda_ref[0])
                 + _dot(hidden(wgb_ref, wub_ref, gate_b), wdb_ref[0]))
            for k in range(tile):
                ybuf[q, pl.ds(k, MOE_ROWS, stride=tile), :] = y[:, k * LANES:(k + 1) * LANES]


def _moe(rows, plan, w_gate, w_up, w_down):
    ea, eb, cnt, src, dst, gates = plan
    last = N_MOE_BLOCKS - 1

    def block_of(step, lag):
        return jnp.clip(step - lag, 0, last)

    src_spec = pl.BlockSpec((1, 1, MOE_ROWS), lambda s, *_: (block_of(s, 0), 0, 0), memory_space=pltpu.SMEM)
    dst_spec = pl.BlockSpec((1, 1, MOE_ROWS), lambda s, *_: (block_of(s, 2), 0, 0), memory_space=pltpu.SMEM)
    gates_spec = pl.BlockSpec((1, ROW_TILE, MOE_ROWS), lambda s, *_: (block_of(s, 1), 0, 0))
    up_a = pl.BlockSpec((1, D, D_EXPERT), lambda s, ea, eb, cnt: (ea[block_of(s, 1)], 0, 0))
    up_b = pl.BlockSpec((1, D, D_EXPERT), lambda s, ea, eb, cnt: (eb[block_of(s, 1)], 0, 0))
    down_a = pl.BlockSpec((1, D_EXPERT, D), lambda s, ea, eb, cnt: (ea[block_of(s, 1)], 0, 0))
    down_b = pl.BlockSpec((1, D_EXPERT, D), lambda s, ea, eb, cnt: (eb[block_of(s, 1)], 0, 0))
    grid_spec = pltpu.PrefetchScalarGridSpec(
        num_scalar_prefetch=3,
        grid=(N_MOE_BLOCKS + 2,),
        in_specs=[src_spec, dst_spec, gates_spec, pl.BlockSpec(memory_space=pl.ANY),
                  up_a, up_a, down_a, up_b, up_b, down_b],
        out_specs=pl.BlockSpec(memory_space=pl.ANY),
        scratch_shapes=[pltpu.VMEM((2, MOE_ROWS * ROW_TILE, LANES), F32),
                        pltpu.VMEM((2, MOE_ROWS * ROW_TILE, LANES), F32),
                        pltpu.SemaphoreType.DMA((2,)), pltpu.SemaphoreType.DMA((2,))],
    )
    return pl.pallas_call(
        _moe_body, name="moe",
        grid_spec=grid_spec,
        out_shape=jax.ShapeDtypeStruct(((T_ALL + 2 * MOE_ROWS) * ROW_TILE, LANES), F32),
        compiler_params=pltpu.CompilerParams(dimension_semantics=("arbitrary",)),
    )(ea, eb, cnt, src, dst, gates, rows, w_gate, w_up, w_down, w_gate, w_up, w_down)


def _l1_in_body(x_ref, y_ref, g_ref, w_ref, x_out_ref, gate_ref, cv_ref):
    x = x_ref[...] + _read_row_tiles(y_ref)
    x_out_ref[...] = x
    u = _dot(_rms(x, g_ref[...]).astype(BF16), w_ref[...])
    gate_ref[...] = u[:, :D].astype(BF16)
    cv_ref[...] = (u[:, D:2 * D] * u[:, 2 * D:]).astype(BF16)


def _l1_in(x, y, g, w_in):
    return pl.pallas_call(
        _l1_in_body, name="l1_in",
        grid=(N_TILES,),
        in_specs=[_tile_spec(D), _row_tiles_spec(), _full_spec((1, D)), _full_spec((D, 3 * D))],
        out_specs=[_tile_spec(D), _tile_spec(D), _tile_spec(D)],
        out_shape=[jax.ShapeDtypeStruct((T_ALL, D), F32), jax.ShapeDtypeStruct((T_ALL, D), BF16),
                   jax.ShapeDtypeStruct((T_ALL, D), BF16)],
    )(x, y, g, w_in)


def _l1_out_body(prev_ref, cur_ref, next_ref, gate_ref, x_ref, cw_ref, wo_ref, g_ref, wr_ref, br_ref,
                 x_out_ref, row_ref, meta_ref, buf, ybuf):
    starts, ends = _seq_edges(pl.program_id(0))
    buf[0:HALO, :] = jnp.where(starts, 0.0, prev_ref[...].astype(F32))
    buf[HALO:HALO + TB, :] = cur_ref[...].astype(F32)
    buf[HALO + TB:, :] = jnp.where(ends, 0.0, next_ref[...].astype(F32))
    rows = CONV_ROWS // 2
    for r0 in range(0, TB, rows):
        conv = (cw_ref[0:1, :] * buf[r0 + HALO - 1:r0 + HALO - 1 + rows, :]
                + cw_ref[1:2, :] * buf[r0 + HALO:r0 + HALO + rows, :]
                + cw_ref[2:3, :] * buf[r0 + HALO + 1:r0 + HALO + 1 + rows, :])
        ybuf[r0:r0 + rows, :] = (gate_ref[r0:r0 + rows, :].astype(F32) * conv).astype(BF16)
    _route_and_store(x_ref[...] + _dot(ybuf[...], wo_ref[...]), g_ref, wr_ref, br_ref, x_out_ref, row_ref, meta_ref)


def _l1_out(cv, gate, x, conv_w, w_out, g, wr, br):
    prev, nxt = _halo_specs(D)
    return pl.pallas_call(
        _l1_out_body, name="l1_out",
        grid=(N_TILES,),
        in_specs=[prev, _tile_spec(D), nxt, _tile_spec(D), _tile_spec(D), _full_spec((8, D)),
                  _full_spec((D, D)), _full_spec((1, D)), _full_spec((D, 2 * META)), _full_spec((1, META))],
        out_specs=[_tile_spec(D), _row_tiles_spec(), _tile_spec(META)],
        out_shape=[jax.ShapeDtypeStruct((T_ALL, D), F32), jax.ShapeDtypeStruct((T_ALL * ROW_TILE, LANES), F32),
                   jax.ShapeDtypeStruct((T_ALL, META), F32)],
        scratch_shapes=[pltpu.VMEM((TB + 2 * HALO, D), F32), pltpu.VMEM((TB, D), BF16)],
    )(cv, cv, cv, gate, x, conv_w, w_out, g, wr, br)


def _final_body(x_ref, y_ref, g_ref, o_ref):
    o_ref[...] = _rms(x_ref[...] + _read_row_tiles(y_ref), g_ref[...])


def _final(x, y, g, *, tile0, tiles):
    in_tile = pl.BlockSpec((TB, D), lambda i: (i + tile0, 0))
    y_tile = pl.BlockSpec((TB * ROW_TILE, LANES), lambda i: (i + tile0, 0))
    return pl.pallas_call(
        _final_body, name="final_norm",
        grid=(tiles,),
        in_specs=[in_tile, y_tile, _full_spec((1, D))],
        out_specs=_tile_spec(D),
        out_shape=jax.ShapeDtypeStruct((tiles * TB, D), F32),
    )(x, y, g)


def kernel(x_prompt, x_sample, l0_norm_mix, l0_w_in, l0_conv_w, l0_conv_b, l0_ln_g, l0_ln_b, l0_w_out, l0_norm_ffn, l0_w_coarse, l0_b_coarse, l0_w_fine, l0_b_fine, l0_w_gate, l0_w_up, l0_w_down, l1_norm_mix, l1_w_in, l1_conv_w, l1_w_out, l1_norm_ffn, l1_w_coarse, l1_b_coarse, l1_w_fine, l1_b_fine, l1_w_gate, l1_w_up, l1_w_down, final_norm):
    xp = x_prompt.reshape(T_PROMPT, D)
    xs = x_sample.reshape(T_SAMPLE, D)
    row = lambda v: v.reshape(1, -1)

    a, f = _l0_in(xp, xs, row(l0_norm_mix), l0_w_in.astype(BF16))
    conv_w = jnp.pad(l0_conv_w, ((0, 1), (0, 0)))
    a2 = _l0_conv(a, conv_w, row(l0_conv_b), row(l0_ln_g), row(l0_ln_b))
    d1, l2 = _dft_prompt_tables()
    cs_prompt = _dft2_prompt(_dft1_prompt(f, d1), l2)
    cs_sample = _dft_sample(f, _dft_sample_table())
    w_cs = _fold_channel_dft(l0_w_out[HALF:])
    wr0, br0 = _router_operands(l0_w_coarse, l0_b_coarse, l0_w_fine, l0_b_fine)
    x1, rows0, meta0 = _l0_out(a2, cs_prompt, cs_sample, xp, xs, l0_w_out[:HALF].astype(BF16), w_cs[:HALF],
                               w_cs[HALF:], row(l0_norm_ffn), wr0, br0)

    y0 = _moe(rows0, _moe_plan(meta0), l0_w_gate.astype(BF16), l0_w_up.astype(BF16), l0_w_down.astype(BF16))

    x2, gate, cv = _l1_in(x1, y0, row(l1_norm_mix), l1_w_in.astype(BF16))
    wr1, br1 = _router_operands(l1_w_coarse, l1_b_coarse, l1_w_fine, l1_b_fine)
    x3, rows1, meta1 = _l1_out(cv, gate, x2, jnp.pad(l1_conv_w, ((0, 5), (0, 0))), l1_w_out.astype(BF16),
                               row(l1_norm_ffn), wr1, br1)

    y1 = _moe(rows1, _moe_plan(meta1), l1_w_gate.astype(BF16), l1_w_up.astype(BF16), l1_w_down.astype(BF16))

    out_p = _final(x3, y1, row(final_norm), tile0=0, tiles=N_PROMPT_TILES)
    out_s = _final(x3, y1, row(final_norm), tile0=N_PROMPT_TILES, tiles=N_TILES - N_PROMPT_TILES)
    return out_p.reshape(x_prompt.shape), out_s.reshape(x_sample.shape)
```

```python
import functools
import math

import jax
import jax.numpy as jnp
import numpy as np
from jax import lax
from jax.experimental import pallas as pl
from jax.experimental.pallas import tpu as pltpu

D = 1024
LANES = 128
T_PROMPT = 16384
SEQ_SAMPLE = 2048
T_SAMPLE = 32 * SEQ_SAMPLE
T_ALL = T_PROMPT + T_SAMPLE
HALF = 512
GROUP_CH = 64
CONV_W = 31
N_GROUPS = 4
EPG = 8
N_PAIRS = EPG * (EPG - 1) // 2
N_BUCKETS = N_GROUPS * N_PAIRS
D_EXPERT = 512
RMS_EPS = 1e-6
LN_EPS = 1e-5

TB = 512
N_TILES = T_ALL // TB
N_PROMPT_TILES = T_PROMPT // TB
HALO = 16
CONV_ROWS = 64
MOE_ROWS = 128
N_MOE_BLOCKS = T_ALL // MOE_ROWS + N_BUCKETS + 1
META = 128
ROW_TILE = D // LANES
ROUTED = ROW_TILE + 1
FFT_N2 = 128
FFT_N1_PROMPT = T_PROMPT // FFT_N2
FFT_G1 = 16
FFT_KB = 8
DFT_ROWS = 1024

BF16 = jnp.bfloat16
F32 = jnp.float32


def _rms(x, g):
    return x * lax.rsqrt(jnp.mean(x * x, axis=-1, keepdims=True) + RMS_EPS) * g


def _dot(a, b):
    return jnp.dot(a, b, preferred_element_type=F32)


def _seq_edges(i):
    r0 = i * TB
    r1 = r0 + TB
    in_prompt = r0 < T_PROMPT
    starts = jnp.where(in_prompt, r0 == 0, (r0 - T_PROMPT) % SEQ_SAMPLE == 0)
    ends = jnp.where(in_prompt, r1 == T_PROMPT, (r1 - T_PROMPT) % SEQ_SAMPLE == 0)
    return starts, ends


def _tile_spec(width):
    return pl.BlockSpec((TB, width), lambda i: (i, 0))


def _row_tiles_spec(rows_per_token=ROW_TILE):
    return pl.BlockSpec((TB * rows_per_token, LANES), lambda i: (i, 0))


def _read_row_tiles(ref):
    return jnp.concatenate([ref[pl.ds(k, TB, stride=ROW_TILE), :] for k in range(ROW_TILE)], axis=1)


def _full_spec(shape):
    return pl.BlockSpec(shape, lambda *_: (0,) * len(shape))


def _prompt_spec(width=D):
    return pl.BlockSpec((TB, width), lambda i: (jnp.minimum(i, N_PROMPT_TILES - 1), 0))


def _sample_spec(width=D):
    return pl.BlockSpec((TB, width), lambda i: (jnp.maximum(i - N_PROMPT_TILES, 0), 0))


def _halo_specs(width):
    per_tile = TB // HALO
    last = T_ALL // HALO - 1
    prev = pl.BlockSpec((HALO, width), lambda i: (jnp.maximum(i * per_tile - 1, 0), 0))
    nxt = pl.BlockSpec((HALO, width), lambda i: (jnp.minimum((i + 1) * per_tile, last), 0))
    return prev, nxt


def _l0_in_body(xp_ref, xs_ref, g_ref, w_ref, a_ref, f_ref):
    i = pl.program_id(0)
    x = jnp.where(i < N_PROMPT_TILES, xp_ref[...], xs_ref[...])
    h = _rms(x, g_ref[...])
    u = _dot(h.astype(BF16), w_ref[...])
    a_ref[...] = u[:, :HALF] * jax.nn.sigmoid(u[:, HALF:2 * HALF])
    f_ref[...] = u[:, 2 * HALF:].astype(BF16)


def _l0_in(xp, xs, g, w_in):
    return pl.pallas_call(
        _l0_in_body, name="l0_in",
        grid=(N_TILES,),
        in_specs=[_prompt_spec(), _sample_spec(), _full_spec((1, D)), _full_spec((D, 3 * HALF))],
        out_specs=[_tile_spec(HALF), _tile_spec(HALF)],
        out_shape=[jax.ShapeDtypeStruct((T_ALL, HALF), F32), jax.ShapeDtypeStruct((T_ALL, HALF), BF16)],
    )(xp, xs, g, w_in)


def _l0_conv_body(prev_ref, cur_ref, next_ref, w_ref, b_ref, g_ref, beta_ref, o_ref, buf):
    starts, ends = _seq_edges(pl.program_id(0))
    buf[0, 0:HALO, :] = jnp.where(starts, 0.0, prev_ref[...])
    buf[0, HALO:HALO + TB, :] = cur_ref[...]
    buf[0, HALO + TB:, :] = jnp.where(ends, 0.0, next_ref[...])
    rows = TB + 2 * HALO - 8
    for s in range(1, 8):
        for c0 in range(0, rows, 128):
            c1 = min(c0 + 128, rows)
            buf[s, c0:c1, :] = buf[0, c0 + s:c1 + s, :]
    first = HALO - CONV_W // 2
    for r0 in range(0, TB, CONV_ROWS):
        acc = jnp.broadcast_to(b_ref[...], (CONV_ROWS, HALF))
        for k in range(CONV_W):
            s = (first + k) % 8
            base = r0 + first + k - s
            acc = acc + w_ref[k:k + 1, :] * buf[s, base:base + CONV_ROWS, :]
        mu = jnp.mean(acc, axis=-1, keepdims=True)
        xc = acc - mu
        var = jnp.mean(xc * xc, axis=-1, keepdims=True)
        y = xc * lax.rsqrt(var + LN_EPS) * g_ref[...] + beta_ref[...]
        o_ref[r0:r0 + CONV_ROWS, :] = (y * jax.nn.sigmoid(y)).astype(BF16)


def _l0_conv(a, w, b, g, beta):
    prev, nxt = _halo_specs(HALF)
    return pl.pallas_call(
        _l0_conv_body, name="l0_conv",
        grid=(N_TILES,),
        in_specs=[prev, _tile_spec(HALF), nxt, _full_spec((CONV_W + 1, HALF)),
                  _full_spec((1, HALF)), _full_spec((1, HALF)), _full_spec((1, HALF))],
        out_specs=_tile_spec(HALF),
        out_shape=jax.ShapeDtypeStruct((T_ALL, HALF), BF16),
        scratch_shapes=[pltpu.VMEM((8, TB + 2 * HALO, HALF), F32)],
    )(a, a, a, w, b, g, beta)


def _dft1_prompt_body(d_ref, x_ref, y_ref, xs, ys):
    n1 = FFT_N1_PROMPT
    x = x_ref[0].reshape(n1 * FFT_G1, HALF).astype(F32)
    for c in range(HALF // LANES):
        xs[c] = x[:, c * LANES:(c + 1) * LANES]
    for q in range(FFT_G1):
        rows = pl.ds(q, n1, stride=FFT_G1)
        xq = jnp.concatenate([xs[c, rows, :] for c in range(HALF // LANES)], axis=1)
        y = _dot(d_ref[...], xq.astype(BF16))
        for c in range(HALF // LANES):
            ys[0, c, rows, :] = y[:n1, c * LANES:(c + 1) * LANES]
            ys[1, c, rows, :] = y[n1:, c * LANES:(c + 1) * LANES]
    for p in range(2):
        for c in range(HALF // LANES):
            y_ref[p, :, :, c * LANES:(c + 1) * LANES] = ys[p, c].reshape(n1, FFT_G1, LANES)


def _dft1_prompt(f_all, d1):
    n1 = FFT_N1_PROMPT
    x = f_all.reshape(T_ALL // T_PROMPT, n1, FFT_N2, HALF)
    return pl.pallas_call(
        _dft1_prompt_body, name="dft1_prompt",
        grid=(FFT_N2 // FFT_G1,),
        in_specs=[_full_spec((2 * n1, n1)),
                  pl.BlockSpec((1, n1, FFT_G1, HALF), lambda j: (0, 0, j, 0))],
        out_specs=pl.BlockSpec((2, n1, FFT_G1, HALF), lambda j: (0, 0, j, 0)),
        out_shape=jax.ShapeDtypeStruct((2, n1, FFT_N2, HALF), F32),
        scratch_shapes=[pltpu.VMEM((HALF // LANES, n1 * FFT_G1, LANES), F32),
                        pltpu.VMEM((2, HALF // LANES, n1 * FFT_G1, LANES), F32)],
        compiler_params=pltpu.CompilerParams(vmem_limit_bytes=48 * 1024 * 1024),
    )(d1, x)


def _dft2_prompt_body(l_ref, y_ref, c_ref, s_ref, cs, ss):
    for q in range(FFT_KB):
        yq = y_ref[:, q].reshape(2 * FFT_N2, HALF).astype(BF16)
        r = _dot(l_ref[q], yq)
        rows = pl.ds(q, FFT_N2, stride=FFT_KB)
        for c in range(HALF // LANES):
            cs[c, rows, :] = r[:FFT_N2, c * LANES:(c + 1) * LANES]
            ss[c, rows, :] = r[FFT_N2:, c * LANES:(c + 1) * LANES]
    for c in range(HALF // LANES):
        c_ref[:, :, c * LANES:(c + 1) * LANES] = cs[c].reshape(FFT_N2, FFT_KB, LANES)
        s_ref[:, :, c * LANES:(c + 1) * LANES] = ss[c].reshape(FFT_N2, FFT_KB, LANES)


def _dft2_prompt(y, l2):
    n1 = FFT_N1_PROMPT
    out_sds = jax.ShapeDtypeStruct((FFT_N2, n1, HALF), F32)
    out_spec = pl.BlockSpec((FFT_N2, FFT_KB, HALF), lambda k: (0, k, 0))
    c, s = pl.pallas_call(
        _dft2_prompt_body, name="dft2_prompt",
        grid=(n1 // FFT_KB,),
        in_specs=[pl.BlockSpec((FFT_KB, 2 * FFT_N2, 2 * FFT_N2), lambda k: (k, 0, 0)),
                  pl.BlockSpec((2, FFT_KB, FFT_N2, HALF), lambda k: (0, k, 0, 0))],
        out_specs=[out_spec, out_spec],
        out_shape=[out_sds, out_sds],
        scratch_shapes=[pltpu.VMEM((HALF // LANES, FFT_N2 * FFT_KB, LANES), F32),
                        pltpu.VMEM((HALF // LANES, FFT_N2 * FFT_KB, LANES), F32)],
    )(l2, y)
    return c.reshape(T_PROMPT, HALF), s.reshape(T_PROMPT, HALF)


def _dft_sample_body(m_ref, x_ref, o_ref):
    for part in range(2):
        for r0 in range(0, SEQ_SAMPLE, DFT_ROWS):
            rows = m_ref[part * SEQ_SAMPLE + r0:part * SEQ_SAMPLE + r0 + DFT_ROWS, :]
            o_ref[part, r0:r0 + DFT_ROWS, :] = _dot(rows, x_ref[...]).astype(BF16)


def _dft_sample(f_all, m):
    first = T_PROMPT // SEQ_SAMPLE
    return pl.pallas_call(
        _dft_sample_body, name="dft_sample",
        grid=(T_SAMPLE // SEQ_SAMPLE,),
        in_specs=[pl.BlockSpec((2 * SEQ_SAMPLE, SEQ_SAMPLE), lambda b: (0, 0), pipeline_mode=pl.Buffered(1)),
                  pl.BlockSpec((SEQ_SAMPLE, HALF), lambda b: (first + b, 0))],
        out_specs=pl.BlockSpec((2, SEQ_SAMPLE, HALF), lambda b: (0, b, 0)),
        out_shape=jax.ShapeDtypeStruct((2, T_SAMPLE, HALF), BF16),
        compiler_params=pltpu.CompilerParams(vmem_limit_bytes=48 * 1024 * 1024),
    )(m, f_all)


def _dft_sample_table():
    k = jnp.arange(SEQ_SAMPLE, dtype=jnp.int32)
    ang = (2.0 * math.pi / SEQ_SAMPLE) * ((k[:, None] * k[None, :]) % SEQ_SAMPLE).astype(F32)
    scale = SEQ_SAMPLE ** -0.5
    return jnp.concatenate([jnp.cos(ang) * scale, jnp.sin(ang) * scale], axis=0).astype(BF16)


def _dft_prompt_tables():
    n1 = FFT_N1_PROMPT
    n = n1 * FFT_N2
    k1 = jnp.arange(n1, dtype=jnp.int32)
    ang1 = (2.0 * math.pi / n1) * ((k1[:, None] * k1[None, :]) % n1).astype(F32)
    d1 = jnp.concatenate([jnp.cos(ang1), -jnp.sin(ang1)], axis=0).astype(BF16)
    k2 = jnp.arange(FFT_N2, dtype=jnp.int32)
    k = k1[:, None, None] + n1 * k2[None, :, None]
    ang2 = (2.0 * math.pi / n) * ((k * k2[None, None, :]) % n).astype(F32)
    cr = jnp.cos(ang2) * (n ** -0.5)
    sr = jnp.sin(ang2) * (n ** -0.5)
    top = jnp.concatenate([cr, sr], axis=2)
    bot = jnp.concatenate([sr, -cr], axis=2)
    l2 = jnp.concatenate([top, bot], axis=1).astype(BF16)
    return d1, l2


def _fold_body(m_ref, w_ref, o_ref):
    o_ref[...] = jnp.dot(m_ref[...], w_ref[...], preferred_element_type=F32,
                         precision=lax.Precision.HIGHEST).astype(BF16)


def _fold_channel_dft(w_out_f):
    c = jnp.arange(HALF, dtype=jnp.int32)
    same = (c[:, None] // GROUP_CH) == (c[None, :] // GROUP_CH)
    ang = (2.0 * math.pi / GROUP_CH) * ((c[:, None] * c[None, :]) % GROUP_CH).astype(F32)
    scale = GROUP_CH ** -0.5
    cbd = jnp.where(same, jnp.cos(ang), 0.0) * scale
    sbd = jnp.where(same, jnp.sin(ang), 0.0) * scale
    m = jnp.concatenate([cbd, -sbd], axis=0)
    return pl.pallas_call(
        _fold_body, name="fold_channel_dft",
        out_shape=jax.ShapeDtypeStruct((2 * HALF, D), BF16),
    )(m, w_out_f)


def _route_and_store(x_new, g_ref, wr_ref, br_ref, x_out_ref, row_ref, meta_ref):
    x_out_ref[...] = x_new
    h = _rms(x_new, g_ref[...])
    h_hi = h.astype(BF16)
    h_lo = (h - h_hi.astype(F32)).astype(BF16)
    p = _dot(h_hi, wr_ref[...])
    q = _dot(h_lo, wr_ref[...])
    lg = p[:, :META] + p[:, META:] + q[:, :META] + q[:, META:] + br_ref[...]
    lane = lax.broadcasted_iota(jnp.int32, lg.shape, 1).astype(F32)
    neg = -jnp.inf
    lc = jnp.where(lane < N_GROUPS, lg, neg)
    mc = jnp.max(lc, axis=-1, keepdims=True)
    grp = jnp.min(jnp.where(lc == mc, lane, META), axis=-1, keepdims=True)
    p_grp = 1.0 / jnp.sum(jnp.exp(lc - mc), axis=-1, keepdims=True)
    lo = N_GROUPS + EPG * grp
    lf = jnp.where((lane >= lo) & (lane < lo + EPG), lg, neg)
    m1 = jnp.max(lf, axis=-1, keepdims=True)
    i1 = jnp.min(jnp.where(lf == m1, lane, META), axis=-1, keepdims=True)
    lf2 = jnp.where(lane == i1, neg, lf)
    m2 = jnp.max(lf2, axis=-1, keepdims=True)
    i2 = jnp.min(jnp.where(lf2 == m2, lane, META), axis=-1, keepdims=True)
    e = jnp.exp(m2 - m1)
    gate1 = p_grp / (1.0 + e)
    gate2 = p_grp * e / (1.0 + e)
    j1 = i1 - lo
    j2 = i2 - lo
    ja = jnp.minimum(j1, j2)
    jb = jnp.maximum(j1, j2)
    bucket = grp * N_PAIRS + (ja * (2 * EPG - 1 - ja)) * 0.5 + (jb - ja - 1)
    gate_a = jnp.where(j1 < j2, gate1, gate2)
    gate_b = jnp.where(j1 < j2, gate2, gate1)
    meta = jnp.where(lane == 0, gate_a,
                     jnp.where(lane == 1, gate_b,
                               jnp.where(lane == 2, bucket, 0.0)))
    for k in range(ROW_TILE):
        row_ref[pl.ds(k, TB, stride=ROUTED), :] = h[:, k * LANES:(k + 1) * LANES]
    row_ref[pl.ds(ROW_TILE, TB, stride=ROUTED), :] = meta
    meta_ref[...] = meta


def _router_operands(w_coarse, b_coarse, w_fine, b_fine):
    w = jnp.concatenate([w_coarse, w_fine.reshape(D, N_GROUPS * EPG)], axis=1)
    w = jnp.pad(w, ((0, 0), (0, META - w.shape[1])))
    w_hi = w.astype(BF16)
    w_lo = (w - w_hi.astype(F32)).astype(BF16)
    b = jnp.concatenate([b_coarse, b_fine.reshape(-1)])
    b = jnp.pad(b, (0, META - b.shape[0])).reshape(1, META)
    return jnp.concatenate([w_hi, w_lo], axis=1), b


def _l0_out_body(a_ref, cp_ref, sp_ref, cs_ref, ss_ref, xp_ref, xs_ref, wa_ref, wc_ref, ws_ref,
                 g_ref, wr_ref, br_ref, x_out_ref, row_ref, meta_ref):
    in_prompt = pl.program_id(0) < N_PROMPT_TILES
    x = jnp.where(in_prompt, xp_ref[...], xs_ref[...])
    c = jnp.where(in_prompt, cp_ref[...].astype(BF16), cs_ref[...])
    s = jnp.where(in_prompt, sp_ref[...].astype(BF16), ss_ref[...])
    mix = _dot(a_ref[...], wa_ref[...]) + _dot(c, wc_ref[...]) + _dot(s, ws_ref[...])
    _route_and_store(x + mix, g_ref, wr_ref, br_ref, x_out_ref, row_ref, meta_ref)


def _l0_out(a2, cs_prompt, cs_sample, xp, xs, wa, wc, ws, g, wr, br):
    def sample_half(which):
        return pl.BlockSpec((None, TB, HALF), lambda i: (which, jnp.maximum(i - N_PROMPT_TILES, 0), 0))
    return pl.pallas_call(
        _l0_out_body, name="l0_out",
        grid=(N_TILES,),
        in_specs=[_tile_spec(HALF), _prompt_spec(HALF), _prompt_spec(HALF), sample_half(0), sample_half(1),
                  _prompt_spec(), _sample_spec(),
                  _full_spec((HALF, D)), _full_spec((HALF, D)), _full_spec((HALF, D)),
                  _full_spec((1, D)), _full_spec((D, 2 * META)), _full_spec((1, META))],
        out_specs=[_tile_spec(D), _row_tiles_spec(ROUTED), _tile_spec(META)],
        out_shape=[jax.ShapeDtypeStruct((T_ALL, D), F32), jax.ShapeDtypeStruct((T_ALL * ROUTED, LANES), F32),
                   jax.ShapeDtypeStruct((T_ALL, META), F32)],
    )(a2, *cs_prompt, cs_sample, cs_sample, xp, xs, wa, wc, ws, g, wr, br)


def _moe_plan(meta):
    bucket = meta[:, 2].astype(jnp.int32)
    sorted_b, order = lax.sort((bucket, jnp.arange(T_ALL, dtype=jnp.int32)), num_keys=1, is_stable=True)
    edges = jnp.arange(N_BUCKETS + 1, dtype=jnp.int32)
    starts = jnp.sum(sorted_b[None, :] < edges[:, None], axis=1).astype(jnp.int32)
    counts = starts[1:] - starts[:-1]
    nblk = (counts + MOE_ROWS - 1) // MOE_ROWS
    cum = jnp.cumsum(nblk)
    first = cum - nblk
    j = jnp.arange(N_MOE_BLOCKS, dtype=jnp.int32)
    valid = j < cum[-1]
    j_eff = jnp.minimum(j, cum[-1] - 1)
    bj = jnp.minimum(jnp.sum(cum[None, :] <= j_eff[:, None], axis=1), N_BUCKETS - 1).astype(jnp.int32)
    off = (j_eff - first[bj]) * MOE_ROWS
    cnt = jnp.where(valid, jnp.clip(counts[bj] - off, 0, MOE_ROWS), 0).astype(jnp.int32)
    r = jnp.arange(MOE_ROWS, dtype=jnp.int32)[None, :]
    pos = jnp.clip(starts[bj][:, None] + off[:, None] + r, 0, T_ALL - 1)
    tok = order[pos]
    spare = T_ALL + (j % 2)[:, None] * MOE_ROWS + r
    src = (tok * ROUTED).reshape(N_MOE_BLOCKS, 1, MOE_ROWS)
    dst = (jnp.where(r < cnt[:, None], tok, spare) * ROW_TILE).reshape(N_MOE_BLOCKS, 1, MOE_ROWS)
    pa, pb = np.triu_indices(EPG, k=1)
    grp = bj // N_PAIRS
    ea = grp * EPG + jnp.asarray(pa, jnp.int32)[bj % N_PAIRS]
    eb = grp * EPG + jnp.asarray(pb, jnp.int32)[bj % N_PAIRS]
    return ea.astype(jnp.int32), eb.astype(jnp.int32), cnt, src, dst


def _moe_body(ea_ref, eb_ref, cnt_ref, src_ref, dst_ref, rows_hbm,
              wga_ref, wua_ref, wda_ref, wgb_ref, wub_ref, wdb_ref,
              y_hbm, xbuf, ybuf, gsem, ssem):
    s = pl.program_id(0)
    tile = ROW_TILE
    buf_rows = MOE_ROWS * tile

    def rows_in(b):
        inside = (b >= 0) & (b < N_MOE_BLOCKS)
        return jnp.where(inside, cnt_ref[jnp.clip(b, 0, N_MOE_BLOCKS - 1)], 0)

    def wait_fetch(p):
        pltpu.make_async_copy(rows_hbm.at[pl.ds(0, MOE_ROWS * ROUTED)], xbuf.at[p], gsem.at[p]).wait()

    def wait_send(p):
        pltpu.make_async_copy(ybuf.at[p], y_hbm.at[pl.ds(0, buf_rows)], ssem.at[p]).wait()

    @pl.when(s == 0)
    def _():
        ybuf[...] = jnp.zeros(ybuf.shape, ybuf.dtype)
        for p in range(2):
            clear = pltpu.make_async_copy(ybuf.at[p], y_hbm.at[pl.ds((T_ALL + p * MOE_ROWS) * tile, buf_rows)],
                                          ssem.at[p])
            clear.start()
            clear.wait()

    for p in range(2):
        q = 1 - p
        mine = (s % 2) == p

        @pl.when(mine & (rows_in(s - 3) > 0))
        def _():
            wait_send(q)

        @pl.when(mine & (rows_in(s) > 0))
        def _():
            for r in range(MOE_ROWS):
                pltpu.make_async_copy(rows_hbm.at[pl.ds(src_ref[0, 0, r], ROUTED)],
                                      xbuf.at[p, pl.ds(r * ROUTED, ROUTED)], gsem.at[p]).start(priority=r % 2)

        @pl.when(mine & (rows_in(s - 2) > 0))
        def _():
            for r in range(MOE_ROWS):
                start = pl.multiple_of(dst_ref[0, 0, r], tile)
                pltpu.make_async_copy(ybuf.at[p, pl.ds(r * tile, tile)], y_hbm.at[pl.ds(start, tile)],
                                      ssem.at[p]).start(priority=r % 2)

        @pl.when(mine & (rows_in(s - 1) > 0))
        def _():
            wait_fetch(q)
            x = jnp.concatenate([xbuf[q, pl.ds(k, MOE_ROWS, stride=ROUTED), :] for k in range(tile)],
                                axis=1).astype(BF16)
            record = xbuf[q, pl.ds(tile, MOE_ROWS, stride=ROUTED), :]
            gate_a = record[:, 0:1]
            gate_b = record[:, 1:2]

            def hidden(wg_ref, wu_ref, gate):
                g = _dot(x, wg_ref[0])
                u = _dot(x, wu_ref[0])
                return (g * jax.nn.sigmoid(g) * u * gate).astype(BF16)

            y = (_dot(hidden(wga_ref, wua_ref, gate_a), wda_ref[0])
                 + _dot(hidden(wgb_ref, wub_ref, gate_b), wdb_ref[0]))
            for k in range(tile):
                ybuf[q, pl.ds(k, MOE_ROWS, stride=tile), :] = y[:, k * LANES:(k + 1) * LANES]


def _moe(rows, plan, w_gate, w_up, w_down):
    ea, eb, cnt, src, dst = plan
    last = N_MOE_BLOCKS - 1

    def block_of(step, lag):
        return jnp.clip(step - lag, 0, last)

    src_spec = pl.BlockSpec((1, 1, MOE_ROWS), lambda s, *_: (block_of(s, 0), 0, 0), memory_space=pltpu.SMEM)
    dst_spec = pl.BlockSpec((1, 1, MOE_ROWS), lambda s, *_: (block_of(s, 2), 0, 0), memory_space=pltpu.SMEM)
    up_a = pl.BlockSpec((1, D, D_EXPERT), lambda s, ea, eb, cnt: (ea[block_of(s, 1)], 0, 0))
    up_b = pl.BlockSpec((1, D, D_EXPERT), lambda s, ea, eb, cnt: (eb[block_of(s, 1)], 0, 0))
    down_a = pl.BlockSpec((1, D_EXPERT, D), lambda s, ea, eb, cnt: (ea[block_of(s, 1)], 0, 0))
    down_b = pl.BlockSpec((1, D_EXPERT, D), lambda s, ea, eb, cnt: (eb[block_of(s, 1)], 0, 0))
    grid_spec = pltpu.PrefetchScalarGridSpec(
        num_scalar_prefetch=3,
        grid=(N_MOE_BLOCKS + 2,),
        in_specs=[src_spec, dst_spec, pl.BlockSpec(memory_space=pl.ANY),
                  up_a, up_a, down_a, up_b, up_b, down_b],
        out_specs=pl.BlockSpec(memory_space=pl.ANY),
        scratch_shapes=[pltpu.VMEM((2, MOE_ROWS * ROUTED, LANES), F32),
                        pltpu.VMEM((2, MOE_ROWS * ROW_TILE, LANES), F32),
                        pltpu.SemaphoreType.DMA((2,)), pltpu.SemaphoreType.DMA((2,))],
    )
    return pl.pallas_call(
        _moe_body, name="moe",
        grid_spec=grid_spec,
        out_shape=jax.ShapeDtypeStruct(((T_ALL + 2 * MOE_ROWS) * ROW_TILE, LANES), F32),
        compiler_params=pltpu.CompilerParams(dimension_semantics=("arbitrary",)),
    )(ea, eb, cnt, src, dst, rows, w_gate, w_up, w_down, w_gate, w_up, w_down)


def _l1_in_body(x_ref, y_ref, g_ref, w_ref, x_out_ref, gate_ref, cv_ref):
    x = x_ref[...] + _read_row_tiles(y_ref)
    x_out_ref[...] = x
    u = _dot(_rms(x, g_ref[...]).astype(BF16), w_ref[...])
    gate_ref[...] = u[:, :D].astype(BF16)
    cv_ref[...] = (u[:, D:2 * D] * u[:, 2 * D:]).astype(BF16)


def _l1_in(x, y, g, w_in):
    return pl.pallas_call(
        _l1_in_body, name="l1_in",
        grid=(N_TILES,),
        in_specs=[_tile_spec(D), _row_tiles_spec(), _full_spec((1, D)), _full_spec((D, 3 * D))],
        out_specs=[_tile_spec(D), _tile_spec(D), _tile_spec(D)],
        out_shape=[jax.ShapeDtypeStruct((T_ALL, D), F32), jax.ShapeDtypeStruct((T_ALL, D), BF16),
                   jax.ShapeDtypeStruct((T_ALL, D), BF16)],
    )(x, y, g, w_in)


def _l1_out_body(prev_ref, cur_ref, next_ref, gate_ref, x_ref, cw_ref, wo_ref, g_ref, wr_ref, br_ref,
                 x_out_ref, row_ref, meta_ref, buf, ybuf):
    starts, ends = _seq_edges(pl.program_id(0))
    buf[0:HALO, :] = jnp.where(starts, 0.0, prev_ref[...].astype(F32))
    buf[HALO:HALO + TB, :] = cur_ref[...].astype(F32)
    buf[HALO + TB:, :] = jnp.where(ends, 0.0, next_ref[...].astype(F32))
    rows = CONV_ROWS // 2
    for r0 in range(0, TB, rows):
        conv = (cw_ref[0:1, :] * buf[r0 + HALO - 1:r0 + HALO - 1 + rows, :]
                + cw_ref[1:2, :] * buf[r0 + HALO:r0 + HALO + rows, :]
                + cw_ref[2:3, :] * buf[r0 + HALO + 1:r0 + HALO + 1 + rows, :])
        ybuf[r0:r0 + rows, :] = (gate_ref[r0:r0 + rows, :].astype(F32) * conv).astype(BF16)
    _route_and_store(x_ref[...] + _dot(ybuf[...], wo_ref[...]), g_ref, wr_ref, br_ref, x_out_ref, row_ref, meta_ref)


def _l1_out(cv, gate, x, conv_w, w_out, g, wr, br):
    prev, nxt = _halo_specs(D)
    return pl.pallas_call(
        _l1_out_body, name="l1_out",
        grid=(N_TILES,),
        in_specs=[prev, _tile_spec(D), nxt, _tile_spec(D), _tile_spec(D), _full_spec((8, D)),
                  _full_spec((D, D)), _full_spec((1, D)), _full_spec((D, 2 * META)), _full_spec((1, META))],
        out_specs=[_tile_spec(D), _row_tiles_spec(ROUTED), _tile_spec(META)],
        out_shape=[jax.ShapeDtypeStruct((T_ALL, D), F32), jax.ShapeDtypeStruct((T_ALL * ROUTED, LANES), F32),
                   jax.ShapeDtypeStruct((T_ALL, META), F32)],
        scratch_shapes=[pltpu.VMEM((TB + 2 * HALO, D), F32), pltpu.VMEM((TB, D), BF16)],
    )(cv, cv, cv, gate, x, conv_w, w_out, g, wr, br)


def _final_body(x_ref, y_ref, g_ref, o_ref):
    o_ref[...] = _rms(x_ref[...] + _read_row_tiles(y_ref), g_ref[...])


def _final(x, y, g, *, tile0, tiles):
    in_tile = pl.BlockSpec((TB, D), lambda i: (i + tile0, 0))
    y_tile = pl.BlockSpec((TB * ROW_TILE, LANES), lambda i: (i + tile0, 0))
    return pl.pallas_call(
        _final_body, name="final_norm",
        grid=(tiles,),
        in_specs=[in_tile, y_tile, _full_spec((1, D))],
        out_specs=_tile_spec(D),
        out_shape=jax.ShapeDtypeStruct((tiles * TB, D), F32),
    )(x, y, g)


def kernel(x_prompt, x_sample, l0_norm_mix, l0_w_in, l0_conv_w, l0_conv_b, l0_ln_g, l0_ln_b, l0_w_out, l0_norm_ffn, l0_w_coarse, l0_b_coarse, l0_w_fine, l0_b_fine, l0_w_gate, l0_w_up, l0_w_down, l1_norm_mix, l1_w_in, l1_conv_w, l1_w_out, l1_norm_ffn, l1_w_coarse, l1_b_coarse, l1_w_fine, l1_b_fine, l1_w_gate, l1_w_up, l1_w_down, final_norm):
    xp = x_prompt.reshape(T_PROMPT, D)
    xs = x_sample.reshape(T_SAMPLE, D)
    row = lambda v: v.reshape(1, -1)

    a, f = _l0_in(xp, xs, row(l0_norm_mix), l0_w_in.astype(BF16))
    conv_w = jnp.pad(l0_conv_w, ((0, 1), (0, 0)))
    a2 = _l0_conv(a, conv_w, row(l0_conv_b), row(l0_ln_g), row(l0_ln_b))
    d1, l2 = _dft_prompt_tables()
    cs_prompt = _dft2_prompt(_dft1_prompt(f, d1), l2)
    cs_sample = _dft_sample(f, _dft_sample_table())
    w_cs = _fold_channel_dft(l0_w_out[HALF:])
    wr0, br0 = _router_operands(l0_w_coarse, l0_b_coarse, l0_w_fine, l0_b_fine)
    x1, rows0, meta0 = _l0_out(a2, cs_prompt, cs_sample, xp, xs, l0_w_out[:HALF].astype(BF16), w_cs[:HALF],
                               w_cs[HALF:], row(l0_norm_ffn), wr0, br0)

    y0 = _moe(rows0, _moe_plan(meta0), l0_w_gate.astype(BF16), l0_w_up.astype(BF16), l0_w_down.astype(BF16))

    x2, gate, cv = _l1_in(x1, y0, row(l1_norm_mix), l1_w_in.astype(BF16))
    wr1, br1 = _router_operands(l1_w_coarse, l1_b_coarse, l1_w_fine, l1_b_fine)
    x3, rows1, meta1 = _l1_out(cv, gate, x2, jnp.pad(l1_conv_w, ((0, 5), (0, 0))), l1_w_out.astype(BF16),
                               row(l1_norm_ffn), wr1, br1)

    y1 = _moe(rows1, _moe_plan(meta1), l1_w_gate.astype(BF16), l1_w_up.astype(BF16), l1_w_down.astype(BF16))

    out_p = _final(x3, y1, row(final_norm), tile0=0, tiles=N_PROMPT_TILES)
    out_s = _final(x3, y1, row(final_norm), tile0=N_PROMPT_TILES, tiles=N_TILES - N_PROMPT_TILES)
    return out_p.reshape(x_prompt.shape), out_s.reshape(x_sample.shape)
```

```python
import functools
import math

import jax
import jax.numpy as jnp
import numpy as np
from jax import lax
from jax.experimental import pallas as pl
from jax.experimental.pallas import tpu as pltpu

D = 1024
LANES = 128
T_PROMPT = 16384
SEQ_SAMPLE = 2048
T_SAMPLE = 32 * SEQ_SAMPLE
T_ALL = T_PROMPT + T_SAMPLE
HALF = 512
GROUP_CH = 64
CONV_W = 31
N_GROUPS = 4
EPG = 8
N_PAIRS = EPG * (EPG - 1) // 2
N_BUCKETS = N_GROUPS * N_PAIRS
D_EXPERT = 512
RMS_EPS = 1e-6
LN_EPS = 1e-5

TB = 512
N_TILES = T_ALL // TB
N_PROMPT_TILES = T_PROMPT // TB
HALO = 16
CONV_ROWS = 64
MOE_ROWS = 256
N_MOE_BLOCKS = T_ALL // MOE_ROWS + N_BUCKETS + 1
META = 128
ROW_TILE = D // LANES
ROUTED = ROW_TILE + 1
FFT_N2 = 128
FFT_N1_PROMPT = T_PROMPT // FFT_N2
FFT_G1 = 16
FFT_KB = 8
DFT_ROWS = 1024

BF16 = jnp.bfloat16
F32 = jnp.float32


def _rms(x, g):
    return x * lax.rsqrt(jnp.mean(x * x, axis=-1, keepdims=True) + RMS_EPS) * g


def _dot(a, b):
    return jnp.dot(a, b, preferred_element_type=F32)


def _seq_edges(i):
    r0 = i * TB
    r1 = r0 + TB
    in_prompt = r0 < T_PROMPT
    starts = jnp.where(in_prompt, r0 == 0, (r0 - T_PROMPT) % SEQ_SAMPLE == 0)
    ends = jnp.where(in_prompt, r1 == T_PROMPT, (r1 - T_PROMPT) % SEQ_SAMPLE == 0)
    return starts, ends


def _tile_spec(width):
    return pl.BlockSpec((TB, width), lambda i: (i, 0))


def _row_tiles_spec(rows_per_token=ROW_TILE):
    return pl.BlockSpec((TB * rows_per_token, LANES), lambda i: (i, 0))


def _read_row_tiles(ref):
    return jnp.concatenate([ref[pl.ds(k, TB, stride=ROW_TILE), :] for k in range(ROW_TILE)], axis=1)


def _full_spec(shape):
    return pl.BlockSpec(shape, lambda *_: (0,) * len(shape))


def _prompt_spec(width=D):
    return pl.BlockSpec((TB, width), lambda i: (jnp.minimum(i, N_PROMPT_TILES - 1), 0))


def _sample_spec(width=D):
    return pl.BlockSpec((TB, width), lambda i: (jnp.maximum(i - N_PROMPT_TILES, 0), 0))


def _halo_specs(width):
    per_tile = TB // HALO
    last = T_ALL // HALO - 1
    prev = pl.BlockSpec((HALO, width), lambda i: (jnp.maximum(i * per_tile - 1, 0), 0))
    nxt = pl.BlockSpec((HALO, width), lambda i: (jnp.minimum((i + 1) * per_tile, last), 0))
    return prev, nxt


def _l0_in_body(xp_ref, xs_ref, g_ref, w_ref, a_ref, f_ref):
    i = pl.program_id(0)
    x = jnp.where(i < N_PROMPT_TILES, xp_ref[...], xs_ref[...])
    h = _rms(x, g_ref[...])
    u = _dot(h.astype(BF16), w_ref[...])
    a_ref[...] = u[:, :HALF] * jax.nn.sigmoid(u[:, HALF:2 * HALF])
    f_ref[...] = u[:, 2 * HALF:].astype(BF16)


def _l0_in(xp, xs, g, w_in):
    return pl.pallas_call(
        _l0_in_body, name="l0_in",
        grid=(N_TILES,),
        in_specs=[_prompt_spec(), _sample_spec(), _full_spec((1, D)), _full_spec((D, 3 * HALF))],
        out_specs=[_tile_spec(HALF), _tile_spec(HALF)],
        out_shape=[jax.ShapeDtypeStruct((T_ALL, HALF), F32), jax.ShapeDtypeStruct((T_ALL, HALF), BF16)],
    )(xp, xs, g, w_in)


def _conv_module(prev_ref, cur_ref, next_ref, w_ref, b_ref, g_ref, beta_ref, o_ref, buf):
    starts, ends = _seq_edges(pl.program_id(0))
    buf[0, 0:HALO, :] = jnp.where(starts, 0.0, prev_ref[...])
    buf[0, HALO:HALO + TB, :] = cur_ref[...]
    buf[0, HALO + TB:, :] = jnp.where(ends, 0.0, next_ref[...])
    rows = TB + 2 * HALO - 8
    for s in range(1, 8):
        for c0 in range(0, rows, 128):
            c1 = min(c0 + 128, rows)
            buf[s, c0:c1, :] = buf[0, c0 + s:c1 + s, :]
    first = HALO - CONV_W // 2
    for r0 in range(0, TB, CONV_ROWS):
        acc = jnp.broadcast_to(b_ref[...], (CONV_ROWS, HALF))
        for k in range(CONV_W):
            s = (first + k) % 8
            base = r0 + first + k - s
            acc = acc + w_ref[k:k + 1, :] * buf[s, base:base + CONV_ROWS, :]
        mu = jnp.mean(acc, axis=-1, keepdims=True)
        xc = acc - mu
        var = jnp.mean(xc * xc, axis=-1, keepdims=True)
        y = xc * lax.rsqrt(var + LN_EPS) * g_ref[...] + beta_ref[...]
        o_ref[r0:r0 + CONV_ROWS, :] = (y * jax.nn.sigmoid(y)).astype(BF16)


def _dft1_prompt_body(d_ref, x_ref, y_ref, xs, ys):
    n1 = FFT_N1_PROMPT
    x = x_ref[0].reshape(n1 * FFT_G1, HALF).astype(F32)
    for c in range(HALF // LANES):
        xs[c] = x[:, c * LANES:(c + 1) * LANES]
    for q in range(FFT_G1):
        rows = pl.ds(q, n1, stride=FFT_G1)
        xq = jnp.concatenate([xs[c, rows, :] for c in range(HALF // LANES)], axis=1)
        y = _dot(d_ref[...], xq.astype(BF16))
        for c in range(HALF // LANES):
            ys[0, c, rows, :] = y[:n1, c * LANES:(c + 1) * LANES]
            ys[1, c, rows, :] = y[n1:, c * LANES:(c + 1) * LANES]
    for p in range(2):
        for c in range(HALF // LANES):
            y_ref[p, :, :, c * LANES:(c + 1) * LANES] = ys[p, c].reshape(n1, FFT_G1, LANES)


def _dft1_prompt(f_all, d1):
    n1 = FFT_N1_PROMPT
    x = f_all.reshape(T_ALL // T_PROMPT, n1, FFT_N2, HALF)
    return pl.pallas_call(
        _dft1_prompt_body, name="dft1_prompt",
        grid=(FFT_N2 // FFT_G1,),
        in_specs=[_full_spec((2 * n1, n1)),
                  pl.BlockSpec((1, n1, FFT_G1, HALF), lambda j: (0, 0, j, 0))],
        out_specs=pl.BlockSpec((2, n1, FFT_G1, HALF), lambda j: (0, 0, j, 0)),
        out_shape=jax.ShapeDtypeStruct((2, n1, FFT_N2, HALF), F32),
        scratch_shapes=[pltpu.VMEM((HALF // LANES, n1 * FFT_G1, LANES), F32),
                        pltpu.VMEM((2, HALF // LANES, n1 * FFT_G1, LANES), F32)],
        compiler_params=pltpu.CompilerParams(vmem_limit_bytes=48 * 1024 * 1024),
    )(d1, x)


def _dft2_prompt_body(l_ref, y_ref, c_ref, s_ref, cs, ss):
    for q in range(FFT_KB):
        yq = y_ref[:, q].reshape(2 * FFT_N2, HALF).astype(BF16)
        r = _dot(l_ref[q], yq)
        rows = pl.ds(q, FFT_N2, stride=FFT_KB)
        for c in range(HALF // LANES):
            cs[c, rows, :] = r[:FFT_N2, c * LANES:(c + 1) * LANES]
            ss[c, rows, :] = r[FFT_N2:, c * LANES:(c + 1) * LANES]
    for c in range(HALF // LANES):
        c_ref[:, :, c * LANES:(c + 1) * LANES] = cs[c].reshape(FFT_N2, FFT_KB, LANES)
        s_ref[:, :, c * LANES:(c + 1) * LANES] = ss[c].reshape(FFT_N2, FFT_KB, LANES)


def _dft2_prompt(y, l2):
    n1 = FFT_N1_PROMPT
    out_sds = jax.ShapeDtypeStruct((FFT_N2, n1, HALF), F32)
    out_spec = pl.BlockSpec((FFT_N2, FFT_KB, HALF), lambda k: (0, k, 0))
    c, s = pl.pallas_call(
        _dft2_prompt_body, name="dft2_prompt",
        grid=(n1 // FFT_KB,),
        in_specs=[pl.BlockSpec((FFT_KB, 2 * FFT_N2, 2 * FFT_N2), lambda k: (k, 0, 0)),
                  pl.BlockSpec((2, FFT_KB, FFT_N2, HALF), lambda k: (0, k, 0, 0))],
        out_specs=[out_spec, out_spec],
        out_shape=[out_sds, out_sds],
        scratch_shapes=[pltpu.VMEM((HALF // LANES, FFT_N2 * FFT_KB, LANES), F32),
                        pltpu.VMEM((HALF // LANES, FFT_N2 * FFT_KB, LANES), F32)],
    )(l2, y)
    return c.reshape(T_PROMPT, HALF), s.reshape(T_PROMPT, HALF)


def _dft_sample_body(m_ref, x_ref, o_ref):
    for part in range(2):
        for r0 in range(0, SEQ_SAMPLE, DFT_ROWS):
            rows = m_ref[part * SEQ_SAMPLE + r0:part * SEQ_SAMPLE + r0 + DFT_ROWS, :]
            o_ref[part, r0:r0 + DFT_ROWS, :] = _dot(rows, x_ref[...]).astype(BF16)


def _dft_sample(f_all, m):
    first = T_PROMPT // SEQ_SAMPLE
    return pl.pallas_call(
        _dft_sample_body, name="dft_sample",
        grid=(T_SAMPLE // SEQ_SAMPLE,),
        in_specs=[pl.BlockSpec((2 * SEQ_SAMPLE, SEQ_SAMPLE), lambda b: (0, 0), pipeline_mode=pl.Buffered(1)),
                  pl.BlockSpec((SEQ_SAMPLE, HALF), lambda b: (first + b, 0))],
        out_specs=pl.BlockSpec((2, SEQ_SAMPLE, HALF), lambda b: (0, b, 0)),
        out_shape=jax.ShapeDtypeStruct((2, T_SAMPLE, HALF), BF16),
        compiler_params=pltpu.CompilerParams(vmem_limit_bytes=48 * 1024 * 1024),
    )(m, f_all)


def _dft_sample_table():
    k = jnp.arange(SEQ_SAMPLE, dtype=jnp.int32)
    ang = (2.0 * math.pi / SEQ_SAMPLE) * ((k[:, None] * k[None, :]) % SEQ_SAMPLE).astype(F32)
    scale = SEQ_SAMPLE ** -0.5
    return jnp.concatenate([jnp.cos(ang) * scale, jnp.sin(ang) * scale], axis=0).astype(BF16)


def _dft_prompt_tables():
    n1 = FFT_N1_PROMPT
    n = n1 * FFT_N2
    k1 = jnp.arange(n1, dtype=jnp.int32)
    ang1 = (2.0 * math.pi / n1) * ((k1[:, None] * k1[None, :]) % n1).astype(F32)
    d1 = jnp.concatenate([jnp.cos(ang1), -jnp.sin(ang1)], axis=0).astype(BF16)
    k2 = jnp.arange(FFT_N2, dtype=jnp.int32)
    k = k1[:, None, None] + n1 * k2[None, :, None]
    ang2 = (2.0 * math.pi / n) * ((k * k2[None, None, :]) % n).astype(F32)
    cr = jnp.cos(ang2) * (n ** -0.5)
    sr = jnp.sin(ang2) * (n ** -0.5)
    top = jnp.concatenate([cr, sr], axis=2)
    bot = jnp.concatenate([sr, -cr], axis=2)
    l2 = jnp.concatenate([top, bot], axis=1).astype(BF16)
    return d1, l2


def _fold_body(m_ref, w_ref, o_ref):
    o_ref[...] = jnp.dot(m_ref[...], w_ref[...], preferred_element_type=F32,
                         precision=lax.Precision.HIGHEST).astype(BF16)


def _fold_channel_dft(w_out_f):
    c = jnp.arange(HALF, dtype=jnp.int32)
    same = (c[:, None] // GROUP_CH) == (c[None, :] // GROUP_CH)
    ang = (2.0 * math.pi / GROUP_CH) * ((c[:, None] * c[None, :]) % GROUP_CH).astype(F32)
    scale = GROUP_CH ** -0.5
    cbd = jnp.where(same, jnp.cos(ang), 0.0) * scale
    sbd = jnp.where(same, jnp.sin(ang), 0.0) * scale
    m = jnp.concatenate([cbd, -sbd], axis=0)
    return pl.pallas_call(
        _fold_body, name="fold_channel_dft",
        out_shape=jax.ShapeDtypeStruct((2 * HALF, D), BF16),
    )(m, w_out_f)


def _route_and_store(x_new, g_ref, wr_ref, br_ref, x_out_ref, row_ref, meta_ref):
    x_out_ref[...] = x_new
    h = _rms(x_new, g_ref[...])
    h_hi = h.astype(BF16)
    h_lo = (h - h_hi.astype(F32)).astype(BF16)
    p = _dot(h_hi, wr_ref[...])
    q = _dot(h_lo, wr_ref[...])
    lg = p[:, :META] + p[:, META:] + q[:, :META] + q[:, META:] + br_ref[...]
    lane = lax.broadcasted_iota(jnp.int32, lg.shape, 1).astype(F32)
    neg = -jnp.inf
    lc = jnp.where(lane < N_GROUPS, lg, neg)
    mc = jnp.max(lc, axis=-1, keepdims=True)
    grp = jnp.min(jnp.where(lc == mc, lane, META), axis=-1, keepdims=True)
    p_grp = 1.0 / jnp.sum(jnp.exp(lc - mc), axis=-1, keepdims=True)
    lo = N_GROUPS + EPG * grp
    lf = jnp.where((lane >= lo) & (lane < lo + EPG), lg, neg)
    m1 = jnp.max(lf, axis=-1, keepdims=True)
    i1 = jnp.min(jnp.where(lf == m1, lane, META), axis=-1, keepdims=True)
    lf2 = jnp.where(lane == i1, neg, lf)
    m2 = jnp.max(lf2, axis=-1, keepdims=True)
    i2 = jnp.min(jnp.where(lf2 == m2, lane, META), axis=-1, keepdims=True)
    e = jnp.exp(m2 - m1)
    gate1 = p_grp / (1.0 + e)
    gate2 = p_grp * e / (1.0 + e)
    j1 = i1 - lo
    j2 = i2 - lo
    ja = jnp.minimum(j1, j2)
    jb = jnp.maximum(j1, j2)
    bucket = grp * N_PAIRS + (ja * (2 * EPG - 1 - ja)) * 0.5 + (jb - ja - 1)
    gate_a = jnp.where(j1 < j2, gate1, gate2)
    gate_b = jnp.where(j1 < j2, gate2, gate1)
    meta = jnp.where(lane == 0, gate_a,
                     jnp.where(lane == 1, gate_b,
                               jnp.where(lane == 2, bucket, 0.0)))
    for k in range(ROW_TILE):
        row_ref[pl.ds(k, TB, stride=ROUTED), :] = h[:, k * LANES:(k + 1) * LANES]
    row_ref[pl.ds(ROW_TILE, TB, stride=ROUTED), :] = meta
    meta_ref[...] = meta


def _router_operands(w_coarse, b_coarse, w_fine, b_fine):
    w = jnp.concatenate([w_coarse, w_fine.reshape(D, N_GROUPS * EPG)], axis=1)
    w = jnp.pad(w, ((0, 0), (0, META - w.shape[1])))
    w_hi = w.astype(BF16)
    w_lo = (w - w_hi.astype(F32)).astype(BF16)
    b = jnp.concatenate([b_coarse, b_fine.reshape(-1)])
    b = jnp.pad(b, (0, META - b.shape[0])).reshape(1, META)
    return jnp.concatenate([w_hi, w_lo], axis=1), b


def _l0_out_body(prev_ref, cur_ref, next_ref, cw_ref, cb_ref, lg_ref, lb_ref,
                 cp_ref, sp_ref, cs_ref, ss_ref, xp_ref, xs_ref, wa_ref, wc_ref, ws_ref,
                 g_ref, wr_ref, br_ref, x_out_ref, row_ref, meta_ref, buf, a2):
    in_prompt = pl.program_id(0) < N_PROMPT_TILES
    x = jnp.where(in_prompt, xp_ref[...], xs_ref[...])
    c = jnp.where(in_prompt, cp_ref[...].astype(BF16), cs_ref[...])
    s = jnp.where(in_prompt, sp_ref[...].astype(BF16), ss_ref[...])
    mix = _dot(c, wc_ref[...]) + _dot(s, ws_ref[...])
    _conv_module(prev_ref, cur_ref, next_ref, cw_ref, cb_ref, lg_ref, lb_ref, a2, buf)
    mix = mix + _dot(a2[...], wa_ref[...])
    _route_and_store(x + mix, g_ref, wr_ref, br_ref, x_out_ref, row_ref, meta_ref)


def _l0_out(a, conv_w, conv_b, ln_g, ln_b, cs_prompt, cs_sample, xp, xs, wa, wc, ws, g, wr, br):
    def sample_half(which):
        return pl.BlockSpec((None, TB, HALF), lambda i: (which, jnp.maximum(i - N_PROMPT_TILES, 0), 0))
    prev, nxt = _halo_specs(HALF)
    return pl.pallas_call(
        _l0_out_body, name="l0_out",
        grid=(N_TILES,),
        in_specs=[prev, _tile_spec(HALF), nxt, _full_spec((CONV_W + 1, HALF)),
                  _full_spec((1, HALF)), _full_spec((1, HALF)), _full_spec((1, HALF)),
                  _prompt_spec(HALF), _prompt_spec(HALF), sample_half(0), sample_half(1),
                  _prompt_spec(), _sample_spec(),
                  _full_spec((HALF, D)), _full_spec((HALF, D)), _full_spec((HALF, D)),
                  _full_spec((1, D)), _full_spec((D, 2 * META)), _full_spec((1, META))],
        out_specs=[_tile_spec(D), _row_tiles_spec(ROUTED), _tile_spec(META)],
        out_shape=[jax.ShapeDtypeStruct((T_ALL, D), F32), jax.ShapeDtypeStruct((T_ALL * ROUTED, LANES), F32),
                   jax.ShapeDtypeStruct((T_ALL, META), F32)],
        scratch_shapes=[pltpu.VMEM((8, TB + 2 * HALO, HALF), F32), pltpu.VMEM((TB, HALF), BF16)],
    )(a, a, a, conv_w, conv_b, ln_g, ln_b, *cs_prompt, cs_sample, cs_sample, xp, xs, wa, wc, ws, g, wr, br)


def _moe_plan(meta):
    bucket = meta[:, 2].astype(jnp.int32)
    sorted_b, order = lax.sort((bucket, jnp.arange(T_ALL, dtype=jnp.int32)), num_keys=1, is_stable=True)
    edges = jnp.arange(N_BUCKETS + 1, dtype=jnp.int32)
    starts = jnp.sum(sorted_b[None, :] < edges[:, None], axis=1).astype(jnp.int32)
    counts = starts[1:] - starts[:-1]
    nblk = (counts + MOE_ROWS - 1) // MOE_ROWS
    cum = jnp.cumsum(nblk)
    first = cum - nblk
    j = jnp.arange(N_MOE_BLOCKS, dtype=jnp.int32)
    valid = j < cum[-1]
    j_eff = jnp.minimum(j, cum[-1] - 1)
    bj = jnp.minimum(jnp.sum(cum[None, :] <= j_eff[:, None], axis=1), N_BUCKETS - 1).astype(jnp.int32)
    off = (j_eff - first[bj]) * MOE_ROWS
    cnt = jnp.where(valid, jnp.clip(counts[bj] - off, 0, MOE_ROWS), 0).astype(jnp.int32)
    r = jnp.arange(MOE_ROWS, dtype=jnp.int32)[None, :]
    pos = jnp.clip(starts[bj][:, None] + off[:, None] + r, 0, T_ALL - 1)
    tok = order[pos]
    spare = T_ALL + (j % 2)[:, None] * MOE_ROWS + r
    src = (tok * ROUTED).reshape(N_MOE_BLOCKS, 1, MOE_ROWS)
    dst = (jnp.where(r < cnt[:, None], tok, spare) * ROW_TILE).reshape(N_MOE_BLOCKS, 1, MOE_ROWS)
    pa, pb = np.triu_indices(EPG, k=1)
    grp = bj // N_PAIRS
    ea = grp * EPG + jnp.asarray(pa, jnp.int32)[bj % N_PAIRS]
    eb = grp * EPG + jnp.asarray(pb, jnp.int32)[bj % N_PAIRS]
    return ea.astype(jnp.int32), eb.astype(jnp.int32), cnt, src, dst


def _moe_body(ea_ref, eb_ref, cnt_ref, src_ref, dst_ref, rows_hbm,
              wga_ref, wua_ref, wda_ref, wgb_ref, wub_ref, wdb_ref,
              y_hbm, xbuf, ybuf, gsem, ssem):
    s = pl.program_id(0)
    tile = ROW_TILE
    buf_rows = MOE_ROWS * tile

    def rows_in(b):
        inside = (b >= 0) & (b < N_MOE_BLOCKS)
        return jnp.where(inside, cnt_ref[jnp.clip(b, 0, N_MOE_BLOCKS - 1)], 0)

    def wait_fetch(p):
        pltpu.make_async_copy(rows_hbm.at[pl.ds(0, MOE_ROWS * ROUTED)], xbuf.at[p], gsem.at[p]).wait()

    def wait_send(p):
        pltpu.make_async_copy(ybuf.at[p], y_hbm.at[pl.ds(0, buf_rows)], ssem.at[p]).wait()

    @pl.when(s == 0)
    def _():
        ybuf[...] = jnp.zeros(ybuf.shape, ybuf.dtype)
        for p in range(2):
            clear = pltpu.make_async_copy(ybuf.at[p], y_hbm.at[pl.ds((T_ALL + p * MOE_ROWS) * tile, buf_rows)],
                                          ssem.at[p])
            clear.start()
            clear.wait()

    for p in range(2):
        q = 1 - p
        mine = (s % 2) == p

        @pl.when(mine & (rows_in(s - 3) > 0))
        def _():
            wait_send(q)

        @pl.when(mine & (rows_in(s) > 0))
        def _():
            for r in range(MOE_ROWS):
                pltpu.make_async_copy(rows_hbm.at[pl.ds(src_ref[0, 0, r], ROUTED)],
                                      xbuf.at[p, pl.ds(r * ROUTED, ROUTED)], gsem.at[p]).start(priority=r % 2)

        @pl.when(mine & (rows_in(s - 2) > 0))
        def _():
            for r in range(MOE_ROWS):
                start = pl.multiple_of(dst_ref[0, 0, r], tile)
                pltpu.make_async_copy(ybuf.at[p, pl.ds(r * tile, tile)], y_hbm.at[pl.ds(start, tile)],
                                      ssem.at[p]).start(priority=r % 2)

        @pl.when(mine & (rows_in(s - 1) > 0))
        def _():
            wait_fetch(q)
            x = jnp.concatenate([xbuf[q, pl.ds(k, MOE_ROWS, stride=ROUTED), :] for k in range(tile)],
                                axis=1).astype(BF16)
            record = xbuf[q, pl.ds(tile, MOE_ROWS, stride=ROUTED), :]
            gate_a = record[:, 0:1]
            gate_b = record[:, 1:2]

            def hidden(wg_ref, wu_ref, gate):
                g = _dot(x, wg_ref[0])
                u = _dot(x, wu_ref[0])
                return (g * jax.nn.sigmoid(g) * u * gate).astype(BF16)

            y = (_dot(hidden(wga_ref, wua_ref, gate_a), wda_ref[0])
                 + _dot(hidden(wgb_ref, wub_ref, gate_b), wdb_ref[0]))
            for k in range(tile):
                ybuf[q, pl.ds(k, MOE_ROWS, stride=tile), :] = y[:, k * LANES:(k + 1) * LANES]


def _moe(rows, plan, w_gate, w_up, w_down):
    ea, eb, cnt, src, dst = plan
    last = N_MOE_BLOCKS - 1

    def block_of(step, lag):
        return jnp.clip(step - lag, 0, last)

    src_spec = pl.BlockSpec((1, 1, MOE_ROWS), lambda s, *_: (block_of(s, 0), 0, 0), memory_space=pltpu.SMEM)
    dst_spec = pl.BlockSpec((1, 1, MOE_ROWS), lambda s, *_: (block_of(s, 2), 0, 0), memory_space=pltpu.SMEM)
    up_a = pl.BlockSpec((1, D, D_EXPERT), lambda s, ea, eb, cnt: (ea[block_of(s, 1)], 0, 0))
    up_b = pl.BlockSpec((1, D, D_EXPERT), lambda s, ea, eb, cnt: (eb[block_of(s, 1)], 0, 0))
    down_a = pl.BlockSpec((1, D_EXPERT, D), lambda s, ea, eb, cnt: (ea[block_of(s, 1)], 0, 0))
    down_b = pl.BlockSpec((1, D_EXPERT, D), lambda s, ea, eb, cnt: (eb[block_of(s, 1)], 0, 0))
    grid_spec = pltpu.PrefetchScalarGridSpec(
        num_scalar_prefetch=3,
        grid=(N_MOE_BLOCKS + 2,),
        in_specs=[src_spec, dst_spec, pl.BlockSpec(memory_space=pl.ANY),
                  up_a, up_a, down_a, up_b, up_b, down_b],
        out_specs=pl.BlockSpec(memory_space=pl.ANY),
        scratch_shapes=[pltpu.VMEM((2, MOE_ROWS * ROUTED, LANES), F32),
                        pltpu.VMEM((2, MOE_ROWS * ROW_TILE, LANES), F32),
                        pltpu.SemaphoreType.DMA((2,)), pltpu.SemaphoreType.DMA((2,))],
    )
    return pl.pallas_call(
        _moe_body, name="moe",
        grid_spec=grid_spec,
        out_shape=jax.ShapeDtypeStruct(((T_ALL + 2 * MOE_ROWS) * ROW_TILE, LANES), F32),
        compiler_params=pltpu.CompilerParams(dimension_semantics=("arbitrary",)),
    )(ea, eb, cnt, src, dst, rows, w_gate, w_up, w_down, w_gate, w_up, w_down)


def _l1_in_body(x_ref, y_ref, g_ref, w_ref, x_out_ref, gate_ref, cv_ref):
    x = x_ref[...] + _read_row_tiles(y_ref)
    x_out_ref[...] = x
    u = _dot(_rms(x, g_ref[...]).astype(BF16), w_ref[...])
    gate_ref[...] = u[:, :D].astype(BF16)
    cv_ref[...] = (u[:, D:2 * D] * u[:, 2 * D:]).astype(BF16)


def _l1_in(x, y, g, w_in):
    return pl.pallas_call(
        _l1_in_body, name="l1_in",
        grid=(N_TILES,),
        in_specs=[_tile_spec(D), _row_tiles_spec(), _full_spec((1, D)), _full_spec((D, 3 * D))],
        out_specs=[_tile_spec(D), _tile_spec(D), _tile_spec(D)],
        out_shape=[jax.ShapeDtypeStruct((T_ALL, D), F32), jax.ShapeDtypeStruct((T_ALL, D), BF16),
                   jax.ShapeDtypeStruct((T_ALL, D), BF16)],
    )(x, y, g, w_in)


def _l1_out_body(prev_ref, cur_ref, next_ref, gate_ref, x_ref, cw_ref, wo_ref, g_ref, wr_ref, br_ref,
                 x_out_ref, row_ref, meta_ref, buf, ybuf):
    starts, ends = _seq_edges(pl.program_id(0))
    buf[0:HALO, :] = jnp.where(starts, 0.0, prev_ref[...].astype(F32))
    buf[HALO:HALO + TB, :] = cur_ref[...].astype(F32)
    buf[HALO + TB:, :] = jnp.where(ends, 0.0, next_ref[...].astype(F32))
    rows = CONV_ROWS // 2
    for r0 in range(0, TB, rows):
        conv = (cw_ref[0:1, :] * buf[r0 + HALO - 1:r0 + HALO - 1 + rows, :]
                + cw_ref[1:2, :] * buf[r0 + HALO:r0 + HALO + rows, :]
                + cw_ref[2:3, :] * buf[r0 + HALO + 1:r0 + HALO + 1 + rows, :])
        ybuf[r0:r0 + rows, :] = (gate_ref[r0:r0 + rows, :].astype(F32) * conv).astype(BF16)
    _route_and_store(x_ref[...] + _dot(ybuf[...], wo_ref[...]), g_ref, wr_ref, br_ref, x_out_ref, row_ref, meta_ref)


def _l1_out(cv, gate, x, conv_w, w_out, g, wr, br):
    prev, nxt = _halo_specs(D)
    return pl.pallas_call(
        _l1_out_body, name="l1_out",
        grid=(N_TILES,),
        in_specs=[prev, _tile_spec(D), nxt, _tile_spec(D), _tile_spec(D), _full_spec((8, D)),
                  _full_spec((D, D)), _full_spec((1, D)), _full_spec((D, 2 * META)), _full_spec((1, META))],
        out_specs=[_tile_spec(D), _row_tiles_spec(ROUTED), _tile_spec(META)],
        out_shape=[jax.ShapeDtypeStruct((T_ALL, D), F32), jax.ShapeDtypeStruct((T_ALL * ROUTED, LANES), F32),
                   jax.ShapeDtypeStruct((T_ALL, META), F32)],
        scratch_shapes=[pltpu.VMEM((TB + 2 * HALO, D), F32), pltpu.VMEM((TB, D), BF16)],
    )(cv, cv, cv, gate, x, conv_w, w_out, g, wr, br)


def _final_body(x_ref, y_ref, g_ref, o_ref):
    o_ref[...] = _rms(x_ref[...] + _read_row_tiles(y_ref), g_ref[...])


def _final(x, y, g, *, tile0, tiles):
    in_tile = pl.BlockSpec((TB, D), lambda i: (i + tile0, 0))
    y_tile = pl.BlockSpec((TB * ROW_TILE, LANES), lambda i: (i + tile0, 0))
    return pl.pallas_call(
        _final_body, name="final_norm",
        grid=(tiles,),
        in_specs=[in_tile, y_tile, _full_spec((1, D))],
        out_specs=_tile_spec(D),
        out_shape=jax.ShapeDtypeStruct((tiles * TB, D), F32),
    )(x, y, g)


def kernel(x_prompt, x_sample, l0_norm_mix, l0_w_in, l0_conv_w, l0_conv_b, l0_ln_g, l0_ln_b, l0_w_out, l0_norm_ffn, l0_w_coarse, l0_b_coarse, l0_w_fine, l0_b_fine, l0_w_gate, l0_w_up, l0_w_down, l1_norm_mix, l1_w_in, l1_conv_w, l1_w_out, l1_norm_ffn, l1_w_coarse, l1_b_coarse, l1_w_fine, l1_b_fine, l1_w_gate, l1_w_up, l1_w_down, final_norm):
    xp = x_prompt.reshape(T_PROMPT, D)
    xs = x_sample.reshape(T_SAMPLE, D)
    row = lambda v: v.reshape(1, -1)

    a, f = _l0_in(xp, xs, row(l0_norm_mix), l0_w_in.astype(BF16))
    conv_w = jnp.pad(l0_conv_w, ((0, 1), (0, 0)))
    d1, l2 = _dft_prompt_tables()
    cs_prompt = _dft2_prompt(_dft1_prompt(f, d1), l2)
    cs_sample = _dft_sample(f, _dft_sample_table())
    w_cs = _fold_channel_dft(l0_w_out[HALF:])
    wr0, br0 = _router_operands(l0_w_coarse, l0_b_coarse, l0_w_fine, l0_b_fine)
    x1, rows0, meta0 = _l0_out(a, conv_w, row(l0_conv_b), row(l0_ln_g), row(l0_ln_b), cs_prompt, cs_sample, xp, xs,
                               l0_w_out[:HALF].astype(BF16), w_cs[:HALF], w_cs[HALF:], row(l0_norm_ffn), wr0, br0)

    y0 = _moe(rows0, _moe_plan(meta0), l0_w_gate.astype(BF16), l0_w_up.astype(BF16), l0_w_down.astype(BF16))

    x2, gate, cv = _l1_in(x1, y0, row(l1_norm_mix), l1_w_in.astype(BF16))
    wr1, br1 = _router_operands(l1_w_coarse, l1_b_coarse, l1_w_fine, l1_b_fine)
    x3, rows1, meta1 = _l1_out(cv, gate, x2, jnp.pad(l1_conv_w, ((0, 5), (0, 0))), l1_w_out.astype(BF16),
                               row(l1_norm_ffn), wr1, br1)

    y1 = _moe(rows1, _moe_plan(meta1), l1_w_gate.astype(BF16), l1_w_up.astype(BF16), l1_w_down.astype(BF16))

    out_p = _final(x3, y1, row(final_norm), tile0=0, tiles=N_PROMPT_TILES)
    out_s = _final(x3, y1, row(final_norm), tile0=N_PROMPT_TILES, tiles=N_TILES - N_PROMPT_TILES)
    return out_p.reshape(x_prompt.shape), out_s.reshape(x_sample.shape)
```

```python
import functools
import math

import jax
import jax.numpy as jnp
import numpy as np
from jax import lax
from jax.experimental import pallas as pl
from jax.experimental.pallas import tpu as pltpu

D = 1024
LANES = 128
T_PROMPT = 16384
SEQ_SAMPLE = 2048
T_SAMPLE = 32 * SEQ_SAMPLE
T_ALL = T_PROMPT + T_SAMPLE
HALF = 512
GROUP_CH = 64
CONV_W = 31
N_GROUPS = 4
EPG = 8
N_PAIRS = EPG * (EPG - 1) // 2
N_BUCKETS = N_GROUPS * N_PAIRS
D_EXPERT = 512
RMS_EPS = 1e-6
LN_EPS = 1e-5

TB = 512
N_TILES = T_ALL // TB
N_PROMPT_TILES = T_PROMPT // TB
HALO = 16
CONV_ROWS = 64
MOE_ROWS = 256
N_MOE_BLOCKS = T_ALL // MOE_ROWS + N_BUCKETS + 1
META = 128
ROW_TILE = D // LANES
ROUTED = ROW_TILE + 1
FFT_N2 = 128
FFT_N1_PROMPT = T_PROMPT // FFT_N2
FFT_G1 = 16
FFT_KB = 8
DFT_ROWS = 1024

BF16 = jnp.bfloat16
F32 = jnp.float32


def _rms(x, g):
    return x * lax.rsqrt(jnp.mean(x * x, axis=-1, keepdims=True) + RMS_EPS) * g


def _dot(a, b):
    return jnp.dot(a, b, preferred_element_type=F32)


def _seq_edges(i):
    r0 = i * TB
    r1 = r0 + TB
    in_prompt = r0 < T_PROMPT
    starts = jnp.where(in_prompt, r0 == 0, (r0 - T_PROMPT) % SEQ_SAMPLE == 0)
    ends = jnp.where(in_prompt, r1 == T_PROMPT, (r1 - T_PROMPT) % SEQ_SAMPLE == 0)
    return starts, ends


def _tile_spec(width):
    return pl.BlockSpec((TB, width), lambda i: (i, 0))


def _row_tiles_spec(rows_per_token=ROW_TILE):
    return pl.BlockSpec((TB * rows_per_token, LANES), lambda i: (i, 0))


def _read_row_tiles(ref):
    return jnp.concatenate([ref[pl.ds(k, TB, stride=ROW_TILE), :] for k in range(ROW_TILE)], axis=1)


def _full_spec(shape):
    return pl.BlockSpec(shape, lambda *_: (0,) * len(shape))


def _prompt_spec(width=D):
    return pl.BlockSpec((TB, width), lambda i: (jnp.minimum(i, N_PROMPT_TILES - 1), 0))


def _sample_spec(width=D):
    return pl.BlockSpec((TB, width), lambda i: (jnp.maximum(i - N_PROMPT_TILES, 0), 0))


def _halo_specs(width):
    per_tile = TB // HALO
    last = T_ALL // HALO - 1
    prev = pl.BlockSpec((HALO, width), lambda i: (jnp.maximum(i * per_tile - 1, 0), 0))
    nxt = pl.BlockSpec((HALO, width), lambda i: (jnp.minimum((i + 1) * per_tile, last), 0))
    return prev, nxt


def _l0_in_body(xp_ref, xs_ref, g_ref, w_ref, a_ref, f_ref):
    i = pl.program_id(0)
    x = jnp.where(i < N_PROMPT_TILES, xp_ref[...], xs_ref[...])
    h = _rms(x, g_ref[...])
    u = _dot(h.astype(BF16), w_ref[...])
    a_ref[...] = (u[:, :HALF] * jax.nn.sigmoid(u[:, HALF:2 * HALF])).astype(BF16)
    f_ref[...] = u[:, 2 * HALF:].astype(BF16)


def _l0_in(xp, xs, g, w_in):
    return pl.pallas_call(
        _l0_in_body, name="l0_in",
        grid=(N_TILES,),
        in_specs=[_prompt_spec(), _sample_spec(), _full_spec((1, D)), _full_spec((D, 3 * HALF))],
        out_specs=[_tile_spec(HALF), _tile_spec(HALF)],
        out_shape=[jax.ShapeDtypeStruct((T_ALL, HALF), BF16), jax.ShapeDtypeStruct((T_ALL, HALF), BF16)],
    )(xp, xs, g, w_in)


def _conv_module(top, mid, bottom, w_ref, b_ref, g_ref, beta_ref, o_ref, out_row0, buf):
    buf[0, 0:HALO, :] = top.astype(F32)
    buf[0, HALO:HALO + TB, :] = mid.astype(F32)
    buf[0, HALO + TB:, :] = bottom.astype(F32)
    rows = TB + 2 * HALO - 8
    for s in range(1, 8):
        for c0 in range(0, rows, 128):
            c1 = min(c0 + 128, rows)
            buf[s, c0:c1, :] = buf[0, c0 + s:c1 + s, :]
    first = HALO - CONV_W // 2
    for r0 in range(0, TB, CONV_ROWS):
        acc = jnp.broadcast_to(b_ref[...], (CONV_ROWS, HALF))
        for k in range(CONV_W):
            s = (first + k) % 8
            base = r0 + first + k - s
            acc = acc + w_ref[k:k + 1, :] * buf[s, base:base + CONV_ROWS, :]
        mu = jnp.mean(acc, axis=-1, keepdims=True)
        xc = acc - mu
        var = jnp.mean(xc * xc, axis=-1, keepdims=True)
        y = xc * lax.rsqrt(var + LN_EPS) * g_ref[...] + beta_ref[...]
        o_ref[out_row0 + r0:out_row0 + r0 + CONV_ROWS, :] = (y * jax.nn.sigmoid(y)).astype(BF16)


def _dft1_prompt_body(d_ref, x_ref, y_ref, xs, ys):
    n1 = FFT_N1_PROMPT
    x = x_ref[0].reshape(n1 * FFT_G1, HALF).astype(F32)
    for c in range(HALF // LANES):
        xs[c] = x[:, c * LANES:(c + 1) * LANES]
    for q in range(FFT_G1):
        rows = pl.ds(q, n1, stride=FFT_G1)
        xq = jnp.concatenate([xs[c, rows, :] for c in range(HALF // LANES)], axis=1)
        y = _dot(d_ref[...], xq.astype(BF16))
        for c in range(HALF // LANES):
            ys[0, c, rows, :] = y[:n1, c * LANES:(c + 1) * LANES]
            ys[1, c, rows, :] = y[n1:, c * LANES:(c + 1) * LANES]
    for p in range(2):
        for c in range(HALF // LANES):
            y_ref[p, :, :, c * LANES:(c + 1) * LANES] = ys[p, c].reshape(n1, FFT_G1, LANES)


def _dft1_prompt(f_all, d1):
    n1 = FFT_N1_PROMPT
    x = f_all.reshape(T_ALL // T_PROMPT, n1, FFT_N2, HALF)
    return pl.pallas_call(
        _dft1_prompt_body, name="dft1_prompt",
        grid=(FFT_N2 // FFT_G1,),
        in_specs=[_full_spec((2 * n1, n1)),
                  pl.BlockSpec((1, n1, FFT_G1, HALF), lambda j: (0, 0, j, 0))],
        out_specs=pl.BlockSpec((2, n1, FFT_G1, HALF), lambda j: (0, 0, j, 0)),
        out_shape=jax.ShapeDtypeStruct((2, n1, FFT_N2, HALF), F32),
        scratch_shapes=[pltpu.VMEM((HALF // LANES, n1 * FFT_G1, LANES), F32),
                        pltpu.VMEM((2, HALF // LANES, n1 * FFT_G1, LANES), F32)],
        compiler_params=pltpu.CompilerParams(vmem_limit_bytes=48 * 1024 * 1024),
    )(d1, x)


def _dft2_prompt_body(l_ref, y_ref, c_ref, s_ref, cs, ss):
    for q in range(FFT_KB):
        yq = y_ref[:, q].reshape(2 * FFT_N2, HALF).astype(BF16)
        r = _dot(l_ref[q], yq)
        rows = pl.ds(q, FFT_N2, stride=FFT_KB)
        for c in range(HALF // LANES):
            cs[c, rows, :] = r[:FFT_N2, c * LANES:(c + 1) * LANES]
            ss[c, rows, :] = r[FFT_N2:, c * LANES:(c + 1) * LANES]
    for c in range(HALF // LANES):
        c_ref[:, :, c * LANES:(c + 1) * LANES] = cs[c].reshape(FFT_N2, FFT_KB, LANES)
        s_ref[:, :, c * LANES:(c + 1) * LANES] = ss[c].reshape(FFT_N2, FFT_KB, LANES)


def _dft2_prompt(y, l2):
    n1 = FFT_N1_PROMPT
    out_sds = jax.ShapeDtypeStruct((FFT_N2, n1, HALF), F32)
    out_spec = pl.BlockSpec((FFT_N2, FFT_KB, HALF), lambda k: (0, k, 0))
    c, s = pl.pallas_call(
        _dft2_prompt_body, name="dft2_prompt",
        grid=(n1 // FFT_KB,),
        in_specs=[pl.BlockSpec((FFT_KB, 2 * FFT_N2, 2 * FFT_N2), lambda k: (k, 0, 0)),
                  pl.BlockSpec((2, FFT_KB, FFT_N2, HALF), lambda k: (0, k, 0, 0))],
        out_specs=[out_spec, out_spec],
        out_shape=[out_sds, out_sds],
        scratch_shapes=[pltpu.VMEM((HALF // LANES, FFT_N2 * FFT_KB, LANES), F32),
                        pltpu.VMEM((HALF // LANES, FFT_N2 * FFT_KB, LANES), F32)],
    )(l2, y)
    return c.reshape(T_PROMPT, HALF), s.reshape(T_PROMPT, HALF)


def _sample_mixers_body(m_ref, f_ref, a_ref, w_ref, b_ref, g_ref, beta_ref, cs_ref, a2_ref, buf):
    def dft(part, r0):
        rows = m_ref[part * SEQ_SAMPLE + r0:part * SEQ_SAMPLE + r0 + DFT_ROWS, :]
        cs_ref[part, r0:r0 + DFT_ROWS, :] = _dot(rows, f_ref[...]).astype(BF16)

    def conv(t0):
        zeros = jnp.zeros((HALO, HALF), F32)
        top = zeros if t0 == 0 else a_ref[t0 - HALO:t0, :]
        bottom = zeros if t0 + TB == SEQ_SAMPLE else a_ref[t0 + TB:t0 + TB + HALO, :]
        _conv_module(top, a_ref[t0:t0 + TB, :], bottom, w_ref, b_ref, g_ref, beta_ref, a2_ref, t0, buf)

    matmuls = [(part, r0) for part in range(2) for r0 in range(0, SEQ_SAMPLE, DFT_ROWS)]
    tiles = list(range(0, SEQ_SAMPLE, TB))
    per_tile = len(matmuls) // len(tiles)
    for i, t0 in enumerate(tiles):
        for part, r0 in matmuls[i * per_tile:(i + 1) * per_tile]:
            dft(part, r0)
        conv(t0)


def _sample_mixers(f_all, a_all, m, conv_w, conv_b, ln_g, ln_b):
    first = T_PROMPT // SEQ_SAMPLE
    seq = pl.BlockSpec((SEQ_SAMPLE, HALF), lambda b: (first + b, 0))
    return pl.pallas_call(
        _sample_mixers_body, name="sample_mixers",
        grid=(T_SAMPLE // SEQ_SAMPLE,),
        in_specs=[pl.BlockSpec((2 * SEQ_SAMPLE, SEQ_SAMPLE), lambda b: (0, 0), pipeline_mode=pl.Buffered(1)),
                  seq, seq, _full_spec((CONV_W + 1, HALF)),
                  _full_spec((1, HALF)), _full_spec((1, HALF)), _full_spec((1, HALF))],
        out_specs=[pl.BlockSpec((2, SEQ_SAMPLE, HALF), lambda b: (0, b, 0)),
                   pl.BlockSpec((SEQ_SAMPLE, HALF), lambda b: (b, 0))],
        out_shape=[jax.ShapeDtypeStruct((2, T_SAMPLE, HALF), BF16), jax.ShapeDtypeStruct((T_SAMPLE, HALF), BF16)],
        scratch_shapes=[pltpu.VMEM((8, TB + 2 * HALO, HALF), F32)],
        compiler_params=pltpu.CompilerParams(vmem_limit_bytes=56 * 1024 * 1024),
    )(m, f_all, a_all, conv_w, conv_b, ln_g, ln_b)


def _dft_sample_table():
    k = jnp.arange(SEQ_SAMPLE, dtype=jnp.int32)
    ang = (2.0 * math.pi / SEQ_SAMPLE) * ((k[:, None] * k[None, :]) % SEQ_SAMPLE).astype(F32)
    scale = SEQ_SAMPLE ** -0.5
    return jnp.concatenate([jnp.cos(ang) * scale, jnp.sin(ang) * scale], axis=0).astype(BF16)


def _dft_prompt_tables():
    n1 = FFT_N1_PROMPT
    n = n1 * FFT_N2
    k1 = jnp.arange(n1, dtype=jnp.int32)
    ang1 = (2.0 * math.pi / n1) * ((k1[:, None] * k1[None, :]) % n1).astype(F32)
    d1 = jnp.concatenate([jnp.cos(ang1), -jnp.sin(ang1)], axis=0).astype(BF16)
    k2 = jnp.arange(FFT_N2, dtype=jnp.int32)
    k = k1[:, None, None] + n1 * k2[None, :, None]
    ang2 = (2.0 * math.pi / n) * ((k * k2[None, None, :]) % n).astype(F32)
    cr = jnp.cos(ang2) * (n ** -0.5)
    sr = jnp.sin(ang2) * (n ** -0.5)
    top = jnp.concatenate([cr, sr], axis=2)
    bot = jnp.concatenate([sr, -cr], axis=2)
    l2 = jnp.concatenate([top, bot], axis=1).astype(BF16)
    return d1, l2


def _fold_body(m_ref, w_ref, o_ref):
    o_ref[...] = jnp.dot(m_ref[...], w_ref[...], preferred_element_type=F32,
                         precision=lax.Precision.HIGHEST).astype(BF16)


def _fold_channel_dft(w_out_f):
    c = jnp.arange(HALF, dtype=jnp.int32)
    same = (c[:, None] // GROUP_CH) == (c[None, :] // GROUP_CH)
    ang = (2.0 * math.pi / GROUP_CH) * ((c[:, None] * c[None, :]) % GROUP_CH).astype(F32)
    scale = GROUP_CH ** -0.5
    cbd = jnp.where(same, jnp.cos(ang), 0.0) * scale
    sbd = jnp.where(same, jnp.sin(ang), 0.0) * scale
    m = jnp.concatenate([cbd, -sbd], axis=0)
    return pl.pallas_call(
        _fold_body, name="fold_channel_dft",
        out_shape=jax.ShapeDtypeStruct((2 * HALF, D), BF16),
    )(m, w_out_f)


def _route_and_store(x_new, g_ref, wr_ref, br_ref, x_out_ref, row_ref, meta_ref):
    x_out_ref[...] = x_new
    h = _rms(x_new, g_ref[...])
    h_hi = h.astype(BF16)
    h_lo = (h - h_hi.astype(F32)).astype(BF16)
    p = _dot(h_hi, wr_ref[...])
    q = _dot(h_lo, wr_ref[...])
    lg = p[:, :META] + p[:, META:] + q[:, :META] + q[:, META:] + br_ref[...]
    lane = lax.broadcasted_iota(jnp.int32, lg.shape, 1).astype(F32)
    neg = -jnp.inf
    lc = jnp.where(lane < N_GROUPS, lg, neg)
    mc = jnp.max(lc, axis=-1, keepdims=True)
    grp = jnp.min(jnp.where(lc == mc, lane, META), axis=-1, keepdims=True)
    p_grp = 1.0 / jnp.sum(jnp.exp(lc - mc), axis=-1, keepdims=True)
    lo = N_GROUPS + EPG * grp
    lf = jnp.where((lane >= lo) & (lane < lo + EPG), lg, neg)
    m1 = jnp.max(lf, axis=-1, keepdims=True)
    i1 = jnp.min(jnp.where(lf == m1, lane, META), axis=-1, keepdims=True)
    lf2 = jnp.where(lane == i1, neg, lf)
    m2 = jnp.max(lf2, axis=-1, keepdims=True)
    i2 = jnp.min(jnp.where(lf2 == m2, lane, META), axis=-1, keepdims=True)
    e = jnp.exp(m2 - m1)
    gate1 = p_grp / (1.0 + e)
    gate2 = p_grp * e / (1.0 + e)
    j1 = i1 - lo
    j2 = i2 - lo
    ja = jnp.minimum(j1, j2)
    jb = jnp.maximum(j1, j2)
    bucket = grp * N_PAIRS + (ja * (2 * EPG - 1 - ja)) * 0.5 + (jb - ja - 1)
    gate_a = jnp.where(j1 < j2, gate1, gate2)
    gate_b = jnp.where(j1 < j2, gate2, gate1)
    meta = jnp.where(lane == 0, gate_a,
                     jnp.where(lane == 1, gate_b,
                               jnp.where(lane == 2, bucket, 0.0)))
    for k in range(ROW_TILE):
        row_ref[pl.ds(k, TB, stride=ROUTED), :] = h[:, k * LANES:(k + 1) * LANES]
    row_ref[pl.ds(ROW_TILE, TB, stride=ROUTED), :] = meta
    meta_ref[...] = meta


def _router_operands(w_coarse, b_coarse, w_fine, b_fine):
    w = jnp.concatenate([w_coarse, w_fine.reshape(D, N_GROUPS * EPG)], axis=1)
    w = jnp.pad(w, ((0, 0), (0, META - w.shape[1])))
    w_hi = w.astype(BF16)
    w_lo = (w - w_hi.astype(F32)).astype(BF16)
    b = jnp.concatenate([b_coarse, b_fine.reshape(-1)])
    b = jnp.pad(b, (0, META - b.shape[0])).reshape(1, META)
    return jnp.concatenate([w_hi, w_lo], axis=1), b


def _l0_out_body(prev_ref, cur_ref, next_ref, cw_ref, cb_ref, lg_ref, lb_ref, a2s_ref,
                 cp_ref, sp_ref, cs_ref, ss_ref, xp_ref, xs_ref, wa_ref, wc_ref, ws_ref,
                 g_ref, wr_ref, br_ref, x_out_ref, row_ref, meta_ref, buf, a2):
    i = pl.program_id(0)
    in_prompt = i < N_PROMPT_TILES
    x = jnp.where(in_prompt, xp_ref[...], xs_ref[...])
    c = jnp.where(in_prompt, cp_ref[...].astype(BF16), cs_ref[...])
    s = jnp.where(in_prompt, sp_ref[...].astype(BF16), ss_ref[...])
    mix = _dot(c, wc_ref[...]) + _dot(s, ws_ref[...])

    @pl.when(in_prompt)
    def _():
        top = jnp.where(i == 0, 0.0, prev_ref[...].astype(F32))
        bottom = jnp.where(i == N_PROMPT_TILES - 1, 0.0, next_ref[...].astype(F32))
        _conv_module(top, cur_ref[...], bottom, cw_ref, cb_ref, lg_ref, lb_ref, a2, 0, buf)

    @pl.when(jnp.logical_not(in_prompt))
    def _():
        a2[...] = a2s_ref[...]

    mix = mix + _dot(a2[...], wa_ref[...])
    _route_and_store(x + mix, g_ref, wr_ref, br_ref, x_out_ref, row_ref, meta_ref)


def _l0_out(a, conv_w, conv_b, ln_g, ln_b, a2_sample, cs_prompt, cs_sample, xp, xs, wa, wc, ws, g, wr, br):
    def sample_half(which):
        return pl.BlockSpec((None, TB, HALF), lambda i: (which, jnp.maximum(i - N_PROMPT_TILES, 0), 0))
    per_tile = TB // HALO
    last = N_PROMPT_TILES - 1
    prev = pl.BlockSpec((HALO, HALF), lambda i: (jnp.maximum(jnp.minimum(i, last) * per_tile - 1, 0), 0))
    cur = pl.BlockSpec((TB, HALF), lambda i: (jnp.minimum(i, last), 0))
    nxt = pl.BlockSpec((HALO, HALF), lambda i: ((jnp.minimum(i, last) + 1) * per_tile, 0))
    return pl.pallas_call(
        _l0_out_body, name="l0_out",
        grid=(N_TILES,),
        in_specs=[prev, cur, nxt, _full_spec((CONV_W + 1, HALF)),
                  _full_spec((1, HALF)), _full_spec((1, HALF)), _full_spec((1, HALF)), _sample_spec(HALF),
                  _prompt_spec(HALF), _prompt_spec(HALF), sample_half(0), sample_half(1),
                  _prompt_spec(), _sample_spec(),
                  _full_spec((HALF, D)), _full_spec((HALF, D)), _full_spec((HALF, D)),
                  _full_spec((1, D)), _full_spec((D, 2 * META)), _full_spec((1, META))],
        out_specs=[_tile_spec(D), _row_tiles_spec(ROUTED), _tile_spec(META)],
        out_shape=[jax.ShapeDtypeStruct((T_ALL, D), F32), jax.ShapeDtypeStruct((T_ALL * ROUTED, LANES), F32),
                   jax.ShapeDtypeStruct((T_ALL, META), F32)],
        scratch_shapes=[pltpu.VMEM((8, TB + 2 * HALO, HALF), F32), pltpu.VMEM((TB, HALF), BF16)],
    )(a, a, a, conv_w, conv_b, ln_g, ln_b, a2_sample, *cs_prompt, cs_sample, cs_sample, xp, xs, wa, wc, ws, g, wr, br)


def _moe_plan(meta):
    bucket = meta[:, 2].astype(jnp.int32)
    sorted_b, order = lax.sort((bucket, jnp.arange(T_ALL, dtype=jnp.int32)), num_keys=1, is_stable=True)
    edges = jnp.arange(N_BUCKETS + 1, dtype=jnp.int32)
    starts = jnp.sum(sorted_b[None, :] < edges[:, None], axis=1).astype(jnp.int32)
    counts = starts[1:] - starts[:-1]
    nblk = (counts + MOE_ROWS - 1) // MOE_ROWS
    cum = jnp.cumsum(nblk)
    first = cum - nblk
    j = jnp.arange(N_MOE_BLOCKS, dtype=jnp.int32)
    valid = j < cum[-1]
    j_eff = jnp.minimum(j, cum[-1] - 1)
    bj = jnp.minimum(jnp.sum(cum[None, :] <= j_eff[:, None], axis=1), N_BUCKETS - 1).astype(jnp.int32)
    off = (j_eff - first[bj]) * MOE_ROWS
    cnt = jnp.where(valid, jnp.clip(counts[bj] - off, 0, MOE_ROWS), 0).astype(jnp.int32)
    r = jnp.arange(MOE_ROWS, dtype=jnp.int32)[None, :]
    pos = jnp.clip(starts[bj][:, None] + off[:, None] + r, 0, T_ALL - 1)
    tok = order[pos]
    spare = T_ALL + (j % 2)[:, None] * MOE_ROWS + r
    src = (tok * ROUTED).reshape(N_MOE_BLOCKS, 1, MOE_ROWS)
    dst = (jnp.where(r < cnt[:, None], tok, spare) * ROW_TILE).reshape(N_MOE_BLOCKS, 1, MOE_ROWS)
    pa, pb = np.triu_indices(EPG, k=1)
    grp = bj // N_PAIRS
    ea = grp * EPG + jnp.asarray(pa, jnp.int32)[bj % N_PAIRS]
    eb = grp * EPG + jnp.asarray(pb, jnp.int32)[bj % N_PAIRS]
    return ea.astype(jnp.int32), eb.astype(jnp.int32), cnt, src, dst


def _moe_body(ea_ref, eb_ref, cnt_ref, src_ref, dst_ref, rows_hbm,
              wga_ref, wua_ref, wda_ref, wgb_ref, wub_ref, wdb_ref,
              y_hbm, xbuf, ybuf, gsem, ssem):
    s = pl.program_id(0)
    tile = ROW_TILE
    buf_rows = MOE_ROWS * tile

    def rows_in(b):
        inside = (b >= 0) & (b < N_MOE_BLOCKS)
        return jnp.where(inside, cnt_ref[jnp.clip(b, 0, N_MOE_BLOCKS - 1)], 0)

    def wait_fetch(p):
        pltpu.make_async_copy(rows_hbm.at[pl.ds(0, MOE_ROWS * ROUTED)], xbuf.at[p], gsem.at[p]).wait()

    def wait_send(p):
        pltpu.make_async_copy(ybuf.at[p], y_hbm.at[pl.ds(0, buf_rows)], ssem.at[p]).wait()

    @pl.when(s == 0)
    def _():
        ybuf[...] = jnp.zeros(ybuf.shape, ybuf.dtype)
        for p in range(2):
            clear = pltpu.make_async_copy(ybuf.at[p], y_hbm.at[pl.ds((T_ALL + p * MOE_ROWS) * tile, buf_rows)],
                                          ssem.at[p])
            clear.start()
            clear.wait()

    for p in range(2):
        q = 1 - p
        mine = (s % 2) == p

        @pl.when(mine & (rows_in(s - 3) > 0))
        def _():
            wait_send(q)

        @pl.when(mine & (rows_in(s) > 0))
        def _():
            for r in range(MOE_ROWS):
                pltpu.make_async_copy(rows_hbm.at[pl.ds(src_ref[0, 0, r], ROUTED)],
                                      xbuf.at[p, pl.ds(r * ROUTED, ROUTED)], gsem.at[p]).start(priority=r % 2)

        @pl.when(mine & (rows_in(s - 2) > 0))
        def _():
            for r in range(MOE_ROWS):
                start = pl.multiple_of(dst_ref[0, 0, r], tile)
                pltpu.make_async_copy(ybuf.at[p, pl.ds(r * tile, tile)], y_hbm.at[pl.ds(start, tile)],
                                      ssem.at[p]).start(priority=r % 2)

        @pl.when(mine & (rows_in(s - 1) > 0))
        def _():
            wait_fetch(q)
            x = jnp.concatenate([xbuf[q, pl.ds(k, MOE_ROWS, stride=ROUTED), :] for k in range(tile)],
                                axis=1).astype(BF16)
            record = xbuf[q, pl.ds(tile, MOE_ROWS, stride=ROUTED), :]
            gate_a = record[:, 0:1]
            gate_b = record[:, 1:2]

            def hidden(wg_ref, wu_ref, gate):
                g = _dot(x, wg_ref[0])
                u = _dot(x, wu_ref[0])
                return (g * jax.nn.sigmoid(g) * u * gate).astype(BF16)

            y = (_dot(hidden(wga_ref, wua_ref, gate_a), wda_ref[0])
                 + _dot(hidden(wgb_ref, wub_ref, gate_b), wdb_ref[0]))
            for k in range(tile):
                ybuf[q, pl.ds(k, MOE_ROWS, stride=tile), :] = y[:, k * LANES:(k + 1) * LANES]


def _moe(rows, plan, w_gate, w_up, w_down):
    ea, eb, cnt, src, dst = plan
    last = N_MOE_BLOCKS - 1

    def block_of(step, lag):
        return jnp.clip(step - lag, 0, last)

    src_spec = pl.BlockSpec((1, 1, MOE_ROWS), lambda s, *_: (block_of(s, 0), 0, 0), memory_space=pltpu.SMEM)
    dst_spec = pl.BlockSpec((1, 1, MOE_ROWS), lambda s, *_: (block_of(s, 2), 0, 0), memory_space=pltpu.SMEM)
    up_a = pl.BlockSpec((1, D, D_EXPERT), lambda s, ea, eb, cnt: (ea[block_of(s, 1)], 0, 0))
    up_b = pl.BlockSpec((1, D, D_EXPERT), lambda s, ea, eb, cnt: (eb[block_of(s, 1)], 0, 0))
    down_a = pl.BlockSpec((1, D_EXPERT, D), lambda s, ea, eb, cnt: (ea[block_of(s, 1)], 0, 0))
    down_b = pl.BlockSpec((1, D_EXPERT, D), lambda s, ea, eb, cnt: (eb[block_of(s, 1)], 0, 0))
    grid_spec = pltpu.PrefetchScalarGridSpec(
        num_scalar_prefetch=3,
        grid=(N_MOE_BLOCKS + 2,),
        in_specs=[src_spec, dst_spec, pl.BlockSpec(memory_space=pl.ANY),
                  up_a, up_a, down_a, up_b, up_b, down_b],
        out_specs=pl.BlockSpec(memory_space=pl.ANY),
        scratch_shapes=[pltpu.VMEM((2, MOE_ROWS * ROUTED, LANES), F32),
                        pltpu.VMEM((2, MOE_ROWS * ROW_TILE, LANES), F32),
                        pltpu.SemaphoreType.DMA((2,)), pltpu.SemaphoreType.DMA((2,))],
    )
    return pl.pallas_call(
        _moe_body, name="moe",
        grid_spec=grid_spec,
        out_shape=jax.ShapeDtypeStruct(((T_ALL + 2 * MOE_ROWS) * ROW_TILE, LANES), F32),
        compiler_params=pltpu.CompilerParams(dimension_semantics=("arbitrary",)),
    )(ea, eb, cnt, src, dst, rows, w_gate, w_up, w_down, w_gate, w_up, w_down)


def _l1_in_body(x_ref, y_ref, g_ref, w_ref, x_out_ref, gate_ref, cv_ref):
    x = x_ref[...] + _read_row_tiles(y_ref)
    x_out_ref[...] = x
    u = _dot(_rms(x, g_ref[...]).astype(BF16), w_ref[...])
    gate_ref[...] = u[:, :D].astype(BF16)
    cv_ref[...] = (u[:, D:2 * D] * u[:, 2 * D:]).astype(BF16)


def _l1_in(x, y, g, w_in):
    return pl.pallas_call(
        _l1_in_body, name="l1_in",
        grid=(N_TILES,),
        in_specs=[_tile_spec(D), _row_tiles_spec(), _full_spec((1, D)), _full_spec((D, 3 * D))],
        out_specs=[_tile_spec(D), _tile_spec(D), _tile_spec(D)],
        out_shape=[jax.ShapeDtypeStruct((T_ALL, D), F32), jax.ShapeDtypeStruct((T_ALL, D), BF16),
                   jax.ShapeDtypeStruct((T_ALL, D), BF16)],
    )(x, y, g, w_in)


def _l1_out_body(prev_ref, cur_ref, next_ref, gate_ref, x_ref, cw_ref, wo_ref, g_ref, wr_ref, br_ref,
                 x_out_ref, row_ref, meta_ref, buf, ybuf):
    starts, ends = _seq_edges(pl.program_id(0))
    buf[0:HALO, :] = jnp.where(starts, 0.0, prev_ref[...].astype(F32))
    buf[HALO:HALO + TB, :] = cur_ref[...].astype(F32)
    buf[HALO + TB:, :] = jnp.where(ends, 0.0, next_ref[...].astype(F32))
    rows = CONV_ROWS // 2
    for r0 in range(0, TB, rows):
        conv = (cw_ref[0:1, :] * buf[r0 + HALO - 1:r0 + HALO - 1 + rows, :]
                + cw_ref[1:2, :] * buf[r0 + HALO:r0 + HALO + rows, :]
                + cw_ref[2:3, :] * buf[r0 + HALO + 1:r0 + HALO + 1 + rows, :])
        ybuf[r0:r0 + rows, :] = (gate_ref[r0:r0 + rows, :].astype(F32) * conv).astype(BF16)
    _route_and_store(x_ref[...] + _dot(ybuf[...], wo_ref[...]), g_ref, wr_ref, br_ref, x_out_ref, row_ref, meta_ref)


def _l1_out(cv, gate, x, conv_w, w_out, g, wr, br):
    prev, nxt = _halo_specs(D)
    return pl.pallas_call(
        _l1_out_body, name="l1_out",
        grid=(N_TILES,),
        in_specs=[prev, _tile_spec(D), nxt, _tile_spec(D), _tile_spec(D), _full_spec((8, D)),
                  _full_spec((D, D)), _full_spec((1, D)), _full_spec((D, 2 * META)), _full_spec((1, META))],
        out_specs=[_tile_spec(D), _row_tiles_spec(ROUTED), _tile_spec(META)],
        out_shape=[jax.ShapeDtypeStruct((T_ALL, D), F32), jax.ShapeDtypeStruct((T_ALL * ROUTED, LANES), F32),
                   jax.ShapeDtypeStruct((T_ALL, META), F32)],
        scratch_shapes=[pltpu.VMEM((TB + 2 * HALO, D), F32), pltpu.VMEM((TB, D), BF16)],
    )(cv, cv, cv, gate, x, conv_w, w_out, g, wr, br)


def _final_body(x_ref, y_ref, g_ref, o_ref):
    o_ref[...] = _rms(x_ref[...] + _read_row_tiles(y_ref), g_ref[...])


def _final(x, y, g, *, tile0, tiles):
    in_tile = pl.BlockSpec((TB, D), lambda i: (i + tile0, 0))
    y_tile = pl.BlockSpec((TB * ROW_TILE, LANES), lambda i: (i + tile0, 0))
    return pl.pallas_call(
        _final_body, name="final_norm",
        grid=(tiles,),
        in_specs=[in_tile, y_tile, _full_spec((1, D))],
        out_specs=_tile_spec(D),
        out_shape=jax.ShapeDtypeStruct((tiles * TB, D), F32),
    )(x, y, g)


def kernel(x_prompt, x_sample, l0_norm_mix, l0_w_in, l0_conv_w, l0_conv_b, l0_ln_g, l0_ln_b, l0_w_out, l0_norm_ffn, l0_w_coarse, l0_b_coarse, l0_w_fine, l0_b_fine, l0_w_gate, l0_w_up, l0_w_down, l1_norm_mix, l1_w_in, l1_conv_w, l1_w_out, l1_norm_ffn, l1_w_coarse, l1_b_coarse, l1_w_fine, l1_b_fine, l1_w_gate, l1_w_up, l1_w_down, final_norm):
    xp = x_prompt.reshape(T_PROMPT, D)
    xs = x_sample.reshape(T_SAMPLE, D)
    row = lambda v: v.reshape(1, -1)

    a, f = _l0_in(xp, xs, row(l0_norm_mix), l0_w_in.astype(BF16))
    conv_w = jnp.pad(l0_conv_w, ((0, 1), (0, 0)))
    d1, l2 = _dft_prompt_tables()
    cs_prompt = _dft2_prompt(_dft1_prompt(f, d1), l2)
    conv_params = (conv_w, row(l0_conv_b), row(l0_ln_g), row(l0_ln_b))
    cs_sample, a2_sample = _sample_mixers(f, a, _dft_sample_table(), *conv_params)
    w_cs = _fold_channel_dft(l0_w_out[HALF:])
    wr0, br0 = _router_operands(l0_w_coarse, l0_b_coarse, l0_w_fine, l0_b_fine)
    x1, rows0, meta0 = _l0_out(a, *conv_params, a2_sample, cs_prompt, cs_sample, xp, xs,
                               l0_w_out[:HALF].astype(BF16), w_cs[:HALF], w_cs[HALF:], row(l0_norm_ffn), wr0, br0)

    y0 = _moe(rows0, _moe_plan(meta0), l0_w_gate.astype(BF16), l0_w_up.astype(BF16), l0_w_down.astype(BF16))

    x2, gate, cv = _l1_in(x1, y0, row(l1_norm_mix), l1_w_in.astype(BF16))
    wr1, br1 = _router_operands(l1_w_coarse, l1_b_coarse, l1_w_fine, l1_b_fine)
    x3, rows1, meta1 = _l1_out(cv, gate, x2, jnp.pad(l1_conv_w, ((0, 5), (0, 0))), l1_w_out.astype(BF16),
                               row(l1_norm_ffn), wr1, br1)

    y1 = _moe(rows1, _moe_plan(meta1), l1_w_gate.astype(BF16), l1_w_up.astype(BF16), l1_w_down.astype(BF16))

    out_p = _final(x3, y1, row(final_norm), tile0=0, tiles=N_PROMPT_TILES)
    out_s = _final(x3, y1, row(final_norm), tile0=N_PROMPT_TILES, tiles=N_TILES - N_PROMPT_TILES)
    return out_p.reshape(x_prompt.shape), out_s.reshape(x_sample.shape)
```

```python
import math

import jax
import jax.numpy as jnp
import numpy as np
from jax import lax
from jax.experimental import pallas as pl
from jax.experimental.pallas import tpu as pltpu

D = 1024
LANES = 128
T_PROMPT = 16384
SEQ_SAMPLE = 2048
T_SAMPLE = 32 * SEQ_SAMPLE
T_ALL = T_PROMPT + T_SAMPLE
HALF = 512
GROUP_CH = 64
CONV_W = 31
N_GROUPS = 4
EPG = 8
N_PAIRS = EPG * (EPG - 1) // 2
N_BUCKETS = N_GROUPS * N_PAIRS
D_EXPERT = 512
RMS_EPS = 1e-6
LN_EPS = 1e-5

TB = 512
N_TILES = T_ALL // TB
N_PROMPT_TILES = T_PROMPT // TB
HALO = 16
EDGE = 8
CONV_ROWS = 64
MOE_ROWS = 256
N_MOE_BLOCKS = T_ALL // MOE_ROWS + N_BUCKETS + 1
META = 128
ROW_TILE = D // LANES
ROUTED = ROW_TILE + 1
FFT_N2 = 128
FFT_N1_PROMPT = T_PROMPT // FFT_N2
FFT_G1 = 16
FFT_KB = 8
DFT_ROWS = 1024

BF16 = jnp.bfloat16
F32 = jnp.float32


def _rms(x, g):
    return x * lax.rsqrt(jnp.mean(x * x, axis=-1, keepdims=True) + RMS_EPS) * g


def _dot(a, b):
    return jnp.dot(a, b, preferred_element_type=F32)


def _seq_edges(i):
    r0 = i * TB
    r1 = r0 + TB
    in_prompt = r0 < T_PROMPT
    starts = jnp.where(in_prompt, r0 == 0, (r0 - T_PROMPT) % SEQ_SAMPLE == 0)
    ends = jnp.where(in_prompt, r1 == T_PROMPT, (r1 - T_PROMPT) % SEQ_SAMPLE == 0)
    return starts, ends


def _tile_spec(width):
    return pl.BlockSpec((TB, width), lambda i: (i, 0))


def _row_tiles_spec(rows_per_token=ROW_TILE):
    return pl.BlockSpec((TB * rows_per_token, LANES), lambda i: (i, 0))


def _read_row_tiles(ref, tokens):
    return jnp.concatenate([ref[pl.ds(k, tokens, stride=ROW_TILE), :] for k in range(ROW_TILE)], axis=1)


def _full_spec(shape):
    return pl.BlockSpec(shape, lambda *_: (0,) * len(shape))


def _prompt_spec(width=D):
    return pl.BlockSpec((TB, width), lambda i: (jnp.minimum(i, N_PROMPT_TILES - 1), 0))


def _sample_spec(width=D):
    return pl.BlockSpec((TB, width), lambda i: (jnp.maximum(i - N_PROMPT_TILES, 0), 0))


def _halo_specs(width):
    per_tile = TB // HALO
    last = T_ALL // HALO - 1
    prev = pl.BlockSpec((HALO, width), lambda i: (jnp.maximum(i * per_tile - 1, 0), 0))
    nxt = pl.BlockSpec((HALO, width), lambda i: (jnp.minimum((i + 1) * per_tile, last), 0))
    return prev, nxt


def _l0_in_body(xp_ref, xs_ref, g_ref, w_ref, a_ref, f_ref):
    i = pl.program_id(0)
    x = jnp.where(i < N_PROMPT_TILES, xp_ref[...], xs_ref[...])
    h = _rms(x, g_ref[...])
    u = _dot(h.astype(BF16), w_ref[...])
    a_ref[...] = u[:, :HALF] * jax.nn.sigmoid(u[:, HALF:2 * HALF])
    f_ref[...] = u[:, 2 * HALF:].astype(BF16)


def _l0_in(xp, xs, g, w_in):
    return pl.pallas_call(
        _l0_in_body, name="l0_in",
        grid=(N_TILES,),
        in_specs=[_prompt_spec(), _sample_spec(), _full_spec((1, D)), _full_spec((D, 3 * HALF))],
        out_specs=[_tile_spec(HALF), _tile_spec(HALF)],
        out_shape=[jax.ShapeDtypeStruct((T_ALL, HALF), F32), jax.ShapeDtypeStruct((T_ALL, HALF), BF16)],
    )(xp, xs, g, w_in)


def _conv_module(prev_ref, cur_ref, next_ref, w_ref, b_ref, g_ref, beta_ref, o_ref, buf):
    starts, ends = _seq_edges(pl.program_id(0))
    buf[0, 0:HALO, :] = jnp.where(starts, 0.0, prev_ref[...])
    buf[0, HALO:HALO + TB, :] = cur_ref[...]
    buf[0, HALO + TB:, :] = jnp.where(ends, 0.0, next_ref[...])
    rows = TB + 2 * HALO - 8
    for s in range(1, 8):
        for c0 in range(0, rows, 128):
            c1 = min(c0 + 128, rows)
            buf[s, c0:c1, :] = buf[0, c0 + s:c1 + s, :]
    first = HALO - CONV_W // 2
    for r0 in range(0, TB, CONV_ROWS):
        acc = jnp.broadcast_to(b_ref[...], (CONV_ROWS, HALF))
        for k in range(CONV_W):
            s = (first + k) % 8
            base = r0 + first + k - s
            acc = acc + w_ref[k:k + 1, :] * buf[s, base:base + CONV_ROWS, :]
        mu = jnp.mean(acc, axis=-1, keepdims=True)
        xc = acc - mu
        var = jnp.mean(xc * xc, axis=-1, keepdims=True)
        y = xc * lax.rsqrt(var + LN_EPS) * g_ref[...] + beta_ref[...]
        o_ref[r0:r0 + CONV_ROWS, :] = (y * jax.nn.sigmoid(y)).astype(BF16)


def _dft1_prompt_body(d_ref, x_ref, y_ref, xs, ys):
    n1 = FFT_N1_PROMPT
    x = x_ref[0].reshape(n1 * FFT_G1, HALF).astype(F32)
    for c in range(HALF // LANES):
        xs[c] = x[:, c * LANES:(c + 1) * LANES]
    for q in range(FFT_G1):
        rows = pl.ds(q, n1, stride=FFT_G1)
        xq = jnp.concatenate([xs[c, rows, :] for c in range(HALF // LANES)], axis=1)
        y = _dot(d_ref[...], xq.astype(BF16))
        for c in range(HALF // LANES):
            ys[0, c, rows, :] = y[:n1, c * LANES:(c + 1) * LANES]
            ys[1, c, rows, :] = y[n1:, c * LANES:(c + 1) * LANES]
    for p in range(2):
        for c in range(HALF // LANES):
            y_ref[p, :, :, c * LANES:(c + 1) * LANES] = ys[p, c].reshape(n1, FFT_G1, LANES)


def _dft1_prompt(f_all, d1):
    n1 = FFT_N1_PROMPT
    x = f_all.reshape(T_ALL // T_PROMPT, n1, FFT_N2, HALF)
    return pl.pallas_call(
        _dft1_prompt_body, name="dft1_prompt",
        grid=(FFT_N2 // FFT_G1,),
        in_specs=[_full_spec((2 * n1, n1)),
                  pl.BlockSpec((1, n1, FFT_G1, HALF), lambda j: (0, 0, j, 0))],
        out_specs=pl.BlockSpec((2, n1, FFT_G1, HALF), lambda j: (0, 0, j, 0)),
        out_shape=jax.ShapeDtypeStruct((2, n1, FFT_N2, HALF), F32),
        scratch_shapes=[pltpu.VMEM((HALF // LANES, n1 * FFT_G1, LANES), F32),
                        pltpu.VMEM((2, HALF // LANES, n1 * FFT_G1, LANES), F32)],
        compiler_params=pltpu.CompilerParams(vmem_limit_bytes=48 * 1024 * 1024),
    )(d1, x)


def _dft2_prompt_body(l_ref, y_ref, c_ref, s_ref, cs, ss):
    for q in range(FFT_KB):
        yq = y_ref[:, q].reshape(2 * FFT_N2, HALF).astype(BF16)
        r = _dot(l_ref[q], yq)
        rows = pl.ds(q, FFT_N2, stride=FFT_KB)
        for c in range(HALF // LANES):
            cs[c, rows, :] = r[:FFT_N2, c * LANES:(c + 1) * LANES]
            ss[c, rows, :] = r[FFT_N2:, c * LANES:(c + 1) * LANES]
    for c in range(HALF // LANES):
        c_ref[:, :, c * LANES:(c + 1) * LANES] = cs[c].reshape(FFT_N2, FFT_KB, LANES)
        s_ref[:, :, c * LANES:(c + 1) * LANES] = ss[c].reshape(FFT_N2, FFT_KB, LANES)


def _dft2_prompt(y, l2):
    n1 = FFT_N1_PROMPT
    out_sds = jax.ShapeDtypeStruct((FFT_N2, n1, HALF), F32)
    out_spec = pl.BlockSpec((FFT_N2, FFT_KB, HALF), lambda k: (0, k, 0))
    c, s = pl.pallas_call(
        _dft2_prompt_body, name="dft2_prompt",
        grid=(n1 // FFT_KB,),
        in_specs=[pl.BlockSpec((FFT_KB, 2 * FFT_N2, 2 * FFT_N2), lambda k: (k, 0, 0)),
                  pl.BlockSpec((2, FFT_KB, FFT_N2, HALF), lambda k: (0, k, 0, 0))],
        out_specs=[out_spec, out_spec],
        out_shape=[out_sds, out_sds],
        scratch_shapes=[pltpu.VMEM((HALF // LANES, FFT_N2 * FFT_KB, LANES), F32),
                        pltpu.VMEM((HALF // LANES, FFT_N2 * FFT_KB, LANES), F32)],
    )(l2, y)
    return c.reshape(T_PROMPT, HALF), s.reshape(T_PROMPT, HALF)


def _dft_sample_body(m_ref, x_ref, o_ref):
    for part in range(2):
        for r0 in range(0, SEQ_SAMPLE, DFT_ROWS):
            rows = m_ref[part * SEQ_SAMPLE + r0:part * SEQ_SAMPLE + r0 + DFT_ROWS, :]
            o_ref[part, r0:r0 + DFT_ROWS, :] = _dot(rows, x_ref[...]).astype(BF16)


def _dft_sample(f_all, m):
    first = T_PROMPT // SEQ_SAMPLE
    return pl.pallas_call(
        _dft_sample_body, name="dft_sample",
        grid=(T_SAMPLE // SEQ_SAMPLE,),
        in_specs=[pl.BlockSpec((2 * SEQ_SAMPLE, SEQ_SAMPLE), lambda b: (0, 0), pipeline_mode=pl.Buffered(1)),
                  pl.BlockSpec((SEQ_SAMPLE, HALF), lambda b: (first + b, 0))],
        out_specs=pl.BlockSpec((2, SEQ_SAMPLE, HALF), lambda b: (0, b, 0)),
        out_shape=jax.ShapeDtypeStruct((2, T_SAMPLE, HALF), BF16),
        compiler_params=pltpu.CompilerParams(vmem_limit_bytes=48 * 1024 * 1024),
    )(m, f_all)


def _dft_sample_table():
    k = jnp.arange(SEQ_SAMPLE, dtype=jnp.int32)
    ang = (2.0 * math.pi / SEQ_SAMPLE) * ((k[:, None] * k[None, :]) % SEQ_SAMPLE).astype(F32)
    scale = SEQ_SAMPLE ** -0.5
    return jnp.concatenate([jnp.cos(ang) * scale, jnp.sin(ang) * scale], axis=0).astype(BF16)


def _dft_prompt_tables():
    n1 = FFT_N1_PROMPT
    n = n1 * FFT_N2
    k1 = jnp.arange(n1, dtype=jnp.int32)
    ang1 = (2.0 * math.pi / n1) * ((k1[:, None] * k1[None, :]) % n1).astype(F32)
    d1 = jnp.concatenate([jnp.cos(ang1), -jnp.sin(ang1)], axis=0).astype(BF16)
    k2 = jnp.arange(FFT_N2, dtype=jnp.int32)
    k = k1[:, None, None] + n1 * k2[None, :, None]
    ang2 = (2.0 * math.pi / n) * ((k * k2[None, None, :]) % n).astype(F32)
    cr = jnp.cos(ang2) * (n ** -0.5)
    sr = jnp.sin(ang2) * (n ** -0.5)
    top = jnp.concatenate([cr, sr], axis=2)
    bot = jnp.concatenate([sr, -cr], axis=2)
    l2 = jnp.concatenate([top, bot], axis=1).astype(BF16)
    return d1, l2


def _fold_body(m_ref, w_ref, o_ref):
    o_ref[...] = jnp.dot(m_ref[...], w_ref[...], preferred_element_type=F32,
                         precision=lax.Precision.HIGHEST).astype(BF16)


def _fold_channel_dft(w_out_f):
    c = jnp.arange(HALF, dtype=jnp.int32)
    same = (c[:, None] // GROUP_CH) == (c[None, :] // GROUP_CH)
    ang = (2.0 * math.pi / GROUP_CH) * ((c[:, None] * c[None, :]) % GROUP_CH).astype(F32)
    scale = GROUP_CH ** -0.5
    cbd = jnp.where(same, jnp.cos(ang), 0.0) * scale
    sbd = jnp.where(same, jnp.sin(ang), 0.0) * scale
    m = jnp.concatenate([cbd, -sbd], axis=0)
    return pl.pallas_call(
        _fold_body, name="fold_channel_dft",
        out_shape=jax.ShapeDtypeStruct((2 * HALF, D), BF16),
    )(m, w_out_f)


def _route_and_store(x_new, g_ref, wr_ref, br_ref, x_out_ref, row_ref, meta_ref):
    x_out_ref[...] = x_new
    h = _rms(x_new, g_ref[...])
    h_hi = h.astype(BF16)
    h_lo = (h - h_hi.astype(F32)).astype(BF16)
    p = _dot(h_hi, wr_ref[...])
    q = _dot(h_lo, wr_ref[...])
    lg = p[:, :META] + p[:, META:] + q[:, :META] + q[:, META:] + br_ref[...]
    lane = lax.broadcasted_iota(jnp.int32, lg.shape, 1).astype(F32)
    neg = -jnp.inf
    lc = jnp.where(lane < N_GROUPS, lg, neg)
    mc = jnp.max(lc, axis=-1, keepdims=True)
    grp = jnp.min(jnp.where(lc == mc, lane, META), axis=-1, keepdims=True)
    p_grp = 1.0 / jnp.sum(jnp.exp(lc - mc), axis=-1, keepdims=True)
    lo = N_GROUPS + EPG * grp
    lf = jnp.where((lane >= lo) & (lane < lo + EPG), lg, neg)
    m1 = jnp.max(lf, axis=-1, keepdims=True)
    i1 = jnp.min(jnp.where(lf == m1, lane, META), axis=-1, keepdims=True)
    lf2 = jnp.where(lane == i1, neg, lf)
    m2 = jnp.max(lf2, axis=-1, keepdims=True)
    i2 = jnp.min(jnp.where(lf2 == m2, lane, META), axis=-1, keepdims=True)
    e = jnp.exp(m2 - m1)
    gate1 = p_grp / (1.0 + e)
    gate2 = p_grp * e / (1.0 + e)
    j1 = i1 - lo
    j2 = i2 - lo
    ja = jnp.minimum(j1, j2)
    jb = jnp.maximum(j1, j2)
    bucket = grp * N_PAIRS + (ja * (2 * EPG - 1 - ja)) * 0.5 + (jb - ja - 1)
    gate_a = jnp.where(j1 < j2, gate1, gate2)
    gate_b = jnp.where(j1 < j2, gate2, gate1)
    meta = jnp.where(lane == 0, gate_a,
                     jnp.where(lane == 1, gate_b,
                               jnp.where(lane == 2, bucket, 0.0)))
    for k in range(ROW_TILE):
        row_ref[pl.ds(k, TB, stride=ROUTED), :] = h[:, k * LANES:(k + 1) * LANES]
    row_ref[pl.ds(ROW_TILE, TB, stride=ROUTED), :] = meta
    meta_ref[...] = meta


def _router_operands(w_coarse, b_coarse, w_fine, b_fine):
    w = jnp.concatenate([w_coarse, w_fine.reshape(D, N_GROUPS * EPG)], axis=1)
    w = jnp.pad(w, ((0, 0), (0, META - w.shape[1])))
    w_hi = w.astype(BF16)
    w_lo = (w - w_hi.astype(F32)).astype(BF16)
    b = jnp.concatenate([b_coarse, b_fine.reshape(-1)])
    b = jnp.pad(b, (0, META - b.shape[0])).reshape(1, META)
    return jnp.concatenate([w_hi, w_lo], axis=1), b


def _l0_out_body(prev_ref, cur_ref, next_ref, cw_ref, cb_ref, lg_ref, lb_ref,
                 cp_ref, sp_ref, cs_ref, ss_ref, xp_ref, xs_ref, wa_ref, wc_ref, ws_ref,
                 g_ref, wr_ref, br_ref, x_out_ref, row_ref, meta_ref, buf, a2):
    in_prompt = pl.program_id(0) < N_PROMPT_TILES
    x = jnp.where(in_prompt, xp_ref[...], xs_ref[...])
    c = jnp.where(in_prompt, cp_ref[...].astype(BF16), cs_ref[...])
    s = jnp.where(in_prompt, sp_ref[...].astype(BF16), ss_ref[...])
    mix = _dot(c, wc_ref[...]) + _dot(s, ws_ref[...])
    _conv_module(prev_ref, cur_ref, next_ref, cw_ref, cb_ref, lg_ref, lb_ref, a2, buf)
    mix = mix + _dot(a2[...], wa_ref[...])
    _route_and_store(x + mix, g_ref, wr_ref, br_ref, x_out_ref, row_ref, meta_ref)


def _l0_out(a, conv_w, conv_b, ln_g, ln_b, cs_prompt, cs_sample, xp, xs, wa, wc, ws, g, wr, br):
    def sample_half(which):
        return pl.BlockSpec((None, TB, HALF), lambda i: (which, jnp.maximum(i - N_PROMPT_TILES, 0), 0))
    prev, nxt = _halo_specs(HALF)
    return pl.pallas_call(
        _l0_out_body, name="l0_out",
        grid=(N_TILES,),
        in_specs=[prev, _tile_spec(HALF), nxt, _full_spec((CONV_W + 1, HALF)),
                  _full_spec((1, HALF)), _full_spec((1, HALF)), _full_spec((1, HALF)),
                  _prompt_spec(HALF), _prompt_spec(HALF), sample_half(0), sample_half(1),
                  _prompt_spec(), _sample_spec(),
                  _full_spec((HALF, D)), _full_spec((HALF, D)), _full_spec((HALF, D)),
                  _full_spec((1, D)), _full_spec((D, 2 * META)), _full_spec((1, META))],
        out_specs=[_tile_spec(D), _row_tiles_spec(ROUTED), _tile_spec(META)],
        out_shape=[jax.ShapeDtypeStruct((T_ALL, D), F32), jax.ShapeDtypeStruct((T_ALL * ROUTED, LANES), F32),
                   jax.ShapeDtypeStruct((T_ALL, META), F32)],
        scratch_shapes=[pltpu.VMEM((8, TB + 2 * HALO, HALF), F32), pltpu.VMEM((TB, HALF), BF16)],
    )(a, a, a, conv_w, conv_b, ln_g, ln_b, *cs_prompt, cs_sample, cs_sample, xp, xs, wa, wc, ws, g, wr, br)


def _moe_plan(meta):
    bucket = meta[:, 2].astype(jnp.int32)
    sorted_b, order = lax.sort((bucket, jnp.arange(T_ALL, dtype=jnp.int32)), num_keys=1, is_stable=True)
    edges = jnp.arange(N_BUCKETS + 1, dtype=jnp.int32)
    starts = jnp.sum(sorted_b[None, :] < edges[:, None], axis=1).astype(jnp.int32)
    counts = starts[1:] - starts[:-1]
    nblk = (counts + MOE_ROWS - 1) // MOE_ROWS
    cum = jnp.cumsum(nblk)
    first = cum - nblk
    j = jnp.arange(N_MOE_BLOCKS, dtype=jnp.int32)
    valid = j < cum[-1]
    j_eff = jnp.minimum(j, cum[-1] - 1)
    bj = jnp.minimum(jnp.sum(cum[None, :] <= j_eff[:, None], axis=1), N_BUCKETS - 1).astype(jnp.int32)
    off = (j_eff - first[bj]) * MOE_ROWS
    cnt = jnp.where(valid, jnp.clip(counts[bj] - off, 0, MOE_ROWS), 0).astype(jnp.int32)
    r = jnp.arange(MOE_ROWS, dtype=jnp.int32)[None, :]
    pos = jnp.clip(starts[bj][:, None] + off[:, None] + r, 0, T_ALL - 1)
    tok = order[pos]
    spare = T_ALL + (j % 2)[:, None] * MOE_ROWS + r
    src = (tok * ROUTED).reshape(N_MOE_BLOCKS, 1, MOE_ROWS)
    dst = (jnp.where(r < cnt[:, None], tok, spare) * ROW_TILE).reshape(N_MOE_BLOCKS, 1, MOE_ROWS)
    pa, pb = np.triu_indices(EPG, k=1)
    grp = bj // N_PAIRS
    ea = grp * EPG + jnp.asarray(pa, jnp.int32)[bj % N_PAIRS]
    eb = grp * EPG + jnp.asarray(pb, jnp.int32)[bj % N_PAIRS]
    return ea.astype(jnp.int32), eb.astype(jnp.int32), cnt, src, dst


def _moe_body(ea_ref, eb_ref, cnt_ref, src_ref, dst_ref, rows_hbm,
              wga_ref, wua_ref, wda_ref, wgb_ref, wub_ref, wdb_ref,
              y_hbm, xbuf, ybuf, gsem, ssem):
    s = pl.program_id(0)
    tile = ROW_TILE
    buf_rows = MOE_ROWS * tile

    def rows_in(b):
        inside = (b >= 0) & (b < N_MOE_BLOCKS)
        return jnp.where(inside, cnt_ref[jnp.clip(b, 0, N_MOE_BLOCKS - 1)], 0)

    def wait_fetch(p):
        pltpu.make_async_copy(rows_hbm.at[pl.ds(0, MOE_ROWS * ROUTED)], xbuf.at[p], gsem.at[p]).wait()

    def wait_send(p):
        pltpu.make_async_copy(ybuf.at[p], y_hbm.at[pl.ds(0, buf_rows)], ssem.at[p]).wait()

    @pl.when(s == 0)
    def _():
        ybuf[...] = jnp.zeros(ybuf.shape, ybuf.dtype)
        for p in range(2):
            clear = pltpu.make_async_copy(ybuf.at[p], y_hbm.at[pl.ds((T_ALL + p * MOE_ROWS) * tile, buf_rows)],
                                          ssem.at[p])
            clear.start()
            clear.wait()

    for p in range(2):
        q = 1 - p
        mine = (s % 2) == p

        @pl.when(mine & (rows_in(s - 3) > 0))
        def _():
            wait_send(q)

        @pl.when(mine & (rows_in(s) > 0))
        def _():
            for r in range(MOE_ROWS):
                pltpu.make_async_copy(rows_hbm.at[pl.ds(src_ref[0, 0, r], ROUTED)],
                                      xbuf.at[p, pl.ds(r * ROUTED, ROUTED)], gsem.at[p]).start(priority=r % 2)

        @pl.when(mine & (rows_in(s - 2) > 0))
        def _():
            for r in range(MOE_ROWS):
                start = pl.multiple_of(dst_ref[0, 0, r], tile)
                pltpu.make_async_copy(ybuf.at[p, pl.ds(r * tile, tile)], y_hbm.at[pl.ds(start, tile)],
                                      ssem.at[p]).start(priority=r % 2)

        @pl.when(mine & (rows_in(s - 1) > 0))
        def _():
            wait_fetch(q)
            x = jnp.concatenate([xbuf[q, pl.ds(k, MOE_ROWS, stride=ROUTED), :] for k in range(tile)],
                                axis=1).astype(BF16)
            record = xbuf[q, pl.ds(tile, MOE_ROWS, stride=ROUTED), :]
            gate_a = record[:, 0:1]
            gate_b = record[:, 1:2]

            def hidden(wg_ref, wu_ref, gate):
                g = _dot(x, wg_ref[0])
                u = _dot(x, wu_ref[0])
                return (g * jax.nn.sigmoid(g) * u * gate).astype(BF16)

            y = (_dot(hidden(wga_ref, wua_ref, gate_a), wda_ref[0])
                 + _dot(hidden(wgb_ref, wub_ref, gate_b), wdb_ref[0]))
            for k in range(tile):
                ybuf[q, pl.ds(k, MOE_ROWS, stride=tile), :] = y[:, k * LANES:(k + 1) * LANES]


def _moe(rows, plan, w_gate, w_up, w_down):
    ea, eb, cnt, src, dst = plan
    last = N_MOE_BLOCKS - 1

    def block_of(step, lag):
        return jnp.clip(step - lag, 0, last)

    src_spec = pl.BlockSpec((1, 1, MOE_ROWS), lambda s, *_: (block_of(s, 0), 0, 0), memory_space=pltpu.SMEM)
    dst_spec = pl.BlockSpec((1, 1, MOE_ROWS), lambda s, *_: (block_of(s, 2), 0, 0), memory_space=pltpu.SMEM)
    up_a = pl.BlockSpec((1, D, D_EXPERT), lambda s, ea, eb, cnt: (ea[block_of(s, 1)], 0, 0))
    up_b = pl.BlockSpec((1, D, D_EXPERT), lambda s, ea, eb, cnt: (eb[block_of(s, 1)], 0, 0))
    down_a = pl.BlockSpec((1, D_EXPERT, D), lambda s, ea, eb, cnt: (ea[block_of(s, 1)], 0, 0))
    down_b = pl.BlockSpec((1, D_EXPERT, D), lambda s, ea, eb, cnt: (eb[block_of(s, 1)], 0, 0))
    grid_spec = pltpu.PrefetchScalarGridSpec(
        num_scalar_prefetch=3,
        grid=(N_MOE_BLOCKS + 2,),
        in_specs=[src_spec, dst_spec, pl.BlockSpec(memory_space=pl.ANY),
                  up_a, up_a, down_a, up_b, up_b, down_b],
        out_specs=pl.BlockSpec(memory_space=pl.ANY),
        scratch_shapes=[pltpu.VMEM((2, MOE_ROWS * ROUTED, LANES), F32),
                        pltpu.VMEM((2, MOE_ROWS * ROW_TILE, LANES), F32),
                        pltpu.SemaphoreType.DMA((2,)), pltpu.SemaphoreType.DMA((2,))],
    )
    return pl.pallas_call(
        _moe_body, name="moe",
        grid_spec=grid_spec,
        out_shape=jax.ShapeDtypeStruct(((T_ALL + 2 * MOE_ROWS) * ROW_TILE, LANES), F32),
        compiler_params=pltpu.CompilerParams(dimension_semantics=("arbitrary",)),
    )(ea, eb, cnt, src, dst, rows, w_gate, w_up, w_down, w_gate, w_up, w_down)


def _l1_body(x_ref, xprev_ref, xnext_ref, y_ref, yprev_ref, ynext_ref, gm_ref, wi_ref, cw_ref, wo_ref,
             g_ref, wr_ref, br_ref, x_out_ref, row_ref, meta_ref, buf, ybuf):
    starts, ends = _seq_edges(pl.program_id(0))
    x = x_ref[...] + _read_row_tiles(y_ref, TB)
    x_edge = jnp.concatenate([xprev_ref[...] + _read_row_tiles(yprev_ref, EDGE),
                              xnext_ref[...] + _read_row_tiles(ynext_ref, EDGE)], axis=0)
    h = jnp.concatenate([_rms(x, gm_ref[...]), _rms(x_edge, gm_ref[...])], axis=0).astype(BF16)
    u = _dot(h, wi_ref[...])
    cv = u[:, D:2 * D] * u[:, 2 * D:]
    buf[HALO - EDGE:HALO, :] = jnp.where(starts, 0.0, cv[TB:TB + EDGE, :])
    buf[HALO:HALO + TB, :] = cv[:TB, :]
    buf[HALO + TB:HALO + TB + EDGE, :] = jnp.where(ends, 0.0, cv[TB + EDGE:, :])
    rows = CONV_ROWS // 2
    for r0 in range(0, TB, rows):
        conv = (cw_ref[0:1, :] * buf[r0 + HALO - 1:r0 + HALO - 1 + rows, :]
                + cw_ref[1:2, :] * buf[r0 + HALO:r0 + HALO + rows, :]
                + cw_ref[2:3, :] * buf[r0 + HALO + 1:r0 + HALO + 1 + rows, :])
        ybuf[r0:r0 + rows, :] = (u[r0:r0 + rows, :D] * conv).astype(BF16)
    _route_and_store(x + _dot(ybuf[...], wo_ref[...]), g_ref, wr_ref, br_ref, x_out_ref, row_ref, meta_ref)


def _l1(x, y, g_mix, w_in, conv_w, w_out, g_ffn, wr, br):
    per_tile = TB // EDGE
    last = T_ALL // EDGE - 1

    def prev_of(i):
        return jnp.maximum(i * per_tile - 1, 0)

    def next_of(i):
        return jnp.minimum((i + 1) * per_tile, last)

    def resident(shape):
        return pl.BlockSpec(shape, lambda *_: (0,) * len(shape), pipeline_mode=pl.Buffered(1))

    return pl.pallas_call(
        _l1_body, name="l1_mixer",
        grid=(N_TILES,),
        in_specs=[_tile_spec(D),
                  pl.BlockSpec((EDGE, D), lambda i: (prev_of(i), 0)),
                  pl.BlockSpec((EDGE, D), lambda i: (next_of(i), 0)),
                  _row_tiles_spec(),
                  pl.BlockSpec((EDGE * ROW_TILE, LANES), lambda i: (prev_of(i), 0)),
                  pl.BlockSpec((EDGE * ROW_TILE, LANES), lambda i: (next_of(i), 0)),
                  _full_spec((1, D)), resident((D, 3 * D)), _full_spec((8, D)), resident((D, D)),
                  _full_spec((1, D)), _full_spec((D, 2 * META)), _full_spec((1, META))],
        out_specs=[_tile_spec(D), _row_tiles_spec(ROUTED), _tile_spec(META)],
        out_shape=[jax.ShapeDtypeStruct((T_ALL, D), F32), jax.ShapeDtypeStruct((T_ALL * ROUTED, LANES), F32),
                   jax.ShapeDtypeStruct((T_ALL, META), F32)],
        scratch_shapes=[pltpu.VMEM((TB + 2 * HALO, D), F32), pltpu.VMEM((TB, D), BF16)],
        compiler_params=pltpu.CompilerParams(vmem_limit_bytes=56 * 1024 * 1024),
    )(x, x, x, y, y, y, g_mix, w_in, conv_w, w_out, g_ffn, wr, br)


def _final_body(x_ref, y_ref, g_ref, o_ref):
    o_ref[...] = _rms(x_ref[...] + _read_row_tiles(y_ref, TB), g_ref[...])


def _final(x, y, g, *, tile0, tiles):
    in_tile = pl.BlockSpec((TB, D), lambda i: (i + tile0, 0))
    y_tile = pl.BlockSpec((TB * ROW_TILE, LANES), lambda i: (i + tile0, 0))
    return pl.pallas_call(
        _final_body, name="final_norm",
        grid=(tiles,),
        in_specs=[in_tile, y_tile, _full_spec((1, D))],
        out_specs=_tile_spec(D),
        out_shape=jax.ShapeDtypeStruct((tiles * TB, D), F32),
    )(x, y, g)


def kernel(x_prompt, x_sample, l0_norm_mix, l0_w_in, l0_conv_w, l0_conv_b, l0_ln_g, l0_ln_b, l0_w_out, l0_norm_ffn, l0_w_coarse, l0_b_coarse, l0_w_fine, l0_b_fine, l0_w_gate, l0_w_up, l0_w_down, l1_norm_mix, l1_w_in, l1_conv_w, l1_w_out, l1_norm_ffn, l1_w_coarse, l1_b_coarse, l1_w_fine, l1_b_fine, l1_w_gate, l1_w_up, l1_w_down, final_norm):
    xp = x_prompt.reshape(T_PROMPT, D)
    xs = x_sample.reshape(T_SAMPLE, D)
    row = lambda v: v.reshape(1, -1)

    a, f = _l0_in(xp, xs, row(l0_norm_mix), l0_w_in.astype(BF16))
    conv_w = jnp.pad(l0_conv_w, ((0, 1), (0, 0)))
    d1, l2 = _dft_prompt_tables()
    cs_prompt = _dft2_prompt(_dft1_prompt(f, d1), l2)
    cs_sample = _dft_sample(f, _dft_sample_table())
    w_cs = _fold_channel_dft(l0_w_out[HALF:])
    wr0, br0 = _router_operands(l0_w_coarse, l0_b_coarse, l0_w_fine, l0_b_fine)
    x1, rows0, meta0 = _l0_out(a, conv_w, row(l0_conv_b), row(l0_ln_g), row(l0_ln_b), cs_prompt, cs_sample, xp, xs,
                               l0_w_out[:HALF].astype(BF16), w_cs[:HALF], w_cs[HALF:], row(l0_norm_ffn), wr0, br0)

    y0 = _moe(rows0, _moe_plan(meta0), l0_w_gate.astype(BF16), l0_w_up.astype(BF16), l0_w_down.astype(BF16))

    wr1, br1 = _router_operands(l1_w_coarse, l1_b_coarse, l1_w_fine, l1_b_fine)
    x3, rows1, meta1 = _l1(x1, y0, row(l1_norm_mix), l1_w_in.astype(BF16), jnp.pad(l1_conv_w, ((0, 5), (0, 0))),
                           l1_w_out.astype(BF16), row(l1_norm_ffn), wr1, br1)

    y1 = _moe(rows1, _moe_plan(meta1), l1_w_gate.astype(BF16), l1_w_up.astype(BF16), l1_w_down.astype(BF16))

    out_p = _final(x3, y1, row(final_norm), tile0=0, tiles=N_PROMPT_TILES)
    out_s = _final(x3, y1, row(final_norm), tile0=N_PROMPT_TILES, tiles=N_TILES - N_PROMPT_TILES)
    return out_p.reshape(x_prompt.shape), out_s.reshape(x_sample.shape)
```

```python
import math

import jax
import jax.numpy as jnp
import numpy as np
from jax import lax
from jax.experimental import pallas as pl
from jax.experimental.pallas import tpu as pltpu

D = 1024
LANES = 128
T_PROMPT = 16384
SEQ_SAMPLE = 2048
T_SAMPLE = 32 * SEQ_SAMPLE
T_ALL = T_PROMPT + T_SAMPLE
HALF = 512
GROUP_CH = 64
CONV_W = 31
N_GROUPS = 4
EPG = 8
N_PAIRS = EPG * (EPG - 1) // 2
N_BUCKETS = N_GROUPS * N_PAIRS
D_EXPERT = 512
RMS_EPS = 1e-6
LN_EPS = 1e-5

TB = 512
N_TILES = T_ALL // TB
N_PROMPT_TILES = T_PROMPT // TB
HALO = 16
EDGE = 8
CONV_ROWS = 64
MOE_ROWS = 256
N_MOE_BLOCKS = T_ALL // MOE_ROWS + N_BUCKETS + 1
META = 128
ROW_TILE = D // LANES
ROUTED = ROW_TILE + 1
FFT_N2 = 128
FFT_N1_PROMPT = T_PROMPT // FFT_N2
FFT_G1 = 16
FFT_KB = 8
DFT_ROWS = 1024

BF16 = jnp.bfloat16
F32 = jnp.float32


def _rms(x, g):
    return x * lax.rsqrt(jnp.mean(x * x, axis=-1, keepdims=True) + RMS_EPS) * g


def _dot(a, b):
    return jnp.dot(a, b, preferred_element_type=F32)


def _seq_edges(i):
    r0 = i * TB
    r1 = r0 + TB
    in_prompt = r0 < T_PROMPT
    starts = jnp.where(in_prompt, r0 == 0, (r0 - T_PROMPT) % SEQ_SAMPLE == 0)
    ends = jnp.where(in_prompt, r1 == T_PROMPT, (r1 - T_PROMPT) % SEQ_SAMPLE == 0)
    return starts, ends


def _tile_spec(width):
    return pl.BlockSpec((TB, width), lambda i: (i, 0))


def _row_tiles_spec(rows_per_token=ROW_TILE):
    return pl.BlockSpec((TB * rows_per_token, LANES), lambda i: (i, 0))


def _read_row_tiles(ref, tokens):
    return jnp.concatenate([ref[pl.ds(k, tokens, stride=ROW_TILE), :] for k in range(ROW_TILE)], axis=1)


def _full_spec(shape):
    return pl.BlockSpec(shape, lambda *_: (0,) * len(shape))


def _prompt_spec(width=D):
    return pl.BlockSpec((TB, width), lambda i: (jnp.minimum(i, N_PROMPT_TILES - 1), 0))


def _sample_spec(width=D):
    return pl.BlockSpec((TB, width), lambda i: (jnp.maximum(i - N_PROMPT_TILES, 0), 0))


def _halo_specs(width):
    per_tile = TB // HALO
    last = T_ALL // HALO - 1
    prev = pl.BlockSpec((HALO, width), lambda i: (jnp.maximum(i * per_tile - 1, 0), 0))
    nxt = pl.BlockSpec((HALO, width), lambda i: (jnp.minimum((i + 1) * per_tile, last), 0))
    return prev, nxt


def _l0_in_body(xp_ref, xs_ref, g_ref, w_ref, a_ref, f_ref):
    i = pl.program_id(0)
    x = jnp.where(i < N_PROMPT_TILES, xp_ref[...], xs_ref[...])
    h = _rms(x, g_ref[...])
    u = _dot(h.astype(BF16), w_ref[...])
    a_ref[...] = u[:, :HALF] * jax.nn.sigmoid(u[:, HALF:2 * HALF])
    f_ref[...] = u[:, 2 * HALF:].astype(BF16)


def _l0_in(xp, xs, g, w_in):
    return pl.pallas_call(
        _l0_in_body, name="l0_in",
        grid=(N_TILES,),
        in_specs=[_prompt_spec(), _sample_spec(), _full_spec((1, D)), _full_spec((D, 3 * HALF))],
        out_specs=[_tile_spec(HALF), _tile_spec(HALF)],
        out_shape=[jax.ShapeDtypeStruct((T_ALL, HALF), F32), jax.ShapeDtypeStruct((T_ALL, HALF), BF16)],
    )(xp, xs, g, w_in)


def _conv_module(prev_ref, cur_ref, next_ref, w_ref, b_ref, g_ref, beta_ref, o_ref, buf):
    starts, ends = _seq_edges(pl.program_id(0))
    buf[0, 0:HALO, :] = jnp.where(starts, 0.0, prev_ref[...])
    buf[0, HALO:HALO + TB, :] = cur_ref[...]
    buf[0, HALO + TB:, :] = jnp.where(ends, 0.0, next_ref[...])
    rows = TB + 2 * HALO - 8
    for s in range(1, 8):
        for c0 in range(0, rows, 128):
            c1 = min(c0 + 128, rows)
            buf[s, c0:c1, :] = buf[0, c0 + s:c1 + s, :]
    first = HALO - CONV_W // 2
    for r0 in range(0, TB, CONV_ROWS):
        acc = jnp.broadcast_to(b_ref[...], (CONV_ROWS, HALF))
        for k in range(CONV_W):
            s = (first + k) % 8
            base = r0 + first + k - s
            acc = acc + w_ref[k:k + 1, :] * buf[s, base:base + CONV_ROWS, :]
        mu = jnp.mean(acc, axis=-1, keepdims=True)
        xc = acc - mu
        var = jnp.mean(xc * xc, axis=-1, keepdims=True)
        y = xc * lax.rsqrt(var + LN_EPS) * g_ref[...] + beta_ref[...]
        o_ref[r0:r0 + CONV_ROWS, :] = (y * jax.nn.sigmoid(y)).astype(BF16)


def _dft1_prompt_body(d_ref, x_ref, y_ref, xs, ys):
    n1 = FFT_N1_PROMPT
    x = x_ref[0].reshape(n1 * FFT_G1, HALF).astype(F32)
    for c in range(HALF // LANES):
        xs[c] = x[:, c * LANES:(c + 1) * LANES]
    for q in range(FFT_G1):
        rows = pl.ds(q, n1, stride=FFT_G1)
        xq = jnp.concatenate([xs[c, rows, :] for c in range(HALF // LANES)], axis=1)
        y = _dot(d_ref[...], xq.astype(BF16))
        for c in range(HALF // LANES):
            ys[0, c, rows, :] = y[:n1, c * LANES:(c + 1) * LANES]
            ys[1, c, rows, :] = y[n1:, c * LANES:(c + 1) * LANES]
    for p in range(2):
        for c in range(HALF // LANES):
            y_ref[p, :, :, c * LANES:(c + 1) * LANES] = ys[p, c].reshape(n1, FFT_G1, LANES)


def _dft1_prompt(f_all, d1):
    n1 = FFT_N1_PROMPT
    x = f_all.reshape(T_ALL // T_PROMPT, n1, FFT_N2, HALF)
    return pl.pallas_call(
        _dft1_prompt_body, name="dft1_prompt",
        grid=(FFT_N2 // FFT_G1,),
        in_specs=[_full_spec((2 * n1, n1)),
                  pl.BlockSpec((1, n1, FFT_G1, HALF), lambda j: (0, 0, j, 0))],
        out_specs=pl.BlockSpec((2, n1, FFT_G1, HALF), lambda j: (0, 0, j, 0)),
        out_shape=jax.ShapeDtypeStruct((2, n1, FFT_N2, HALF), F32),
        scratch_shapes=[pltpu.VMEM((HALF // LANES, n1 * FFT_G1, LANES), F32),
                        pltpu.VMEM((2, HALF // LANES, n1 * FFT_G1, LANES), F32)],
        compiler_params=pltpu.CompilerParams(vmem_limit_bytes=48 * 1024 * 1024),
    )(d1, x)


def _dft2_prompt_body(l_ref, y_ref, c_ref, s_ref, cs, ss):
    for q in range(FFT_KB):
        yq = y_ref[:, q].reshape(2 * FFT_N2, HALF).astype(BF16)
        r = _dot(l_ref[q], yq)
        rows = pl.ds(q, FFT_N2, stride=FFT_KB)
        for c in range(HALF // LANES):
            cs[c, rows, :] = r[:FFT_N2, c * LANES:(c + 1) * LANES]
            ss[c, rows, :] = r[FFT_N2:, c * LANES:(c + 1) * LANES]
    for c in range(HALF // LANES):
        c_ref[:, :, c * LANES:(c + 1) * LANES] = cs[c].reshape(FFT_N2, FFT_KB, LANES)
        s_ref[:, :, c * LANES:(c + 1) * LANES] = ss[c].reshape(FFT_N2, FFT_KB, LANES)


def _dft2_prompt(y, l2):
    n1 = FFT_N1_PROMPT
    out_sds = jax.ShapeDtypeStruct((FFT_N2, n1, HALF), F32)
    out_spec = pl.BlockSpec((FFT_N2, FFT_KB, HALF), lambda k: (0, k, 0))
    c, s = pl.pallas_call(
        _dft2_prompt_body, name="dft2_prompt",
        grid=(n1 // FFT_KB,),
        in_specs=[pl.BlockSpec((FFT_KB, 2 * FFT_N2, 2 * FFT_N2), lambda k: (k, 0, 0)),
                  pl.BlockSpec((2, FFT_KB, FFT_N2, HALF), lambda k: (0, k, 0, 0))],
        out_specs=[out_spec, out_spec],
        out_shape=[out_sds, out_sds],
        scratch_shapes=[pltpu.VMEM((HALF // LANES, FFT_N2 * FFT_KB, LANES), F32),
                        pltpu.VMEM((HALF // LANES, FFT_N2 * FFT_KB, LANES), F32)],
    )(l2, y)
    return c.reshape(T_PROMPT, HALF), s.reshape(T_PROMPT, HALF)


def _dft_sample_body(m_ref, x_ref, o_ref):
    for part in range(2):
        for r0 in range(0, SEQ_SAMPLE, DFT_ROWS):
            rows = m_ref[part * SEQ_SAMPLE + r0:part * SEQ_SAMPLE + r0 + DFT_ROWS, :]
            o_ref[part, r0:r0 + DFT_ROWS, :] = _dot(rows, x_ref[...]).astype(BF16)


def _dft_sample(f_all, m):
    first = T_PROMPT // SEQ_SAMPLE
    return pl.pallas_call(
        _dft_sample_body, name="dft_sample",
        grid=(T_SAMPLE // SEQ_SAMPLE,),
        in_specs=[pl.BlockSpec((2 * SEQ_SAMPLE, SEQ_SAMPLE), lambda b: (0, 0), pipeline_mode=pl.Buffered(1)),
                  pl.BlockSpec((SEQ_SAMPLE, HALF), lambda b: (first + b, 0))],
        out_specs=pl.BlockSpec((2, SEQ_SAMPLE, HALF), lambda b: (0, b, 0)),
        out_shape=jax.ShapeDtypeStruct((2, T_SAMPLE, HALF), BF16),
        compiler_params=pltpu.CompilerParams(vmem_limit_bytes=48 * 1024 * 1024),
    )(m, f_all)


def _dft_sample_table():
    k = jnp.arange(SEQ_SAMPLE, dtype=jnp.int32)
    ang = (2.0 * math.pi / SEQ_SAMPLE) * ((k[:, None] * k[None, :]) % SEQ_SAMPLE).astype(F32)
    scale = SEQ_SAMPLE ** -0.5
    return jnp.concatenate([jnp.cos(ang) * scale, jnp.sin(ang) * scale], axis=0).astype(BF16)


def _dft_prompt_tables():
    n1 = FFT_N1_PROMPT
    n = n1 * FFT_N2
    k1 = jnp.arange(n1, dtype=jnp.int32)
    ang1 = (2.0 * math.pi / n1) * ((k1[:, None] * k1[None, :]) % n1).astype(F32)
    d1 = jnp.concatenate([jnp.cos(ang1), -jnp.sin(ang1)], axis=0).astype(BF16)
    k2 = jnp.arange(FFT_N2, dtype=jnp.int32)
    k = k1[:, None, None] + n1 * k2[None, :, None]
    ang2 = (2.0 * math.pi / n) * ((k * k2[None, None, :]) % n).astype(F32)
    cr = jnp.cos(ang2) * (n ** -0.5)
    sr = jnp.sin(ang2) * (n ** -0.5)
    top = jnp.concatenate([cr, sr], axis=2)
    bot = jnp.concatenate([sr, -cr], axis=2)
    l2 = jnp.concatenate([top, bot], axis=1).astype(BF16)
    return d1, l2


def _fold_body(m_ref, w_ref, o_ref):
    o_ref[...] = jnp.dot(m_ref[...], w_ref[...], preferred_element_type=F32,
                         precision=lax.Precision.HIGHEST).astype(BF16)


def _fold_channel_dft(w_out_f):
    c = jnp.arange(HALF, dtype=jnp.int32)
    same = (c[:, None] // GROUP_CH) == (c[None, :] // GROUP_CH)
    ang = (2.0 * math.pi / GROUP_CH) * ((c[:, None] * c[None, :]) % GROUP_CH).astype(F32)
    scale = GROUP_CH ** -0.5
    cbd = jnp.where(same, jnp.cos(ang), 0.0) * scale
    sbd = jnp.where(same, jnp.sin(ang), 0.0) * scale
    m = jnp.concatenate([cbd, -sbd], axis=0)
    return pl.pallas_call(
        _fold_body, name="fold_channel_dft",
        out_shape=jax.ShapeDtypeStruct((2 * HALF, D), BF16),
    )(m, w_out_f)


def _route_and_store(x_new, g_ref, wr_ref, br_ref, x_out_ref, row_ref, meta_ref):
    x_out_ref[...] = x_new
    h = _rms(x_new, g_ref[...])
    h_hi = h.astype(BF16)
    h_lo = (h - h_hi.astype(F32)).astype(BF16)
    p = _dot(h_hi, wr_ref[...])
    q = _dot(h_lo, wr_ref[...])
    lg = p[:, :META] + p[:, META:] + q[:, :META] + q[:, META:] + br_ref[...]
    lane = lax.broadcasted_iota(jnp.int32, lg.shape, 1).astype(F32)
    neg = -jnp.inf
    lc = jnp.where(lane < N_GROUPS, lg, neg)
    mc = jnp.max(lc, axis=-1, keepdims=True)
    grp = jnp.min(jnp.where(lc == mc, lane, META), axis=-1, keepdims=True)
    p_grp = 1.0 / jnp.sum(jnp.exp(lc - mc), axis=-1, keepdims=True)
    lo = N_GROUPS + EPG * grp
    lf = jnp.where((lane >= lo) & (lane < lo + EPG), lg, neg)
    m1 = jnp.max(lf, axis=-1, keepdims=True)
    i1 = jnp.min(jnp.where(lf == m1, lane, META), axis=-1, keepdims=True)
    lf2 = jnp.where(lane == i1, neg, lf)
    m2 = jnp.max(lf2, axis=-1, keepdims=True)
    i2 = jnp.min(jnp.where(lf2 == m2, lane, META), axis=-1, keepdims=True)
    e = jnp.exp(m2 - m1)
    gate1 = p_grp / (1.0 + e)
    gate2 = p_grp * e / (1.0 + e)
    j1 = i1 - lo
    j2 = i2 - lo
    ja = jnp.minimum(j1, j2)
    jb = jnp.maximum(j1, j2)
    bucket = grp * N_PAIRS + (ja * (2 * EPG - 1 - ja)) * 0.5 + (jb - ja - 1)
    gate_a = jnp.where(j1 < j2, gate1, gate2)
    gate_b = jnp.where(j1 < j2, gate2, gate1)
    meta = jnp.where(lane == 0, gate_a,
                     jnp.where(lane == 1, gate_b,
                               jnp.where(lane == 2, bucket, 0.0)))
    for k in range(ROW_TILE):
        row_ref[pl.ds(k, TB, stride=ROUTED), :] = h[:, k * LANES:(k + 1) * LANES]
    row_ref[pl.ds(ROW_TILE, TB, stride=ROUTED), :] = meta
    meta_ref[...] = meta


def _router_operands(w_coarse, b_coarse, w_fine, b_fine):
    w = jnp.concatenate([w_coarse, w_fine.reshape(D, N_GROUPS * EPG)], axis=1)
    w = jnp.pad(w, ((0, 0), (0, META - w.shape[1])))
    w_hi = w.astype(BF16)
    w_lo = (w - w_hi.astype(F32)).astype(BF16)
    b = jnp.concatenate([b_coarse, b_fine.reshape(-1)])
    b = jnp.pad(b, (0, META - b.shape[0])).reshape(1, META)
    return jnp.concatenate([w_hi, w_lo], axis=1), b


def _l0_out_body(prev_ref, cur_ref, next_ref, cw_ref, cb_ref, lg_ref, lb_ref,
                 cp_ref, sp_ref, cs_ref, ss_ref, xp_ref, xs_ref, wa_ref, wc_ref, ws_ref,
                 g_ref, wr_ref, br_ref, x_out_ref, row_ref, meta_ref, buf, a2):
    in_prompt = pl.program_id(0) < N_PROMPT_TILES
    x = jnp.where(in_prompt, xp_ref[...], xs_ref[...])
    c = jnp.where(in_prompt, cp_ref[...].astype(BF16), cs_ref[...])
    s = jnp.where(in_prompt, sp_ref[...].astype(BF16), ss_ref[...])
    mix = _dot(c, wc_ref[...]) + _dot(s, ws_ref[...])
    _conv_module(prev_ref, cur_ref, next_ref, cw_ref, cb_ref, lg_ref, lb_ref, a2, buf)
    mix = mix + _dot(a2[...], wa_ref[...])
    _route_and_store(x + mix, g_ref, wr_ref, br_ref, x_out_ref, row_ref, meta_ref)


def _l0_out(a, conv_w, conv_b, ln_g, ln_b, cs_prompt, cs_sample, xp, xs, wa, wc, ws, g, wr, br):
    def sample_half(which):
        return pl.BlockSpec((None, TB, HALF), lambda i: (which, jnp.maximum(i - N_PROMPT_TILES, 0), 0))
    prev, nxt = _halo_specs(HALF)
    return pl.pallas_call(
        _l0_out_body, name="l0_out",
        grid=(N_TILES,),
        in_specs=[prev, _tile_spec(HALF), nxt, _full_spec((CONV_W + 1, HALF)),
                  _full_spec((1, HALF)), _full_spec((1, HALF)), _full_spec((1, HALF)),
                  _prompt_spec(HALF), _prompt_spec(HALF), sample_half(0), sample_half(1),
                  _prompt_spec(), _sample_spec(),
                  _full_spec((HALF, D)), _full_spec((HALF, D)), _full_spec((HALF, D)),
                  _full_spec((1, D)), _full_spec((D, 2 * META)), _full_spec((1, META))],
        out_specs=[_tile_spec(D), _row_tiles_spec(ROUTED), _tile_spec(META)],
        out_shape=[jax.ShapeDtypeStruct((T_ALL, D), F32), jax.ShapeDtypeStruct((T_ALL * ROUTED, LANES), F32),
                   jax.ShapeDtypeStruct((T_ALL, META), F32)],
        scratch_shapes=[pltpu.VMEM((8, TB + 2 * HALO, HALF), F32), pltpu.VMEM((TB, HALF), BF16)],
    )(a, a, a, conv_w, conv_b, ln_g, ln_b, *cs_prompt, cs_sample, cs_sample, xp, xs, wa, wc, ws, g, wr, br)


def _moe_plan(meta):
    bucket = meta[:, 2].astype(jnp.int32)
    sorted_b, order = lax.sort((bucket, jnp.arange(T_ALL, dtype=jnp.int32)), num_keys=1, is_stable=True)
    edges = jnp.arange(N_BUCKETS + 1, dtype=jnp.int32)
    starts = jnp.sum(sorted_b[None, :] < edges[:, None], axis=1).astype(jnp.int32)
    counts = starts[1:] - starts[:-1]
    nblk = (counts + MOE_ROWS - 1) // MOE_ROWS
    cum = jnp.cumsum(nblk)
    first = cum - nblk
    j = jnp.arange(N_MOE_BLOCKS, dtype=jnp.int32)
    valid = j < cum[-1]
    j_eff = jnp.minimum(j, cum[-1] - 1)
    bj = jnp.minimum(jnp.sum(cum[None, :] <= j_eff[:, None], axis=1), N_BUCKETS - 1).astype(jnp.int32)
    off = (j_eff - first[bj]) * MOE_ROWS
    cnt = jnp.where(valid, jnp.clip(counts[bj] - off, 0, MOE_ROWS), 0).astype(jnp.int32)
    r = jnp.arange(MOE_ROWS, dtype=jnp.int32)[None, :]
    pos = jnp.clip(starts[bj][:, None] + off[:, None] + r, 0, T_ALL - 1)
    tok = order[pos]
    spare = T_ALL + (j % 2)[:, None] * MOE_ROWS + r
    src = (tok * ROUTED).reshape(N_MOE_BLOCKS, 1, MOE_ROWS)
    dst = (jnp.where(r < cnt[:, None], tok, spare) * ROW_TILE).reshape(N_MOE_BLOCKS, 1, MOE_ROWS)
    pa, pb = np.triu_indices(EPG, k=1)
    grp = bj // N_PAIRS
    ea = grp * EPG + jnp.asarray(pa, jnp.int32)[bj % N_PAIRS]
    eb = grp * EPG + jnp.asarray(pb, jnp.int32)[bj % N_PAIRS]
    return ea.astype(jnp.int32), eb.astype(jnp.int32), cnt, src, dst


def _moe_body(ea_ref, eb_ref, cnt_ref, src_ref, dst_ref, rows_hbm,
              wga_ref, wua_ref, wda_ref, wgb_ref, wub_ref, wdb_ref,
              y_hbm, xbuf, ybuf, gsem, ssem):
    s = pl.program_id(0)
    tile = ROW_TILE
    buf_rows = MOE_ROWS * tile

    def rows_in(b):
        inside = (b >= 0) & (b < N_MOE_BLOCKS)
        return jnp.where(inside, cnt_ref[jnp.clip(b, 0, N_MOE_BLOCKS - 1)], 0)

    def wait_fetch(p):
        pltpu.make_async_copy(rows_hbm.at[pl.ds(0, MOE_ROWS * ROUTED)], xbuf.at[p], gsem.at[p]).wait()

    def wait_send(p):
        pltpu.make_async_copy(ybuf.at[p], y_hbm.at[pl.ds(0, buf_rows)], ssem.at[p]).wait()

    @pl.when(s == 0)
    def _():
        ybuf[...] = jnp.zeros(ybuf.shape, ybuf.dtype)
        for p in range(2):
            clear = pltpu.make_async_copy(ybuf.at[p], y_hbm.at[pl.ds((T_ALL + p * MOE_ROWS) * tile, buf_rows)],
                                          ssem.at[p])
            clear.start()
            clear.wait()

    for p in range(2):
        q = 1 - p
        mine = (s % 2) == p

        @pl.when(mine & (rows_in(s - 3) > 0))
        def _():
            wait_send(q)

        @pl.when(mine & (rows_in(s) > 0))
        def _():
            for r in range(MOE_ROWS):
                pltpu.make_async_copy(rows_hbm.at[pl.ds(src_ref[0, 0, r], ROUTED)],
                                      xbuf.at[p, pl.ds(r * ROUTED, ROUTED)], gsem.at[p]).start(priority=r % 2)

        @pl.when(mine & (rows_in(s - 1) > 0))
        def _():
            wait_fetch(q)
            x = jnp.concatenate([xbuf[q, pl.ds(k, MOE_ROWS, stride=ROUTED), :] for k in range(tile)],
                                axis=1).astype(BF16)
            record = xbuf[q, pl.ds(tile, MOE_ROWS, stride=ROUTED), :]
            gate_a = record[:, 0:1]
            gate_b = record[:, 1:2]

            def hidden(wg_ref, wu_ref, gate):
                g = _dot(x, wg_ref[0])
                u = _dot(x, wu_ref[0])
                return (g * jax.nn.sigmoid(g) * u * gate).astype(BF16)

            y = (_dot(hidden(wga_ref, wua_ref, gate_a), wda_ref[0])
                 + _dot(hidden(wgb_ref, wub_ref, gate_b), wdb_ref[0]))
            for k in range(tile):
                ybuf[q, pl.ds(k, MOE_ROWS, stride=tile), :] = y[:, k * LANES:(k + 1) * LANES]
            for r in range(MOE_ROWS):
                start = pl.multiple_of(dst_ref[0, 0, r], tile)
                pltpu.make_async_copy(ybuf.at[q, pl.ds(r * tile, tile)], y_hbm.at[pl.ds(start, tile)],
                                      ssem.at[q]).start(priority=r % 2)


def _moe(rows, plan, w_gate, w_up, w_down):
    ea, eb, cnt, src, dst = plan
    last = N_MOE_BLOCKS - 1

    def block_of(step, lag):
        return jnp.clip(step - lag, 0, last)

    src_spec = pl.BlockSpec((1, 1, MOE_ROWS), lambda s, *_: (block_of(s, 0), 0, 0), memory_space=pltpu.SMEM)
    dst_spec = pl.BlockSpec((1, 1, MOE_ROWS), lambda s, *_: (block_of(s, 1), 0, 0), memory_space=pltpu.SMEM)
    up_a = pl.BlockSpec((1, D, D_EXPERT), lambda s, ea, eb, cnt: (ea[block_of(s, 1)], 0, 0))
    up_b = pl.BlockSpec((1, D, D_EXPERT), lambda s, ea, eb, cnt: (eb[block_of(s, 1)], 0, 0))
    down_a = pl.BlockSpec((1, D_EXPERT, D), lambda s, ea, eb, cnt: (ea[block_of(s, 1)], 0, 0))
    down_b = pl.BlockSpec((1, D_EXPERT, D), lambda s, ea, eb, cnt: (eb[block_of(s, 1)], 0, 0))
    grid_spec = pltpu.PrefetchScalarGridSpec(
        num_scalar_prefetch=3,
        grid=(N_MOE_BLOCKS + 2,),
        in_specs=[src_spec, dst_spec, pl.BlockSpec(memory_space=pl.ANY),
                  up_a, up_a, down_a, up_b, up_b, down_b],
        out_specs=pl.BlockSpec(memory_space=pl.ANY),
        scratch_shapes=[pltpu.VMEM((2, MOE_ROWS * ROUTED, LANES), F32),
                        pltpu.VMEM((2, MOE_ROWS * ROW_TILE, LANES), F32),
                        pltpu.SemaphoreType.DMA((2,)), pltpu.SemaphoreType.DMA((2,))],
    )
    return pl.pallas_call(
        _moe_body, name="moe",
        grid_spec=grid_spec,
        out_shape=jax.ShapeDtypeStruct(((T_ALL + 2 * MOE_ROWS) * ROW_TILE, LANES), F32),
        compiler_params=pltpu.CompilerParams(dimension_semantics=("arbitrary",)),
    )(ea, eb, cnt, src, dst, rows, w_gate, w_up, w_down, w_gate, w_up, w_down)


def _l1_body(x_ref, xprev_ref, xnext_ref, y_ref, yprev_ref, ynext_ref, gm_ref, wi_ref, cw_ref, wo_ref,
             g_ref, wr_ref, br_ref, x_out_ref, row_ref, meta_ref, buf, ybuf):
    starts, ends = _seq_edges(pl.program_id(0))
    x = x_ref[...] + _read_row_tiles(y_ref, TB)
    x_edge = jnp.concatenate([xprev_ref[...] + _read_row_tiles(yprev_ref, EDGE),
                              xnext_ref[...] + _read_row_tiles(ynext_ref, EDGE)], axis=0)
    h = jnp.concatenate([_rms(x, gm_ref[...]), _rms(x_edge, gm_ref[...])], axis=0).astype(BF16)
    u = _dot(h, wi_ref[...])
    cv = u[:, D:2 * D] * u[:, 2 * D:]
    buf[HALO - EDGE:HALO, :] = jnp.where(starts, 0.0, cv[TB:TB + EDGE, :])
    buf[HALO:HALO + TB, :] = cv[:TB, :]
    buf[HALO + TB:HALO + TB + EDGE, :] = jnp.where(ends, 0.0, cv[TB + EDGE:, :])
    rows = CONV_ROWS // 2
    for r0 in range(0, TB, rows):
        conv = (cw_ref[0:1, :] * buf[r0 + HALO - 1:r0 + HALO - 1 + rows, :]
                + cw_ref[1:2, :] * buf[r0 + HALO:r0 + HALO + rows, :]
                + cw_ref[2:3, :] * buf[r0 + HALO + 1:r0 + HALO + 1 + rows, :])
        ybuf[r0:r0 + rows, :] = (u[r0:r0 + rows, :D] * conv).astype(BF16)
    _route_and_store(x + _dot(ybuf[...], wo_ref[...]), g_ref, wr_ref, br_ref, x_out_ref, row_ref, meta_ref)


def _l1(x, y, g_mix, w_in, conv_w, w_out, g_ffn, wr, br):
    per_tile = TB // EDGE
    last = T_ALL // EDGE - 1

    def prev_of(i):
        return jnp.maximum(i * per_tile - 1, 0)

    def next_of(i):
        return jnp.minimum((i + 1) * per_tile, last)

    def resident(shape):
        return pl.BlockSpec(shape, lambda *_: (0,) * len(shape), pipeline_mode=pl.Buffered(1))

    return pl.pallas_call(
        _l1_body, name="l1_mixer",
        grid=(N_TILES,),
        in_specs=[_tile_spec(D),
                  pl.BlockSpec((EDGE, D), lambda i: (prev_of(i), 0)),
                  pl.BlockSpec((EDGE, D), lambda i: (next_of(i), 0)),
                  _row_tiles_spec(),
                  pl.BlockSpec((EDGE * ROW_TILE, LANES), lambda i: (prev_of(i), 0)),
                  pl.BlockSpec((EDGE * ROW_TILE, LANES), lambda i: (next_of(i), 0)),
                  _full_spec((1, D)), resident((D, 3 * D)), _full_spec((8, D)), resident((D, D)),
                  _full_spec((1, D)), _full_spec((D, 2 * META)), _full_spec((1, META))],
        out_specs=[_tile_spec(D), _row_tiles_spec(ROUTED), _tile_spec(META)],
        out_shape=[jax.ShapeDtypeStruct((T_ALL, D), F32), jax.ShapeDtypeStruct((T_ALL * ROUTED, LANES), F32),
                   jax.ShapeDtypeStruct((T_ALL, META), F32)],
        scratch_shapes=[pltpu.VMEM((TB + 2 * HALO, D), F32), pltpu.VMEM((TB, D), BF16)],
        compiler_params=pltpu.CompilerParams(vmem_limit_bytes=56 * 1024 * 1024),
    )(x, x, x, y, y, y, g_mix, w_in, conv_w, w_out, g_ffn, wr, br)


def _final_body(x_ref, y_ref, g_ref, o_ref):
    o_ref[...] = _rms(x_ref[...] + _read_row_tiles(y_ref, TB), g_ref[...])


def _final(x, y, g, *, tile0, tiles):
    in_tile = pl.BlockSpec((TB, D), lambda i: (i + tile0, 0))
    y_tile = pl.BlockSpec((TB * ROW_TILE, LANES), lambda i: (i + tile0, 0))
    return pl.pallas_call(
        _final_body, name="final_norm",
        grid=(tiles,),
        in_specs=[in_tile, y_tile, _full_spec((1, D))],
        out_specs=_tile_spec(D),
        out_shape=jax.ShapeDtypeStruct((tiles * TB, D), F32),
    )(x, y, g)


def kernel(x_prompt, x_sample, l0_norm_mix, l0_w_in, l0_conv_w, l0_conv_b, l0_ln_g, l0_ln_b, l0_w_out, l0_norm_ffn, l0_w_coarse, l0_b_coarse, l0_w_fine, l0_b_fine, l0_w_gate, l0_w_up, l0_w_down, l1_norm_mix, l1_w_in, l1_conv_w, l1_w_out, l1_norm_ffn, l1_w_coarse, l1_b_coarse, l1_w_fine, l1_b_fine, l1_w_gate, l1_w_up, l1_w_down, final_norm):
    xp = x_prompt.reshape(T_PROMPT, D)
    xs = x_sample.reshape(T_SAMPLE, D)
    row = lambda v: v.reshape(1, -1)

    a, f = _l0_in(xp, xs, row(l0_norm_mix), l0_w_in.astype(BF16))
    conv_w = jnp.pad(l0_conv_w, ((0, 1), (0, 0)))
    d1, l2 = _dft_prompt_tables()
    cs_prompt = _dft2_prompt(_dft1_prompt(f, d1), l2)
    cs_sample = _dft_sample(f, _dft_sample_table())
    w_cs = _fold_channel_dft(l0_w_out[HALF:])
    wr0, br0 = _router_operands(l0_w_coarse, l0_b_coarse, l0_w_fine, l0_b_fine)
    x1, rows0, meta0 = _l0_out(a, conv_w, row(l0_conv_b), row(l0_ln_g), row(l0_ln_b), cs_prompt, cs_sample, xp, xs,
                               l0_w_out[:HALF].astype(BF16), w_cs[:HALF], w_cs[HALF:], row(l0_norm_ffn), wr0, br0)

    y0 = _moe(rows0, _moe_plan(meta0), l0_w_gate.astype(BF16), l0_w_up.astype(BF16), l0_w_down.astype(BF16))

    wr1, br1 = _router_operands(l1_w_coarse, l1_b_coarse, l1_w_fine, l1_b_fine)
    x3, rows1, meta1 = _l1(x1, y0, row(l1_norm_mix), l1_w_in.astype(BF16), jnp.pad(l1_conv_w, ((0, 5), (0, 0))),
                           l1_w_out.astype(BF16), row(l1_norm_ffn), wr1, br1)

    y1 = _moe(rows1, _moe_plan(meta1), l1_w_gate.astype(BF16), l1_w_up.astype(BF16), l1_w_down.astype(BF16))

    out_p = _final(x3, y1, row(final_norm), tile0=0, tiles=N_PROMPT_TILES)
    out_s = _final(x3, y1, row(final_norm), tile0=N_PROMPT_TILES, tiles=N_TILES - N_PROMPT_TILES)
    return out_p.reshape(x_prompt.shape), out_s.reshape(x_sample.shape)
```

```python
import math

import jax
import jax.numpy as jnp
import numpy as np
from jax import lax
from jax.experimental import pallas as pl
from jax.experimental.pallas import tpu as pltpu

D = 1024
LANES = 128
T_PROMPT = 16384
SEQ_SAMPLE = 2048
T_SAMPLE = 32 * SEQ_SAMPLE
T_ALL = T_PROMPT + T_SAMPLE
HALF = 512
GROUP_CH = 64
CONV_W = 31
N_GROUPS = 4
EPG = 8
N_PAIRS = EPG * (EPG - 1) // 2
N_BUCKETS = N_GROUPS * N_PAIRS
D_EXPERT = 512
RMS_EPS = 1e-6
LN_EPS = 1e-5

TB = 512
N_TILES = T_ALL // TB
N_PROMPT_TILES = T_PROMPT // TB
HALO = 16
EDGE = 8
CONV_ROWS = 64
MOE_ROWS = 256
ISSUE_UNROLL = 8
N_MOE_BLOCKS = T_ALL // MOE_ROWS + N_BUCKETS + 1
META = 128
ROW_TILE = D // LANES
ROUTED = ROW_TILE + 1
FFT_N2 = 128
FFT_N1_PROMPT = T_PROMPT // FFT_N2
FFT_G1 = 16
FFT_KB = 8
DFT_ROWS = 1024

BF16 = jnp.bfloat16
F32 = jnp.float32


def _rms(x, g):
    return x * lax.rsqrt(jnp.mean(x * x, axis=-1, keepdims=True) + RMS_EPS) * g


def _dot(a, b):
    return jnp.dot(a, b, preferred_element_type=F32)


def _seq_edges(i):
    r0 = i * TB
    r1 = r0 + TB
    in_prompt = r0 < T_PROMPT
    starts = jnp.where(in_prompt, r0 == 0, (r0 - T_PROMPT) % SEQ_SAMPLE == 0)
    ends = jnp.where(in_prompt, r1 == T_PROMPT, (r1 - T_PROMPT) % SEQ_SAMPLE == 0)
    return starts, ends


def _tile_spec(width):
    return pl.BlockSpec((TB, width), lambda i: (i, 0))


def _row_tiles_spec(rows_per_token=ROW_TILE):
    return pl.BlockSpec((TB * rows_per_token, LANES), lambda i: (i, 0))


def _read_row_tiles(ref, tokens):
    return jnp.concatenate([ref[pl.ds(k, tokens, stride=ROW_TILE), :] for k in range(ROW_TILE)], axis=1)


def _full_spec(shape):
    return pl.BlockSpec(shape, lambda *_: (0,) * len(shape))


def _prompt_spec(width=D):
    return pl.BlockSpec((TB, width), lambda i: (jnp.minimum(i, N_PROMPT_TILES - 1), 0))


def _sample_spec(width=D):
    return pl.BlockSpec((TB, width), lambda i: (jnp.maximum(i - N_PROMPT_TILES, 0), 0))


def _halo_specs(width):
    per_tile = TB // HALO
    last = T_ALL // HALO - 1
    prev = pl.BlockSpec((HALO, width), lambda i: (jnp.maximum(i * per_tile - 1, 0), 0))
    nxt = pl.BlockSpec((HALO, width), lambda i: (jnp.minimum((i + 1) * per_tile, last), 0))
    return prev, nxt


def _l0_in_body(xp_ref, xs_ref, g_ref, w_ref, a_ref, f_ref):
    i = pl.program_id(0)
    x = jnp.where(i < N_PROMPT_TILES, xp_ref[...], xs_ref[...])
    h = _rms(x, g_ref[...])
    u = _dot(h.astype(BF16), w_ref[...])
    a_ref[...] = (u[:, :HALF] * jax.nn.sigmoid(u[:, HALF:2 * HALF])).astype(BF16)
    f_ref[...] = u[:, 2 * HALF:].astype(BF16)


def _l0_in(xp, xs, g, w_in):
    return pl.pallas_call(
        _l0_in_body, name="l0_in",
        grid=(N_TILES,),
        in_specs=[_prompt_spec(), _sample_spec(), _full_spec((1, D)), _full_spec((D, 3 * HALF))],
        out_specs=[_tile_spec(HALF), _tile_spec(HALF)],
        out_shape=[jax.ShapeDtypeStruct((T_ALL, HALF), BF16), jax.ShapeDtypeStruct((T_ALL, HALF), BF16)],
    )(xp, xs, g, w_in)


def _conv_module(prev_ref, cur_ref, next_ref, w_ref, b_ref, g_ref, beta_ref, o_ref, buf):
    starts, ends = _seq_edges(pl.program_id(0))
    buf[0, 0:HALO, :] = jnp.where(starts, 0.0, prev_ref[...].astype(F32))
    buf[0, HALO:HALO + TB, :] = cur_ref[...].astype(F32)
    buf[0, HALO + TB:, :] = jnp.where(ends, 0.0, next_ref[...].astype(F32))
    rows = TB + 2 * HALO - 8
    for s in range(1, 8):
        for c0 in range(0, rows, 128):
            c1 = min(c0 + 128, rows)
            buf[s, c0:c1, :] = buf[0, c0 + s:c1 + s, :]
    first = HALO - CONV_W // 2
    for r0 in range(0, TB, CONV_ROWS):
        acc = jnp.broadcast_to(b_ref[...], (CONV_ROWS, HALF))
        for k in range(CONV_W):
            s = (first + k) % 8
            base = r0 + first + k - s
            acc = acc + w_ref[k:k + 1, :] * buf[s, base:base + CONV_ROWS, :]
        mu = jnp.mean(acc, axis=-1, keepdims=True)
        xc = acc - mu
        var = jnp.mean(xc * xc, axis=-1, keepdims=True)
        y = xc * lax.rsqrt(var + LN_EPS) * g_ref[...] + beta_ref[...]
        o_ref[r0:r0 + CONV_ROWS, :] = (y * jax.nn.sigmoid(y)).astype(BF16)


def _dft1_prompt_body(d_ref, x_ref, y_ref, xs, ys):
    n1 = FFT_N1_PROMPT
    x = x_ref[0].reshape(n1 * FFT_G1, HALF).astype(F32)
    for c in range(HALF // LANES):
        xs[c] = x[:, c * LANES:(c + 1) * LANES]
    for q in range(FFT_G1):
        rows = pl.ds(q, n1, stride=FFT_G1)
        xq = jnp.concatenate([xs[c, rows, :] for c in range(HALF // LANES)], axis=1)
        y = _dot(d_ref[...], xq.astype(BF16))
        for c in range(HALF // LANES):
            ys[0, c, rows, :] = y[:n1, c * LANES:(c + 1) * LANES]
            ys[1, c, rows, :] = y[n1:, c * LANES:(c + 1) * LANES]
    for p in range(2):
        for c in range(HALF // LANES):
            y_ref[p, :, :, c * LANES:(c + 1) * LANES] = ys[p, c].reshape(n1, FFT_G1, LANES)


def _dft1_prompt(f_all, d1):
    n1 = FFT_N1_PROMPT
    x = f_all.reshape(T_ALL // T_PROMPT, n1, FFT_N2, HALF)
    return pl.pallas_call(
        _dft1_prompt_body, name="dft1_prompt",
        grid=(FFT_N2 // FFT_G1,),
        in_specs=[_full_spec((2 * n1, n1)),
                  pl.BlockSpec((1, n1, FFT_G1, HALF), lambda j: (0, 0, j, 0))],
        out_specs=pl.BlockSpec((2, n1, FFT_G1, HALF), lambda j: (0, 0, j, 0)),
        out_shape=jax.ShapeDtypeStruct((2, n1, FFT_N2, HALF), F32),
        scratch_shapes=[pltpu.VMEM((HALF // LANES, n1 * FFT_G1, LANES), F32),
                        pltpu.VMEM((2, HALF // LANES, n1 * FFT_G1, LANES), F32)],
        compiler_params=pltpu.CompilerParams(vmem_limit_bytes=48 * 1024 * 1024),
    )(d1, x)


def _dft2_prompt_body(l_ref, y_ref, c_ref, s_ref, cs, ss):
    for q in range(FFT_KB):
        yq = y_ref[:, q].reshape(2 * FFT_N2, HALF).astype(BF16)
        r = _dot(l_ref[q], yq)
        rows = pl.ds(q, FFT_N2, stride=FFT_KB)
        for c in range(HALF // LANES):
            cs[c, rows, :] = r[:FFT_N2, c * LANES:(c + 1) * LANES]
            ss[c, rows, :] = r[FFT_N2:, c * LANES:(c + 1) * LANES]
    for c in range(HALF // LANES):
        c_ref[:, :, c * LANES:(c + 1) * LANES] = cs[c].reshape(FFT_N2, FFT_KB, LANES)
        s_ref[:, :, c * LANES:(c + 1) * LANES] = ss[c].reshape(FFT_N2, FFT_KB, LANES)


def _dft2_prompt(y, l2):
    n1 = FFT_N1_PROMPT
    out_sds = jax.ShapeDtypeStruct((FFT_N2, n1, HALF), F32)
    out_spec = pl.BlockSpec((FFT_N2, FFT_KB, HALF), lambda k: (0, k, 0))
    c, s = pl.pallas_call(
        _dft2_prompt_body, name="dft2_prompt",
        grid=(n1 // FFT_KB,),
        in_specs=[pl.BlockSpec((FFT_KB, 2 * FFT_N2, 2 * FFT_N2), lambda k: (k, 0, 0)),
                  pl.BlockSpec((2, FFT_KB, FFT_N2, HALF), lambda k: (0, k, 0, 0))],
        out_specs=[out_spec, out_spec],
        out_shape=[out_sds, out_sds],
        scratch_shapes=[pltpu.VMEM((HALF // LANES, FFT_N2 * FFT_KB, LANES), F32),
                        pltpu.VMEM((HALF // LANES, FFT_N2 * FFT_KB, LANES), F32)],
    )(l2, y)
    return c.reshape(T_PROMPT, HALF), s.reshape(T_PROMPT, HALF)


def _dft_sample_body(m_ref, x_ref, o_ref):
    for part in range(2):
        for r0 in range(0, SEQ_SAMPLE, DFT_ROWS):
            rows = m_ref[part * SEQ_SAMPLE + r0:part * SEQ_SAMPLE + r0 + DFT_ROWS, :]
            o_ref[part, r0:r0 + DFT_ROWS, :] = _dot(rows, x_ref[...]).astype(BF16)


def _dft_sample(f_all, m):
    first = T_PROMPT // SEQ_SAMPLE
    return pl.pallas_call(
        _dft_sample_body, name="dft_sample",
        grid=(T_SAMPLE // SEQ_SAMPLE,),
        in_specs=[pl.BlockSpec((2 * SEQ_SAMPLE, SEQ_SAMPLE), lambda b: (0, 0), pipeline_mode=pl.Buffered(1)),
                  pl.BlockSpec((SEQ_SAMPLE, HALF), lambda b: (first + b, 0))],
        out_specs=pl.BlockSpec((2, SEQ_SAMPLE, HALF), lambda b: (0, b, 0)),
        out_shape=jax.ShapeDtypeStruct((2, T_SAMPLE, HALF), BF16),
        compiler_params=pltpu.CompilerParams(vmem_limit_bytes=48 * 1024 * 1024),
    )(m, f_all)


def _dft_sample_table():
    k = jnp.arange(SEQ_SAMPLE, dtype=jnp.int32)
    ang = (2.0 * math.pi / SEQ_SAMPLE) * ((k[:, None] * k[None, :]) % SEQ_SAMPLE).astype(F32)
    scale = SEQ_SAMPLE ** -0.5
    return jnp.concatenate([jnp.cos(ang) * scale, jnp.sin(ang) * scale], axis=0).astype(BF16)


def _dft_prompt_tables():
    n1 = FFT_N1_PROMPT
    n = n1 * FFT_N2
    k1 = jnp.arange(n1, dtype=jnp.int32)
    ang1 = (2.0 * math.pi / n1) * ((k1[:, None] * k1[None, :]) % n1).astype(F32)
    d1 = jnp.concatenate([jnp.cos(ang1), -jnp.sin(ang1)], axis=0).astype(BF16)
    k2 = jnp.arange(FFT_N2, dtype=jnp.int32)
    k = k1[:, None, None] + n1 * k2[None, :, None]
    ang2 = (2.0 * math.pi / n) * ((k * k2[None, None, :]) % n).astype(F32)
    cr = jnp.cos(ang2) * (n ** -0.5)
    sr = jnp.sin(ang2) * (n ** -0.5)
    top = jnp.concatenate([cr, sr], axis=2)
    bot = jnp.concatenate([sr, -cr], axis=2)
    l2 = jnp.concatenate([top, bot], axis=1).astype(BF16)
    return d1, l2


def _fold_body(m_ref, w_ref, o_ref):
    o_ref[...] = jnp.dot(m_ref[...], w_ref[...], preferred_element_type=F32,
                         precision=lax.Precision.HIGHEST).astype(BF16)


def _fold_channel_dft(w_out_f):
    c = jnp.arange(HALF, dtype=jnp.int32)
    same = (c[:, None] // GROUP_CH) == (c[None, :] // GROUP_CH)
    ang = (2.0 * math.pi / GROUP_CH) * ((c[:, None] * c[None, :]) % GROUP_CH).astype(F32)
    scale = GROUP_CH ** -0.5
    cbd = jnp.where(same, jnp.cos(ang), 0.0) * scale
    sbd = jnp.where(same, jnp.sin(ang), 0.0) * scale
    m = jnp.concatenate([cbd, -sbd], axis=0)
    return pl.pallas_call(
        _fold_body, name="fold_channel_dft",
        out_shape=jax.ShapeDtypeStruct((2 * HALF, D), BF16),
    )(m, w_out_f)


def _route_and_store(x_new, g_ref, wr_ref, br_ref, x_out_ref, row_ref, meta_ref):
    x_out_ref[...] = x_new
    h = _rms(x_new, g_ref[...])
    h_hi = h.astype(BF16)
    h_lo = (h - h_hi.astype(F32)).astype(BF16)
    p = _dot(h_hi, wr_ref[...])
    q = _dot(h_lo, wr_ref[...])
    lg = p[:, :META] + p[:, META:] + q[:, :META] + q[:, META:] + br_ref[...]
    lane = lax.broadcasted_iota(jnp.int32, lg.shape, 1).astype(F32)
    neg = -jnp.inf
    lc = jnp.where(lane < N_GROUPS, lg, neg)
    mc = jnp.max(lc, axis=-1, keepdims=True)
    grp = jnp.min(jnp.where(lc == mc, lane, META), axis=-1, keepdims=True)
    p_grp = 1.0 / jnp.sum(jnp.exp(lc - mc), axis=-1, keepdims=True)
    lo = N_GROUPS + EPG * grp
    lf = jnp.where((lane >= lo) & (lane < lo + EPG), lg, neg)
    m1 = jnp.max(lf, axis=-1, keepdims=True)
    i1 = jnp.min(jnp.where(lf == m1, lane, META), axis=-1, keepdims=True)
    lf2 = jnp.where(lane == i1, neg, lf)
    m2 = jnp.max(lf2, axis=-1, keepdims=True)
    i2 = jnp.min(jnp.where(lf2 == m2, lane, META), axis=-1, keepdims=True)
    e = jnp.exp(m2 - m1)
    gate1 = p_grp / (1.0 + e)
    gate2 = p_grp * e / (1.0 + e)
    j1 = i1 - lo
    j2 = i2 - lo
    ja = jnp.minimum(j1, j2)
    jb = jnp.maximum(j1, j2)
    bucket = grp * N_PAIRS + (ja * (2 * EPG - 1 - ja)) * 0.5 + (jb - ja - 1)
    gate_a = jnp.where(j1 < j2, gate1, gate2)
    gate_b = jnp.where(j1 < j2, gate2, gate1)
    meta = jnp.where(lane == 0, gate_a,
                     jnp.where(lane == 1, gate_b,
                               jnp.where(lane == 2, bucket, 0.0)))
    for k in range(ROW_TILE):
        row_ref[pl.ds(k, TB, stride=ROUTED), :] = h[:, k * LANES:(k + 1) * LANES]
    row_ref[pl.ds(ROW_TILE, TB, stride=ROUTED), :] = meta
    meta_ref[...] = meta


def _router_operands(w_coarse, b_coarse, w_fine, b_fine):
    w = jnp.concatenate([w_coarse, w_fine.reshape(D, N_GROUPS * EPG)], axis=1)
    w = jnp.pad(w, ((0, 0), (0, META - w.shape[1])))
    w_hi = w.astype(BF16)
    w_lo = (w - w_hi.astype(F32)).astype(BF16)
    b = jnp.concatenate([b_coarse, b_fine.reshape(-1)])
    b = jnp.pad(b, (0, META - b.shape[0])).reshape(1, META)
    return jnp.concatenate([w_hi, w_lo], axis=1), b


def _l0_out_body(prev_ref, cur_ref, next_ref, cw_ref, cb_ref, lg_ref, lb_ref,
                 cp_ref, sp_ref, cs_ref, ss_ref, xp_ref, xs_ref, wa_ref, wc_ref, ws_ref,
                 g_ref, wr_ref, br_ref, x_out_ref, row_ref, meta_ref, buf, a2):
    in_prompt = pl.program_id(0) < N_PROMPT_TILES
    x = jnp.where(in_prompt, xp_ref[...], xs_ref[...])
    c = jnp.where(in_prompt, cp_ref[...].astype(BF16), cs_ref[...])
    s = jnp.where(in_prompt, sp_ref[...].astype(BF16), ss_ref[...])
    mix = _dot(c, wc_ref[...]) + _dot(s, ws_ref[...])
    _conv_module(prev_ref, cur_ref, next_ref, cw_ref, cb_ref, lg_ref, lb_ref, a2, buf)
    mix = mix + _dot(a2[...], wa_ref[...])
    _route_and_store(x + mix, g_ref, wr_ref, br_ref, x_out_ref, row_ref, meta_ref)


def _l0_out(a, conv_w, conv_b, ln_g, ln_b, cs_prompt, cs_sample, xp, xs, wa, wc, ws, g, wr, br):
    def sample_half(which):
        return pl.BlockSpec((None, TB, HALF), lambda i: (which, jnp.maximum(i - N_PROMPT_TILES, 0), 0))
    prev, nxt = _halo_specs(HALF)
    return pl.pallas_call(
        _l0_out_body, name="l0_out",
        grid=(N_TILES,),
        in_specs=[prev, _tile_spec(HALF), nxt, _full_spec((CONV_W + 1, HALF)),
                  _full_spec((1, HALF)), _full_spec((1, HALF)), _full_spec((1, HALF)),
                  _prompt_spec(HALF), _prompt_spec(HALF), sample_half(0), sample_half(1),
                  _prompt_spec(), _sample_spec(),
                  _full_spec((HALF, D)), _full_spec((HALF, D)), _full_spec((HALF, D)),
                  _full_spec((1, D)), _full_spec((D, 2 * META)), _full_spec((1, META))],
        out_specs=[_tile_spec(D), _row_tiles_spec(ROUTED), _tile_spec(META)],
        out_shape=[jax.ShapeDtypeStruct((T_ALL, D), F32), jax.ShapeDtypeStruct((T_ALL * ROUTED, LANES), F32),
                   jax.ShapeDtypeStruct((T_ALL, META), F32)],
        scratch_shapes=[pltpu.VMEM((8, TB + 2 * HALO, HALF), F32), pltpu.VMEM((TB, HALF), BF16)],
    )(a, a, a, conv_w, conv_b, ln_g, ln_b, *cs_prompt, cs_sample, cs_sample, xp, xs, wa, wc, ws, g, wr, br)


def _moe_plan(meta):
    bucket = meta[:, 2].astype(jnp.int32)
    counts = jnp.sum(bucket[None, :] == jnp.arange(N_BUCKETS, dtype=jnp.int32)[:, None], axis=1).astype(jnp.int32)
    nblk = (counts + MOE_ROWS - 1) // MOE_ROWS
    fill = jnp.cumsum(nblk * MOE_ROWS - counts)
    slots = N_MOE_BLOCKS * MOE_ROWS
    pad_id = jnp.arange(slots - T_ALL, dtype=jnp.int32)
    pad_key = jnp.sum(fill[None, :] <= pad_id[:, None], axis=1).astype(jnp.int32)
    keys, ids = lax.sort((jnp.concatenate([bucket, pad_key]), jnp.arange(slots, dtype=jnp.int32)),
                         num_keys=1, is_stable=True)
    keys = keys.reshape(N_MOE_BLOCKS, MOE_ROWS)
    ids = ids.reshape(N_MOE_BLOCKS, MOE_ROWS)
    real = ids < T_ALL
    cnt = jnp.sum(real, axis=1).astype(jnp.int32)
    j = jnp.arange(N_MOE_BLOCKS, dtype=jnp.int32)
    r = jnp.arange(MOE_ROWS, dtype=jnp.int32)[None, :]
    spare = T_ALL + (j % 2)[:, None] * MOE_ROWS + r
    src = (jnp.minimum(ids, T_ALL - 1) * ROUTED).reshape(N_MOE_BLOCKS, 1, MOE_ROWS)
    dst = (jnp.where(real, ids, spare) * ROW_TILE).reshape(N_MOE_BLOCKS, 1, MOE_ROWS)
    bj = keys[:, 0]
    last_used = lax.dynamic_index_in_dim(bj, jnp.sum(nblk) - 1, keepdims=False)
    bj = jnp.where(bj < N_BUCKETS, bj, last_used)
    pa, pb = np.triu_indices(EPG, k=1)
    grp = bj // N_PAIRS
    ea = grp * EPG + jnp.asarray(pa, jnp.int32)[bj % N_PAIRS]
    eb = grp * EPG + jnp.asarray(pb, jnp.int32)[bj % N_PAIRS]
    return ea.astype(jnp.int32), eb.astype(jnp.int32), cnt, src, dst


def _moe_body(ea_ref, eb_ref, cnt_ref, src_ref, dst_ref, rows_hbm,
              wga_ref, wua_ref, wda_ref, wgb_ref, wub_ref, wdb_ref,
              y_hbm, xbuf, ybuf, gsem, ssem):
    s = pl.program_id(0)
    tile = ROW_TILE
    buf_rows = MOE_ROWS * tile

    def rows_in(b):
        inside = (b >= 0) & (b < N_MOE_BLOCKS)
        return jnp.where(inside, cnt_ref[jnp.clip(b, 0, N_MOE_BLOCKS - 1)], 0)

    def wait_fetch(p):
        pltpu.make_async_copy(rows_hbm.at[pl.ds(0, MOE_ROWS * ROUTED)], xbuf.at[p], gsem.at[p]).wait()

    def wait_send(p):
        pltpu.make_async_copy(ybuf.at[p], y_hbm.at[pl.ds(0, buf_rows)], ssem.at[p]).wait()

    @pl.when(s == 0)
    def _():
        ybuf[...] = jnp.zeros(ybuf.shape, ybuf.dtype)
        for p in range(2):
            clear = pltpu.make_async_copy(ybuf.at[p], y_hbm.at[pl.ds((T_ALL + p * MOE_ROWS) * tile, buf_rows)],
                                          ssem.at[p])
            clear.start()
            clear.wait()

    for p in range(2):
        q = 1 - p
        mine = (s % 2) == p

        @pl.when(mine & (rows_in(s - 3) > 0))
        def _():
            wait_send(q)

        @pl.when(mine & (rows_in(s) > 0))
        def _():
            def fetch_pair(i, carry):
                for lane in range(2):
                    r = 2 * i + lane
                    pltpu.make_async_copy(rows_hbm.at[pl.ds(src_ref[0, 0, r], ROUTED)],
                                          xbuf.at[p, pl.ds(r * ROUTED, ROUTED)], gsem.at[p]).start(priority=lane)
                return carry
            lax.fori_loop(0, MOE_ROWS // 2, fetch_pair, 0, unroll=ISSUE_UNROLL)

        @pl.when(mine & (rows_in(s - 1) > 0))
        def _():
            wait_fetch(q)
            x = jnp.concatenate([xbuf[q, pl.ds(k, MOE_ROWS, stride=ROUTED), :] for k in range(tile)],
                                axis=1).astype(BF16)
            record = xbuf[q, pl.ds(tile, MOE_ROWS, stride=ROUTED), :]
            gate_a = record[:, 0:1]
            gate_b = record[:, 1:2]

            def hidden(wg_ref, wu_ref, gate):
                g = _dot(x, wg_ref[0])
                u = _dot(x, wu_ref[0])
                return (g * jax.nn.sigmoid(g) * u * gate).astype(BF16)

            y = (_dot(hidden(wga_ref, wua_ref, gate_a), wda_ref[0])
                 + _dot(hidden(wgb_ref, wub_ref, gate_b), wdb_ref[0]))
            for k in range(tile):
                ybuf[q, pl.ds(k, MOE_ROWS, stride=tile), :] = y[:, k * LANES:(k + 1) * LANES]
            def send_pair(i, carry):
                for lane in range(2):
                    r = 2 * i + lane
                    start = pl.multiple_of(dst_ref[0, 0, r], tile)
                    pltpu.make_async_copy(ybuf.at[q, pl.ds(pl.multiple_of(r * tile, tile), tile)],
                                          y_hbm.at[pl.ds(start, tile)], ssem.at[q]).start(priority=lane)
                return carry
            lax.fori_loop(0, MOE_ROWS // 2, send_pair, 0, unroll=ISSUE_UNROLL)


def _moe(rows, plan, w_gate, w_up, w_down):
    ea, eb, cnt, src, dst = plan
    last = N_MOE_BLOCKS - 1

    def block_of(step, lag):
        return jnp.clip(step - lag, 0, last)

    src_spec = pl.BlockSpec((1, 1, MOE_ROWS), lambda s, *_: (block_of(s, 0), 0, 0), memory_space=pltpu.SMEM)
    dst_spec = pl.BlockSpec((1, 1, MOE_ROWS), lambda s, *_: (block_of(s, 1), 0, 0), memory_space=pltpu.SMEM)
    up_a = pl.BlockSpec((1, D, D_EXPERT), lambda s, ea, eb, cnt: (ea[block_of(s, 1)], 0, 0))
    up_b = pl.BlockSpec((1, D, D_EXPERT), lambda s, ea, eb, cnt: (eb[block_of(s, 1)], 0, 0))
    down_a = pl.BlockSpec((1, D_EXPERT, D), lambda s, ea, eb, cnt: (ea[block_of(s, 1)], 0, 0))
    down_b = pl.BlockSpec((1, D_EXPERT, D), lambda s, ea, eb, cnt: (eb[block_of(s, 1)], 0, 0))
    grid_spec = pltpu.PrefetchScalarGridSpec(
        num_scalar_prefetch=3,
        grid=(N_MOE_BLOCKS + 2,),
        in_specs=[src_spec, dst_spec, pl.BlockSpec(memory_space=pl.ANY),
                  up_a, up_a, down_a, up_b, up_b, down_b],
        out_specs=pl.BlockSpec(memory_space=pl.ANY),
        scratch_shapes=[pltpu.VMEM((2, MOE_ROWS * ROUTED, LANES), F32),
                        pltpu.VMEM((2, MOE_ROWS * ROW_TILE, LANES), F32),
                        pltpu.SemaphoreType.DMA((2,)), pltpu.SemaphoreType.DMA((2,))],
    )
    return pl.pallas_call(
        _moe_body, name="moe",
        grid_spec=grid_spec,
        out_shape=jax.ShapeDtypeStruct(((T_ALL + 2 * MOE_ROWS) * ROW_TILE, LANES), F32),
        compiler_params=pltpu.CompilerParams(dimension_semantics=("arbitrary",)),
    )(ea, eb, cnt, src, dst, rows, w_gate, w_up, w_down, w_gate, w_up, w_down)


def _l1_body(x_ref, xprev_ref, xnext_ref, y_ref, yprev_ref, ynext_ref, gm_ref, wi_ref, cw_ref, wo_ref,
             g_ref, wr_ref, br_ref, x_out_ref, row_ref, meta_ref, buf, ybuf):
    starts, ends = _seq_edges(pl.program_id(0))
    x = x_ref[...] + _read_row_tiles(y_ref, TB)
    x_edge = jnp.concatenate([xprev_ref[...] + _read_row_tiles(yprev_ref, EDGE),
                              xnext_ref[...] + _read_row_tiles(ynext_ref, EDGE)], axis=0)
    h = jnp.concatenate([_rms(x, gm_ref[...]), _rms(x_edge, gm_ref[...])], axis=0).astype(BF16)
    u = _dot(h, wi_ref[...])
    cv = u[:, D:2 * D] * u[:, 2 * D:]
    buf[HALO - EDGE:HALO, :] = jnp.where(starts, 0.0, cv[TB:TB + EDGE, :])
    buf[HALO:HALO + TB, :] = cv[:TB, :]
    buf[HALO + TB:HALO + TB + EDGE, :] = jnp.where(ends, 0.0, cv[TB + EDGE:, :])
    rows = CONV_ROWS // 2
    for r0 in range(0, TB, rows):
        conv = (cw_ref[0:1, :] * buf[r0 + HALO - 1:r0 + HALO - 1 + rows, :]
                + cw_ref[1:2, :] * buf[r0 + HALO:r0 + HALO + rows, :]
                + cw_ref[2:3, :] * buf[r0 + HALO + 1:r0 + HALO + 1 + rows, :])
        ybuf[r0:r0 + rows, :] = (u[r0:r0 + rows, :D] * conv).astype(BF16)
    _route_and_store(x + _dot(ybuf[...], wo_ref[...]), g_ref, wr_ref, br_ref, x_out_ref, row_ref, meta_ref)


def _l1(x, y, g_mix, w_in, conv_w, w_out, g_ffn, wr, br):
    per_tile = TB // EDGE
    last = T_ALL // EDGE - 1

    def prev_of(i):
        return jnp.maximum(i * per_tile - 1, 0)

    def next_of(i):
        return jnp.minimum((i + 1) * per_tile, last)

    def resident(shape):
        return pl.BlockSpec(shape, lambda *_: (0,) * len(shape), pipeline_mode=pl.Buffered(1))

    return pl.pallas_call(
        _l1_body, name="l1_mixer",
        grid=(N_TILES,),
        in_specs=[_tile_spec(D),
                  pl.BlockSpec((EDGE, D), lambda i: (prev_of(i), 0)),
                  pl.BlockSpec((EDGE, D), lambda i: (next_of(i), 0)),
                  _row_tiles_spec(),
                  pl.BlockSpec((EDGE * ROW_TILE, LANES), lambda i: (prev_of(i), 0)),
                  pl.BlockSpec((EDGE * ROW_TILE, LANES), lambda i: (next_of(i), 0)),
                  _full_spec((1, D)), resident((D, 3 * D)), _full_spec((8, D)), resident((D, D)),
                  _full_spec((1, D)), _full_spec((D, 2 * META)), _full_spec((1, META))],
        out_specs=[_tile_spec(D), _row_tiles_spec(ROUTED), _tile_spec(META)],
        out_shape=[jax.ShapeDtypeStruct((T_ALL, D), F32), jax.ShapeDtypeStruct((T_ALL * ROUTED, LANES), F32),
                   jax.ShapeDtypeStruct((T_ALL, META), F32)],
        scratch_shapes=[pltpu.VMEM((TB + 2 * HALO, D), F32), pltpu.VMEM((TB, D), BF16)],
        compiler_params=pltpu.CompilerParams(vmem_limit_bytes=56 * 1024 * 1024),
    )(x, x, x, y, y, y, g_mix, w_in, conv_w, w_out, g_ffn, wr, br)


def _final_body(x_ref, y_ref, g_ref, o_ref):
    o_ref[...] = _rms(x_ref[...] + _read_row_tiles(y_ref, TB), g_ref[...])


def _final(x, y, g, *, tile0, tiles):
    in_tile = pl.BlockSpec((TB, D), lambda i: (i + tile0, 0))
    y_tile = pl.BlockSpec((TB * ROW_TILE, LANES), lambda i: (i + tile0, 0))
    return pl.pallas_call(
        _final_body, name="final_norm",
        grid=(tiles,),
        in_specs=[in_tile, y_tile, _full_spec((1, D))],
        out_specs=_tile_spec(D),
        out_shape=jax.ShapeDtypeStruct((tiles * TB, D), F32),
    )(x, y, g)


def kernel(x_prompt, x_sample, l0_norm_mix, l0_w_in, l0_conv_w, l0_conv_b, l0_ln_g, l0_ln_b, l0_w_out, l0_norm_ffn, l0_w_coarse, l0_b_coarse, l0_w_fine, l0_b_fine, l0_w_gate, l0_w_up, l0_w_down, l1_norm_mix, l1_w_in, l1_conv_w, l1_w_out, l1_norm_ffn, l1_w_coarse, l1_b_coarse, l1_w_fine, l1_b_fine, l1_w_gate, l1_w_up, l1_w_down, final_norm):
    xp = x_prompt.reshape(T_PROMPT, D)
    xs = x_sample.reshape(T_SAMPLE, D)
    row = lambda v: v.reshape(1, -1)

    a, f = _l0_in(xp, xs, row(l0_norm_mix), l0_w_in.astype(BF16))
    conv_w = jnp.pad(l0_conv_w, ((0, 1), (0, 0)))
    d1, l2 = _dft_prompt_tables()
    cs_prompt = _dft2_prompt(_dft1_prompt(f, d1), l2)
    cs_sample = _dft_sample(f, _dft_sample_table())
    w_cs = _fold_channel_dft(l0_w_out[HALF:])
    wr0, br0 = _router_operands(l0_w_coarse, l0_b_coarse, l0_w_fine, l0_b_fine)
    x1, rows0, meta0 = _l0_out(a, conv_w, row(l0_conv_b), row(l0_ln_g), row(l0_ln_b), cs_prompt, cs_sample, xp, xs,
                               l0_w_out[:HALF].astype(BF16), w_cs[:HALF], w_cs[HALF:], row(l0_norm_ffn), wr0, br0)

    y0 = _moe(rows0, _moe_plan(meta0), l0_w_gate.astype(BF16), l0_w_up.astype(BF16), l0_w_down.astype(BF16))

    wr1, br1 = _router_operands(l1_w_coarse, l1_b_coarse, l1_w_fine, l1_b_fine)
    x3, rows1, meta1 = _l1(x1, y0, row(l1_norm_mix), l1_w_in.astype(BF16), jnp.pad(l1_conv_w, ((0, 5), (0, 0))),
                           l1_w_out.astype(BF16), row(l1_norm_ffn), wr1, br1)

    y1 = _moe(rows1, _moe_plan(meta1), l1_w_gate.astype(BF16), l1_w_up.astype(BF16), l1_w_down.astype(BF16))

    out_p = _final(x3, y1, row(final_norm), tile0=0, tiles=N_PROMPT_TILES)
    out_s = _final(x3, y1, row(final_norm), tile0=N_PROMPT_TILES, tiles=N_TILES - N_PROMPT_TILES)
    return out_p.reshape(x_prompt.shape), out_s.reshape(x_sample.shape)
```

```python
import math

import jax
import jax.numpy as jnp
import numpy as np
from jax import lax
from jax.experimental import pallas as pl
from jax.experimental.pallas import tpu as pltpu

D = 1024
LANES = 128
T_PROMPT = 16384
SEQ_SAMPLE = 2048
T_SAMPLE = 32 * SEQ_SAMPLE
T_ALL = T_PROMPT + T_SAMPLE
HALF = 512
GROUP_CH = 64
CONV_W = 31
N_GROUPS = 4
EPG = 8
N_PAIRS = EPG * (EPG - 1) // 2
N_BUCKETS = N_GROUPS * N_PAIRS
D_EXPERT = 512
RMS_EPS = 1e-6
LN_EPS = 1e-5

TB = 512
N_TILES = T_ALL // TB
N_PROMPT_TILES = T_PROMPT // TB
HALO = 16
EDGE = 8
CONV_ROWS = 64
MOE_ROWS = 256
N_MOE_BLOCKS = T_ALL // MOE_ROWS + N_BUCKETS + 1
META = 128
ROW_TILE = D // LANES
ROUTED = ROW_TILE + 1
FFT_N2 = 128
FFT_N1_PROMPT = T_PROMPT // FFT_N2
FFT_G1 = 16
FFT_KB = 8
DFT_ROWS = 1024

BF16 = jnp.bfloat16
F32 = jnp.float32


def _rms(x, g):
    return x * lax.rsqrt(jnp.mean(x * x, axis=-1, keepdims=True) + RMS_EPS) * g


def _dot(a, b):
    return jnp.dot(a, b, preferred_element_type=F32)


def _seq_edges(i):
    r0 = i * TB
    r1 = r0 + TB
    in_prompt = r0 < T_PROMPT
    starts = jnp.where(in_prompt, r0 == 0, (r0 - T_PROMPT) % SEQ_SAMPLE == 0)
    ends = jnp.where(in_prompt, r1 == T_PROMPT, (r1 - T_PROMPT) % SEQ_SAMPLE == 0)
    return starts, ends


def _tile_spec(width):
    return pl.BlockSpec((TB, width), lambda i: (i, 0))


def _row_tiles_spec(rows_per_token=ROW_TILE):
    return pl.BlockSpec((TB * rows_per_token, LANES), lambda i: (i, 0))


def _read_row_tiles(ref, tokens):
    return jnp.concatenate([ref[pl.ds(k, tokens, stride=ROW_TILE), :] for k in range(ROW_TILE)], axis=1)


def _full_spec(shape):
    return pl.BlockSpec(shape, lambda *_: (0,) * len(shape))


def _prompt_spec(width=D):
    return pl.BlockSpec((TB, width), lambda i: (jnp.minimum(i, N_PROMPT_TILES - 1), 0))


def _sample_spec(width=D):
    return pl.BlockSpec((TB, width), lambda i: (jnp.maximum(i - N_PROMPT_TILES, 0), 0))


def _halo_specs(width):
    per_tile = TB // HALO
    last = T_ALL // HALO - 1
    prev = pl.BlockSpec((HALO, width), lambda i: (jnp.maximum(i * per_tile - 1, 0), 0))
    nxt = pl.BlockSpec((HALO, width), lambda i: (jnp.minimum((i + 1) * per_tile, last), 0))
    return prev, nxt


def _l0_in_body(xp_ref, xs_ref, g_ref, w_ref, a_ref, f_ref):
    i = pl.program_id(0)
    x = jnp.where(i < N_PROMPT_TILES, xp_ref[...], xs_ref[...])
    h = _rms(x, g_ref[...])
    u = _dot(h.astype(BF16), w_ref[...])
    a_ref[...] = (u[:, :HALF] * jax.nn.sigmoid(u[:, HALF:2 * HALF])).astype(BF16)
    f_ref[...] = u[:, 2 * HALF:].astype(BF16)


def _l0_in(xp, xs, g, w_in):
    return pl.pallas_call(
        _l0_in_body, name="l0_in",
        grid=(N_TILES,),
        in_specs=[_prompt_spec(), _sample_spec(), _full_spec((1, D)), _full_spec((D, 3 * HALF))],
        out_specs=[_tile_spec(HALF), _tile_spec(HALF)],
        out_shape=[jax.ShapeDtypeStruct((T_ALL, HALF), BF16), jax.ShapeDtypeStruct((T_ALL, HALF), BF16)],
    )(xp, xs, g, w_in)


def _conv_module(prev_ref, cur_ref, next_ref, w_ref, b_ref, g_ref, beta_ref, o_ref, buf):
    starts, ends = _seq_edges(pl.program_id(0))
    buf[0, 0:HALO, :] = jnp.where(starts, 0.0, prev_ref[...].astype(F32))
    buf[0, HALO:HALO + TB, :] = cur_ref[...].astype(F32)
    buf[0, HALO + TB:, :] = jnp.where(ends, 0.0, next_ref[...].astype(F32))
    rows = TB + 2 * HALO - 8
    for s in range(1, 8):
        for c0 in range(0, rows, 128):
            c1 = min(c0 + 128, rows)
            buf[s, c0:c1, :] = buf[0, c0 + s:c1 + s, :]
    first = HALO - CONV_W // 2
    for r0 in range(0, TB, CONV_ROWS):
        acc = jnp.broadcast_to(b_ref[...], (CONV_ROWS, HALF))
        for k in range(CONV_W):
            s = (first + k) % 8
            base = r0 + first + k - s
            acc = acc + w_ref[k:k + 1, :] * buf[s, base:base + CONV_ROWS, :]
        mu = jnp.mean(acc, axis=-1, keepdims=True)
        xc = acc - mu
        var = jnp.mean(xc * xc, axis=-1, keepdims=True)
        y = xc * lax.rsqrt(var + LN_EPS) * g_ref[...] + beta_ref[...]
        o_ref[r0:r0 + CONV_ROWS, :] = (y * jax.nn.sigmoid(y)).astype(BF16)


def _dft1_prompt_body(d_ref, x_ref, y_ref, xs, ys):
    n1 = FFT_N1_PROMPT
    x = x_ref[0].reshape(n1 * FFT_G1, HALF).astype(F32)
    for c in range(HALF // LANES):
        xs[c] = x[:, c * LANES:(c + 1) * LANES]
    for q in range(FFT_G1):
        rows = pl.ds(q, n1, stride=FFT_G1)
        xq = jnp.concatenate([xs[c, rows, :] for c in range(HALF // LANES)], axis=1)
        y = _dot(d_ref[...], xq.astype(BF16))
        for c in range(HALF // LANES):
            ys[0, c, rows, :] = y[:n1, c * LANES:(c + 1) * LANES]
            ys[1, c, rows, :] = y[n1:, c * LANES:(c + 1) * LANES]
    for p in range(2):
        for c in range(HALF // LANES):
            y_ref[p, :, :, c * LANES:(c + 1) * LANES] = ys[p, c].reshape(n1, FFT_G1, LANES)


def _dft1_prompt(f_all, d1):
    n1 = FFT_N1_PROMPT
    x = f_all.reshape(T_ALL // T_PROMPT, n1, FFT_N2, HALF)
    return pl.pallas_call(
        _dft1_prompt_body, name="dft1_prompt",
        grid=(FFT_N2 // FFT_G1,),
        in_specs=[_full_spec((2 * n1, n1)),
                  pl.BlockSpec((1, n1, FFT_G1, HALF), lambda j: (0, 0, j, 0))],
        out_specs=pl.BlockSpec((2, n1, FFT_G1, HALF), lambda j: (0, 0, j, 0)),
        out_shape=jax.ShapeDtypeStruct((2, n1, FFT_N2, HALF), F32),
        scratch_shapes=[pltpu.VMEM((HALF // LANES, n1 * FFT_G1, LANES), F32),
                        pltpu.VMEM((2, HALF // LANES, n1 * FFT_G1, LANES), F32)],
        compiler_params=pltpu.CompilerParams(vmem_limit_bytes=48 * 1024 * 1024),
    )(d1, x)


def _dft2_prompt_body(l_ref, y_ref, c_ref, s_ref, cs, ss):
    for q in range(FFT_KB):
        yq = y_ref[:, q].reshape(2 * FFT_N2, HALF).astype(BF16)
        r = _dot(l_ref[q], yq)
        rows = pl.ds(q, FFT_N2, stride=FFT_KB)
        for c in range(HALF // LANES):
            cs[c, rows, :] = r[:FFT_N2, c * LANES:(c + 1) * LANES]
            ss[c, rows, :] = r[FFT_N2:, c * LANES:(c + 1) * LANES]
    for c in range(HALF // LANES):
        c_ref[:, :, c * LANES:(c + 1) * LANES] = cs[c].reshape(FFT_N2, FFT_KB, LANES)
        s_ref[:, :, c * LANES:(c + 1) * LANES] = ss[c].reshape(FFT_N2, FFT_KB, LANES)


def _dft2_prompt(y, l2):
    n1 = FFT_N1_PROMPT
    out_sds = jax.ShapeDtypeStruct((FFT_N2, n1, HALF), F32)
    out_spec = pl.BlockSpec((FFT_N2, FFT_KB, HALF), lambda k: (0, k, 0))
    c, s = pl.pallas_call(
        _dft2_prompt_body, name="dft2_prompt",
        grid=(n1 // FFT_KB,),
        in_specs=[pl.BlockSpec((FFT_KB, 2 * FFT_N2, 2 * FFT_N2), lambda k: (k, 0, 0)),
                  pl.BlockSpec((2, FFT_KB, FFT_N2, HALF), lambda k: (0, k, 0, 0))],
        out_specs=[out_spec, out_spec],
        out_shape=[out_sds, out_sds],
        scratch_shapes=[pltpu.VMEM((HALF // LANES, FFT_N2 * FFT_KB, LANES), F32),
                        pltpu.VMEM((HALF // LANES, FFT_N2 * FFT_KB, LANES), F32)],
    )(l2, y)
    return c.reshape(T_PROMPT, HALF), s.reshape(T_PROMPT, HALF)


def _dft_sample_body(m_ref, x_ref, o_ref):
    for part in range(2):
        for r0 in range(0, SEQ_SAMPLE, DFT_ROWS):
            rows = m_ref[part * SEQ_SAMPLE + r0:part * SEQ_SAMPLE + r0 + DFT_ROWS, :]
            o_ref[part, r0:r0 + DFT_ROWS, :] = _dot(rows, x_ref[...]).astype(BF16)


def _dft_sample(f_all, m):
    first = T_PROMPT // SEQ_SAMPLE
    return pl.pallas_call(
        _dft_sample_body, name="dft_sample",
        grid=(T_SAMPLE // SEQ_SAMPLE,),
        in_specs=[pl.BlockSpec((2 * SEQ_SAMPLE, SEQ_SAMPLE), lambda b: (0, 0), pipeline_mode=pl.Buffered(1)),
                  pl.BlockSpec((SEQ_SAMPLE, HALF), lambda b: (first + b, 0))],
        out_specs=pl.BlockSpec((2, SEQ_SAMPLE, HALF), lambda b: (0, b, 0)),
        out_shape=jax.ShapeDtypeStruct((2, T_SAMPLE, HALF), BF16),
        compiler_params=pltpu.CompilerParams(vmem_limit_bytes=48 * 1024 * 1024),
    )(m, f_all)


def _dft_sample_table():
    k = jnp.arange(SEQ_SAMPLE, dtype=jnp.int32)
    ang = (2.0 * math.pi / SEQ_SAMPLE) * ((k[:, None] * k[None, :]) % SEQ_SAMPLE).astype(F32)
    scale = SEQ_SAMPLE ** -0.5
    return jnp.concatenate([jnp.cos(ang) * scale, jnp.sin(ang) * scale], axis=0).astype(BF16)


def _dft_prompt_tables():
    n1 = FFT_N1_PROMPT
    n = n1 * FFT_N2
    k1 = jnp.arange(n1, dtype=jnp.int32)
    ang1 = (2.0 * math.pi / n1) * ((k1[:, None] * k1[None, :]) % n1).astype(F32)
    d1 = jnp.concatenate([jnp.cos(ang1), -jnp.sin(ang1)], axis=0).astype(BF16)
    k2 = jnp.arange(FFT_N2, dtype=jnp.int32)
    k = k1[:, None, None] + n1 * k2[None, :, None]
    ang2 = (2.0 * math.pi / n) * ((k * k2[None, None, :]) % n).astype(F32)
    cr = jnp.cos(ang2) * (n ** -0.5)
    sr = jnp.sin(ang2) * (n ** -0.5)
    top = jnp.concatenate([cr, sr], axis=2)
    bot = jnp.concatenate([sr, -cr], axis=2)
    l2 = jnp.concatenate([top, bot], axis=1).astype(BF16)
    return d1, l2


def _fold_body(m_ref, w_ref, o_ref):
    o_ref[...] = jnp.dot(m_ref[...], w_ref[...], preferred_element_type=F32,
                         precision=lax.Precision.HIGHEST).astype(BF16)


def _fold_channel_dft(w_out_f):
    c = jnp.arange(HALF, dtype=jnp.int32)
    same = (c[:, None] // GROUP_CH) == (c[None, :] // GROUP_CH)
    ang = (2.0 * math.pi / GROUP_CH) * ((c[:, None] * c[None, :]) % GROUP_CH).astype(F32)
    scale = GROUP_CH ** -0.5
    cbd = jnp.where(same, jnp.cos(ang), 0.0) * scale
    sbd = jnp.where(same, jnp.sin(ang), 0.0) * scale
    m = jnp.concatenate([cbd, -sbd], axis=0)
    return pl.pallas_call(
        _fold_body, name="fold_channel_dft",
        out_shape=jax.ShapeDtypeStruct((2 * HALF, D), BF16),
    )(m, w_out_f)


def _route_and_store(x_new, g_ref, wr_ref, br_ref, x_out_ref, row_ref, meta_ref):
    x_out_ref[...] = x_new
    h = _rms(x_new, g_ref[...])
    h_hi = h.astype(BF16)
    h_lo = (h - h_hi.astype(F32)).astype(BF16)
    p = _dot(h_hi, wr_ref[...])
    q = _dot(h_lo, wr_ref[...])
    lg = p[:, :META] + p[:, META:] + q[:, :META] + q[:, META:] + br_ref[...]
    lane = lax.broadcasted_iota(jnp.int32, lg.shape, 1).astype(F32)
    neg = -jnp.inf
    lc = jnp.where(lane < N_GROUPS, lg, neg)
    mc = jnp.max(lc, axis=-1, keepdims=True)
    grp = jnp.min(jnp.where(lc == mc, lane, META), axis=-1, keepdims=True)
    p_grp = 1.0 / jnp.sum(jnp.exp(lc - mc), axis=-1, keepdims=True)
    lo = N_GROUPS + EPG * grp
    lf = jnp.where((lane >= lo) & (lane < lo + EPG), lg, neg)
    m1 = jnp.max(lf, axis=-1, keepdims=True)
    i1 = jnp.min(jnp.where(lf == m1, lane, META), axis=-1, keepdims=True)
    lf2 = jnp.where(lane == i1, neg, lf)
    m2 = jnp.max(lf2, axis=-1, keepdims=True)
    i2 = jnp.min(jnp.where(lf2 == m2, lane, META), axis=-1, keepdims=True)
    e = jnp.exp(m2 - m1)
    gate1 = p_grp / (1.0 + e)
    gate2 = p_grp * e / (1.0 + e)
    j1 = i1 - lo
    j2 = i2 - lo
    ja = jnp.minimum(j1, j2)
    jb = jnp.maximum(j1, j2)
    bucket = grp * N_PAIRS + (ja * (2 * EPG - 1 - ja)) * 0.5 + (jb - ja - 1)
    gate_a = jnp.where(j1 < j2, gate1, gate2)
    gate_b = jnp.where(j1 < j2, gate2, gate1)
    meta = jnp.where(lane == 0, gate_a,
                     jnp.where(lane == 1, gate_b,
                               jnp.where(lane == 2, bucket, 0.0)))
    for k in range(ROW_TILE):
        row_ref[pl.ds(k, TB, stride=ROUTED), :] = h[:, k * LANES:(k + 1) * LANES]
    row_ref[pl.ds(ROW_TILE, TB, stride=ROUTED), :] = meta
    meta_ref[...] = meta


def _router_operands(w_coarse, b_coarse, w_fine, b_fine):
    w = jnp.concatenate([w_coarse, w_fine.reshape(D, N_GROUPS * EPG)], axis=1)
    w = jnp.pad(w, ((0, 0), (0, META - w.shape[1])))
    w_hi = w.astype(BF16)
    w_lo = (w - w_hi.astype(F32)).astype(BF16)
    b = jnp.concatenate([b_coarse, b_fine.reshape(-1)])
    b = jnp.pad(b, (0, META - b.shape[0])).reshape(1, META)
    return jnp.concatenate([w_hi, w_lo], axis=1), b


def _l0_out_body(prev_ref, cur_ref, next_ref, cw_ref, cb_ref, lg_ref, lb_ref,
                 cp_ref, sp_ref, cs_ref, ss_ref, xp_ref, xs_ref, wa_ref, wc_ref, ws_ref,
                 g_ref, wr_ref, br_ref, x_out_ref, row_ref, meta_ref, buf, a2):
    in_prompt = pl.program_id(0) < N_PROMPT_TILES
    x = jnp.where(in_prompt, xp_ref[...], xs_ref[...])
    c = jnp.where(in_prompt, cp_ref[...].astype(BF16), cs_ref[...])
    s = jnp.where(in_prompt, sp_ref[...].astype(BF16), ss_ref[...])
    mix = _dot(c, wc_ref[...]) + _dot(s, ws_ref[...])
    _conv_module(prev_ref, cur_ref, next_ref, cw_ref, cb_ref, lg_ref, lb_ref, a2, buf)
    mix = mix + _dot(a2[...], wa_ref[...])
    _route_and_store(x + mix, g_ref, wr_ref, br_ref, x_out_ref, row_ref, meta_ref)


def _l0_out(a, conv_w, conv_b, ln_g, ln_b, cs_prompt, cs_sample, xp, xs, wa, wc, ws, g, wr, br):
    def sample_half(which):
        return pl.BlockSpec((None, TB, HALF), lambda i: (which, jnp.maximum(i - N_PROMPT_TILES, 0), 0))
    prev, nxt = _halo_specs(HALF)
    return pl.pallas_call(
        _l0_out_body, name="l0_out",
        grid=(N_TILES,),
        in_specs=[prev, _tile_spec(HALF), nxt, _full_spec((CONV_W + 1, HALF)),
                  _full_spec((1, HALF)), _full_spec((1, HALF)), _full_spec((1, HALF)),
                  _prompt_spec(HALF), _prompt_spec(HALF), sample_half(0), sample_half(1),
                  _prompt_spec(), _sample_spec(),
                  _full_spec((HALF, D)), _full_spec((HALF, D)), _full_spec((HALF, D)),
                  _full_spec((1, D)), _full_spec((D, 2 * META)), _full_spec((1, META))],
        out_specs=[_tile_spec(D), _row_tiles_spec(ROUTED), _tile_spec(META)],
        out_shape=[jax.ShapeDtypeStruct((T_ALL, D), F32), jax.ShapeDtypeStruct((T_ALL * ROUTED, LANES), F32),
                   jax.ShapeDtypeStruct((T_ALL, META), F32)],
        scratch_shapes=[pltpu.VMEM((8, TB + 2 * HALO, HALF), F32), pltpu.VMEM((TB, HALF), BF16)],
    )(a, a, a, conv_w, conv_b, ln_g, ln_b, *cs_prompt, cs_sample, cs_sample, xp, xs, wa, wc, ws, g, wr, br)


def _moe_plan(meta):
    bucket = meta[:, 2].astype(jnp.int32)
    counts = jnp.sum(bucket[None, :] == jnp.arange(N_BUCKETS, dtype=jnp.int32)[:, None], axis=1).astype(jnp.int32)
    nblk = (counts + MOE_ROWS - 1) // MOE_ROWS
    fill = jnp.cumsum(nblk * MOE_ROWS - counts)
    slots = N_MOE_BLOCKS * MOE_ROWS
    pad_id = jnp.arange(slots - T_ALL, dtype=jnp.int32)
    pad_key = jnp.sum(fill[None, :] <= pad_id[:, None], axis=1).astype(jnp.int32)
    keys, ids = lax.sort((jnp.concatenate([bucket, pad_key]), jnp.arange(slots, dtype=jnp.int32)),
                         num_keys=1, is_stable=True)
    keys = keys.reshape(N_MOE_BLOCKS, MOE_ROWS)
    ids = ids.reshape(N_MOE_BLOCKS, MOE_ROWS)
    real = ids < T_ALL
    cnt = jnp.sum(real, axis=1).astype(jnp.int32)
    j = jnp.arange(N_MOE_BLOCKS, dtype=jnp.int32)
    r = jnp.arange(MOE_ROWS, dtype=jnp.int32)[None, :]
    spare = T_ALL + (j % 2)[:, None] * MOE_ROWS + r
    src = (jnp.where(real, ids, ids - T_ALL) * ROUTED).reshape(N_MOE_BLOCKS, 1, MOE_ROWS)
    dst = (jnp.where(real, ids, spare) * ROW_TILE).reshape(N_MOE_BLOCKS, 1, MOE_ROWS)
    bj = keys[:, 0]
    last_used = lax.dynamic_index_in_dim(bj, jnp.sum(nblk) - 1, keepdims=False)
    bj = jnp.where(bj < N_BUCKETS, bj, last_used)
    pa, pb = np.triu_indices(EPG, k=1)
    grp = bj // N_PAIRS
    ea = grp * EPG + jnp.asarray(pa, jnp.int32)[bj % N_PAIRS]
    eb = grp * EPG + jnp.asarray(pb, jnp.int32)[bj % N_PAIRS]
    return ea.astype(jnp.int32), eb.astype(jnp.int32), cnt, src, dst


def _moe_body(ea_ref, eb_ref, cnt_ref, src_ref, dst_ref, rows_hbm,
              wga_ref, wua_ref, wda_ref, wgb_ref, wub_ref, wdb_ref,
              y_hbm, xbuf, ybuf, gsem, ssem):
    s = pl.program_id(0)
    tile = ROW_TILE
    buf_rows = MOE_ROWS * tile

    def rows_in(b):
        inside = (b >= 0) & (b < N_MOE_BLOCKS)
        return jnp.where(inside, cnt_ref[jnp.clip(b, 0, N_MOE_BLOCKS - 1)], 0)

    def wait_fetch(p):
        pltpu.make_async_copy(rows_hbm.at[pl.ds(0, MOE_ROWS * ROUTED)], xbuf.at[p], gsem.at[p]).wait()

    def wait_send(p):
        pltpu.make_async_copy(ybuf.at[p], y_hbm.at[pl.ds(0, buf_rows)], ssem.at[p]).wait()

    @pl.when(s == 0)
    def _():
        ybuf[...] = jnp.zeros(ybuf.shape, ybuf.dtype)
        for p in range(2):
            clear = pltpu.make_async_copy(ybuf.at[p], y_hbm.at[pl.ds((T_ALL + p * MOE_ROWS) * tile, buf_rows)],
                                          ssem.at[p])
            clear.start()
            clear.wait()

    for p in range(2):
        q = 1 - p
        mine = (s % 2) == p

        @pl.when(mine & (rows_in(s - 3) > 0))
        def _():
            wait_send(q)

        @pl.when(mine & (rows_in(s) > 0))
        def _():
            for r in range(MOE_ROWS):
                pltpu.make_async_copy(rows_hbm.at[pl.ds(src_ref[0, 0, r], ROUTED)],
                                      xbuf.at[p, pl.ds(r * ROUTED, ROUTED)], gsem.at[p]).start(priority=r % 2)

        @pl.when(mine & (rows_in(s - 1) > 0))
        def _():
            wait_fetch(q)
            x = jnp.concatenate([xbuf[q, pl.ds(k, MOE_ROWS, stride=ROUTED), :] for k in range(tile)],
                                axis=1).astype(BF16)
            record = xbuf[q, pl.ds(tile, MOE_ROWS, stride=ROUTED), :]
            gate_a = record[:, 0:1]
            gate_b = record[:, 1:2]

            def hidden(wg_ref, wu_ref, gate):
                g = _dot(x, wg_ref[0])
                u = _dot(x, wu_ref[0])
                return (g * jax.nn.sigmoid(g) * u * gate).astype(BF16)

            y = (_dot(hidden(wga_ref, wua_ref, gate_a), wda_ref[0])
                 + _dot(hidden(wgb_ref, wub_ref, gate_b), wdb_ref[0]))
            for k in range(tile):
                ybuf[q, pl.ds(k, MOE_ROWS, stride=tile), :] = y[:, k * LANES:(k + 1) * LANES]
            for r in range(MOE_ROWS):
                start = pl.multiple_of(dst_ref[0, 0, r], tile)
                pltpu.make_async_copy(ybuf.at[q, pl.ds(r * tile, tile)], y_hbm.at[pl.ds(start, tile)],
                                      ssem.at[q]).start(priority=r % 2)


def _moe(rows, plan, w_gate, w_up, w_down):
    ea, eb, cnt, src, dst = plan
    last = N_MOE_BLOCKS - 1

    def block_of(step, lag):
        return jnp.clip(step - lag, 0, last)

    src_spec = pl.BlockSpec((1, 1, MOE_ROWS), lambda s, *_: (block_of(s, 0), 0, 0), memory_space=pltpu.SMEM)
    dst_spec = pl.BlockSpec((1, 1, MOE_ROWS), lambda s, *_: (block_of(s, 1), 0, 0), memory_space=pltpu.SMEM)
    up_a = pl.BlockSpec((1, D, D_EXPERT), lambda s, ea, eb, cnt: (ea[block_of(s, 1)], 0, 0))
    up_b = pl.BlockSpec((1, D, D_EXPERT), lambda s, ea, eb, cnt: (eb[block_of(s, 1)], 0, 0))
    down_a = pl.BlockSpec((1, D_EXPERT, D), lambda s, ea, eb, cnt: (ea[block_of(s, 1)], 0, 0))
    down_b = pl.BlockSpec((1, D_EXPERT, D), lambda s, ea, eb, cnt: (eb[block_of(s, 1)], 0, 0))
    grid_spec = pltpu.PrefetchScalarGridSpec(
        num_scalar_prefetch=3,
        grid=(N_MOE_BLOCKS + 2,),
        in_specs=[src_spec, dst_spec, pl.BlockSpec(memory_space=pl.ANY),
                  up_a, up_a, down_a, up_b, up_b, down_b],
        out_specs=pl.BlockSpec(memory_space=pl.ANY),
        scratch_shapes=[pltpu.VMEM((2, MOE_ROWS * ROUTED, LANES), F32),
                        pltpu.VMEM((2, MOE_ROWS * ROW_TILE, LANES), F32),
                        pltpu.SemaphoreType.DMA((2,)), pltpu.SemaphoreType.DMA((2,))],
    )
    return pl.pallas_call(
        _moe_body, name="moe",
        grid_spec=grid_spec,
        out_shape=jax.ShapeDtypeStruct(((T_ALL + 2 * MOE_ROWS) * ROW_TILE, LANES), F32),
        compiler_params=pltpu.CompilerParams(dimension_semantics=("arbitrary",)),
    )(ea, eb, cnt, src, dst, rows, w_gate, w_up, w_down, w_gate, w_up, w_down)


def _l1_body(x_ref, xprev_ref, xnext_ref, y_ref, yprev_ref, ynext_ref, gm_ref, wi_ref, cw_ref, wo_ref,
             g_ref, wr_ref, br_ref, x_out_ref, row_ref, meta_ref, buf, ybuf):
    starts, ends = _seq_edges(pl.program_id(0))
    x = x_ref[...] + _read_row_tiles(y_ref, TB)
    x_edge = jnp.concatenate([xprev_ref[...] + _read_row_tiles(yprev_ref, EDGE),
                              xnext_ref[...] + _read_row_tiles(ynext_ref, EDGE)], axis=0)
    h = jnp.concatenate([_rms(x, gm_ref[...]), _rms(x_edge, gm_ref[...])], axis=0).astype(BF16)
    u = _dot(h, wi_ref[...])
    cv = u[:, D:2 * D] * u[:, 2 * D:]
    buf[HALO - EDGE:HALO, :] = jnp.where(starts, 0.0, cv[TB:TB + EDGE, :])
    buf[HALO:HALO + TB, :] = cv[:TB, :]
    buf[HALO + TB:HALO + TB + EDGE, :] = jnp.where(ends, 0.0, cv[TB + EDGE:, :])
    rows = CONV_ROWS // 2
    for r0 in range(0, TB, rows):
        conv = (cw_ref[0:1, :] * buf[r0 + HALO - 1:r0 + HALO - 1 + rows, :]
                + cw_ref[1:2, :] * buf[r0 + HALO:r0 + HALO + rows, :]
                + cw_ref[2:3, :] * buf[r0 + HALO + 1:r0 + HALO + 1 + rows, :])
        ybuf[r0:r0 + rows, :] = (u[r0:r0 + rows, :D] * conv).astype(BF16)
    _route_and_store(x + _dot(ybuf[...], wo_ref[...]), g_ref, wr_ref, br_ref, x_out_ref, row_ref, meta_ref)


def _l1(x, y, g_mix, w_in, conv_w, w_out, g_ffn, wr, br):
    per_tile = TB // EDGE
    last = T_ALL // EDGE - 1

    def prev_of(i):
        return jnp.maximum(i * per_tile - 1, 0)

    def next_of(i):
        return jnp.minimum((i + 1) * per_tile, last)

    def resident(shape):
        return pl.BlockSpec(shape, lambda *_: (0,) * len(shape), pipeline_mode=pl.Buffered(1))

    return pl.pallas_call(
        _l1_body, name="l1_mixer",
        grid=(N_TILES,),
        in_specs=[_tile_spec(D),
                  pl.BlockSpec((EDGE, D), lambda i: (prev_of(i), 0)),
                  pl.BlockSpec((EDGE, D), lambda i: (next_of(i), 0)),
                  _row_tiles_spec(),
                  pl.BlockSpec((EDGE * ROW_TILE, LANES), lambda i: (prev_of(i), 0)),
                  pl.BlockSpec((EDGE * ROW_TILE, LANES), lambda i: (next_of(i), 0)),
                  _full_spec((1, D)), resident((D, 3 * D)), _full_spec((8, D)), resident((D, D)),
                  _full_spec((1, D)), _full_spec((D, 2 * META)), _full_spec((1, META))],
        out_specs=[_tile_spec(D), _row_tiles_spec(ROUTED), _tile_spec(META)],
        out_shape=[jax.ShapeDtypeStruct((T_ALL, D), F32), jax.ShapeDtypeStruct((T_ALL * ROUTED, LANES), F32),
                   jax.ShapeDtypeStruct((T_ALL, META), F32)],
        scratch_shapes=[pltpu.VMEM((TB + 2 * HALO, D), F32), pltpu.VMEM((TB, D), BF16)],
        compiler_params=pltpu.CompilerParams(vmem_limit_bytes=56 * 1024 * 1024),
    )(x, x, x, y, y, y, g_mix, w_in, conv_w, w_out, g_ffn, wr, br)


def _final_body(x_ref, y_ref, g_ref, o_ref):
    o_ref[...] = _rms(x_ref[...] + _read_row_tiles(y_ref, TB), g_ref[...])


def _final(x, y, g, *, tile0, tiles):
    in_tile = pl.BlockSpec((TB, D), lambda i: (i + tile0, 0))
    y_tile = pl.BlockSpec((TB * ROW_TILE, LANES), lambda i: (i + tile0, 0))
    return pl.pallas_call(
        _final_body, name="final_norm",
        grid=(tiles,),
        in_specs=[in_tile, y_tile, _full_spec((1, D))],
        out_specs=_tile_spec(D),
        out_shape=jax.ShapeDtypeStruct((tiles * TB, D), F32),
    )(x, y, g)


def kernel(x_prompt, x_sample, l0_norm_mix, l0_w_in, l0_conv_w, l0_conv_b, l0_ln_g, l0_ln_b, l0_w_out, l0_norm_ffn, l0_w_coarse, l0_b_coarse, l0_w_fine, l0_b_fine, l0_w_gate, l0_w_up, l0_w_down, l1_norm_mix, l1_w_in, l1_conv_w, l1_w_out, l1_norm_ffn, l1_w_coarse, l1_b_coarse, l1_w_fine, l1_b_fine, l1_w_gate, l1_w_up, l1_w_down, final_norm):
    xp = x_prompt.reshape(T_PROMPT, D)
    xs = x_sample.reshape(T_SAMPLE, D)
    row = lambda v: v.reshape(1, -1)

    a, f = _l0_in(xp, xs, row(l0_norm_mix), l0_w_in.astype(BF16))
    conv_w = jnp.pad(l0_conv_w, ((0, 1), (0, 0)))
    d1, l2 = _dft_prompt_tables()
    cs_prompt = _dft2_prompt(_dft1_prompt(f, d1), l2)
    cs_sample = _dft_sample(f, _dft_sample_table())
    w_cs = _fold_channel_dft(l0_w_out[HALF:])
    wr0, br0 = _router_operands(l0_w_coarse, l0_b_coarse, l0_w_fine, l0_b_fine)
    x1, rows0, meta0 = _l0_out(a, conv_w, row(l0_conv_b), row(l0_ln_g), row(l0_ln_b), cs_prompt, cs_sample, xp, xs,
                               l0_w_out[:HALF].astype(BF16), w_cs[:HALF], w_cs[HALF:], row(l0_norm_ffn), wr0, br0)

    y0 = _moe(rows0, _moe_plan(meta0), l0_w_gate.astype(BF16), l0_w_up.astype(BF16), l0_w_down.astype(BF16))

    wr1, br1 = _router_operands(l1_w_coarse, l1_b_coarse, l1_w_fine, l1_b_fine)
    x3, rows1, meta1 = _l1(x1, y0, row(l1_norm_mix), l1_w_in.astype(BF16), jnp.pad(l1_conv_w, ((0, 5), (0, 0))),
                           l1_w_out.astype(BF16), row(l1_norm_ffn), wr1, br1)

    y1 = _moe(rows1, _moe_plan(meta1), l1_w_gate.astype(BF16), l1_w_up.astype(BF16), l1_w_down.astype(BF16))

    out_p = _final(x3, y1, row(final_norm), tile0=0, tiles=N_PROMPT_TILES)
    out_s = _final(x3, y1, row(final_norm), tile0=N_PROMPT_TILES, tiles=N_TILES - N_PROMPT_TILES)
    return out_p.reshape(x_prompt.shape), out_s.reshape(x_sample.shape)
```

```python
import math

import jax
import jax.numpy as jnp
import numpy as np
from jax import lax
from jax.experimental import pallas as pl
from jax.experimental.pallas import tpu as pltpu

D = 1024
LANES = 128
T_PROMPT = 16384
SEQ_SAMPLE = 2048
T_SAMPLE = 32 * SEQ_SAMPLE
T_ALL = T_PROMPT + T_SAMPLE
HALF = 512
GROUP_CH = 64
CONV_W = 31
N_GROUPS = 4
EPG = 8
N_PAIRS = EPG * (EPG - 1) // 2
N_BUCKETS = N_GROUPS * N_PAIRS
D_EXPERT = 512
RMS_EPS = 1e-6
LN_EPS = 1e-5

TB = 512
TB_IO = 1024
N_TILES = T_ALL // TB
N_PROMPT_TILES = T_PROMPT // TB
HALO = 16
EDGE = 8
CONV_ROWS = 64
MOE_ROWS = 256
N_MOE_BLOCKS = T_ALL // MOE_ROWS + N_BUCKETS + 1
META = 128
ROW_TILE = D // LANES
ROUTED = ROW_TILE + 1
FFT_N2 = 128
FFT_N1_PROMPT = T_PROMPT // FFT_N2
FFT_G1 = 16
FFT_KB = 8
DFT_ROWS = 1024

BF16 = jnp.bfloat16
F32 = jnp.float32


def _rms(x, g):
    return x * lax.rsqrt(jnp.mean(x * x, axis=-1, keepdims=True) + RMS_EPS) * g


def _dot(a, b):
    return jnp.dot(a, b, preferred_element_type=F32)


def _seq_edges(i):
    r0 = i * TB
    r1 = r0 + TB
    in_prompt = r0 < T_PROMPT
    starts = jnp.where(in_prompt, r0 == 0, (r0 - T_PROMPT) % SEQ_SAMPLE == 0)
    ends = jnp.where(in_prompt, r1 == T_PROMPT, (r1 - T_PROMPT) % SEQ_SAMPLE == 0)
    return starts, ends


def _tile_spec(width):
    return pl.BlockSpec((TB, width), lambda i: (i, 0))


def _row_tiles_spec(rows_per_token=ROW_TILE):
    return pl.BlockSpec((TB * rows_per_token, LANES), lambda i: (i, 0))


def _read_row_tiles(ref, tokens):
    return jnp.concatenate([ref[pl.ds(k, tokens, stride=ROW_TILE), :] for k in range(ROW_TILE)], axis=1)


def _full_spec(shape):
    return pl.BlockSpec(shape, lambda *_: (0,) * len(shape))


def _prompt_spec(width=D):
    return pl.BlockSpec((TB, width), lambda i: (jnp.minimum(i, N_PROMPT_TILES - 1), 0))


def _sample_spec(width=D):
    return pl.BlockSpec((TB, width), lambda i: (jnp.maximum(i - N_PROMPT_TILES, 0), 0))


def _halo_specs(width):
    per_tile = TB // HALO
    last = T_ALL // HALO - 1
    prev = pl.BlockSpec((HALO, width), lambda i: (jnp.maximum(i * per_tile - 1, 0), 0))
    nxt = pl.BlockSpec((HALO, width), lambda i: (jnp.minimum((i + 1) * per_tile, last), 0))
    return prev, nxt


def _l0_in_body(xp_ref, xs_ref, g_ref, w_ref, a_ref, f_ref):
    i = pl.program_id(0)
    x = jnp.where(i < T_PROMPT // TB_IO, xp_ref[...], xs_ref[...])
    h = _rms(x, g_ref[...])
    u = _dot(h.astype(BF16), w_ref[...])
    a_ref[...] = (u[:, :HALF] * jax.nn.sigmoid(u[:, HALF:2 * HALF])).astype(BF16)
    f_ref[...] = u[:, 2 * HALF:].astype(BF16)


def _l0_in(xp, xs, g, w_in):
    n_prompt = T_PROMPT // TB_IO
    half_tile = pl.BlockSpec((TB_IO, HALF), lambda i: (i, 0))
    return pl.pallas_call(
        _l0_in_body, name="l0_in",
        grid=(T_ALL // TB_IO,),
        in_specs=[pl.BlockSpec((TB_IO, D), lambda i: (jnp.minimum(i, n_prompt - 1), 0)),
                  pl.BlockSpec((TB_IO, D), lambda i: (jnp.maximum(i - n_prompt, 0), 0)),
                  _full_spec((1, D)), _full_spec((D, 3 * HALF))],
        out_specs=[half_tile, half_tile],
        out_shape=[jax.ShapeDtypeStruct((T_ALL, HALF), BF16), jax.ShapeDtypeStruct((T_ALL, HALF), BF16)],
    )(xp, xs, g, w_in)


def _conv_module(prev_ref, cur_ref, next_ref, w_ref, b_ref, g_ref, beta_ref, o_ref, buf):
    starts, ends = _seq_edges(pl.program_id(0))
    buf[0, 0:HALO, :] = jnp.where(starts, 0.0, prev_ref[...].astype(F32))
    buf[0, HALO:HALO + TB, :] = cur_ref[...].astype(F32)
    buf[0, HALO + TB:, :] = jnp.where(ends, 0.0, next_ref[...].astype(F32))
    rows = TB + 2 * HALO - 8
    for s in range(1, 8):
        for c0 in range(0, rows, 128):
            c1 = min(c0 + 128, rows)
            buf[s, c0:c1, :] = buf[0, c0 + s:c1 + s, :]
    first = HALO - CONV_W // 2
    for r0 in range(0, TB, CONV_ROWS):
        acc = jnp.broadcast_to(b_ref[...], (CONV_ROWS, HALF))
        for k in range(CONV_W):
            s = (first + k) % 8
            base = r0 + first + k - s
            acc = acc + w_ref[k:k + 1, :] * buf[s, base:base + CONV_ROWS, :]
        mu = jnp.mean(acc, axis=-1, keepdims=True)
        xc = acc - mu
        var = jnp.mean(xc * xc, axis=-1, keepdims=True)
        y = xc * lax.rsqrt(var + LN_EPS) * g_ref[...] + beta_ref[...]
        o_ref[r0:r0 + CONV_ROWS, :] = (y * jax.nn.sigmoid(y)).astype(BF16)


def _dft1_prompt_body(d_ref, x_ref, y_ref, xs, ys):
    n1 = FFT_N1_PROMPT
    x = x_ref[0].reshape(n1 * FFT_G1, HALF).astype(F32)
    for c in range(HALF // LANES):
        xs[c] = x[:, c * LANES:(c + 1) * LANES]
    for q in range(FFT_G1):
        rows = pl.ds(q, n1, stride=FFT_G1)
        xq = jnp.concatenate([xs[c, rows, :] for c in range(HALF // LANES)], axis=1)
        y = _dot(d_ref[...], xq.astype(BF16))
        for c in range(HALF // LANES):
            ys[0, c, rows, :] = y[:n1, c * LANES:(c + 1) * LANES]
            ys[1, c, rows, :] = y[n1:, c * LANES:(c + 1) * LANES]
    for p in range(2):
        for c in range(HALF // LANES):
            y_ref[p, :, :, c * LANES:(c + 1) * LANES] = ys[p, c].reshape(n1, FFT_G1, LANES)


def _dft1_prompt(f_all, d1):
    n1 = FFT_N1_PROMPT
    x = f_all.reshape(T_ALL // T_PROMPT, n1, FFT_N2, HALF)
    return pl.pallas_call(
        _dft1_prompt_body, name="dft1_prompt",
        grid=(FFT_N2 // FFT_G1,),
        in_specs=[_full_spec((2 * n1, n1)),
                  pl.BlockSpec((1, n1, FFT_G1, HALF), lambda j: (0, 0, j, 0))],
        out_specs=pl.BlockSpec((2, n1, FFT_G1, HALF), lambda j: (0, 0, j, 0)),
        out_shape=jax.ShapeDtypeStruct((2, n1, FFT_N2, HALF), F32),
        scratch_shapes=[pltpu.VMEM((HALF // LANES, n1 * FFT_G1, LANES), F32),
                        pltpu.VMEM((2, HALF // LANES, n1 * FFT_G1, LANES), F32)],
        compiler_params=pltpu.CompilerParams(vmem_limit_bytes=48 * 1024 * 1024),
    )(d1, x)


def _dft2_prompt_body(l_ref, y_ref, c_ref, s_ref, cs, ss):
    for q in range(FFT_KB):
        yq = y_ref[:, q].reshape(2 * FFT_N2, HALF).astype(BF16)
        r = _dot(l_ref[q], yq)
        rows = pl.ds(q, FFT_N2, stride=FFT_KB)
        for c in range(HALF // LANES):
            cs[c, rows, :] = r[:FFT_N2, c * LANES:(c + 1) * LANES]
            ss[c, rows, :] = r[FFT_N2:, c * LANES:(c + 1) * LANES]
    for c in range(HALF // LANES):
        c_ref[:, :, c * LANES:(c + 1) * LANES] = cs[c].reshape(FFT_N2, FFT_KB, LANES)
        s_ref[:, :, c * LANES:(c + 1) * LANES] = ss[c].reshape(FFT_N2, FFT_KB, LANES)


def _dft2_prompt(y, l2):
    n1 = FFT_N1_PROMPT
    out_sds = jax.ShapeDtypeStruct((FFT_N2, n1, HALF), F32)
    out_spec = pl.BlockSpec((FFT_N2, FFT_KB, HALF), lambda k: (0, k, 0))
    c, s = pl.pallas_call(
        _dft2_prompt_body, name="dft2_prompt",
        grid=(n1 // FFT_KB,),
        in_specs=[pl.BlockSpec((FFT_KB, 2 * FFT_N2, 2 * FFT_N2), lambda k: (k, 0, 0)),
                  pl.BlockSpec((2, FFT_KB, FFT_N2, HALF), lambda k: (0, k, 0, 0))],
        out_specs=[out_spec, out_spec],
        out_shape=[out_sds, out_sds],
        scratch_shapes=[pltpu.VMEM((HALF // LANES, FFT_N2 * FFT_KB, LANES), F32),
                        pltpu.VMEM((HALF // LANES, FFT_N2 * FFT_KB, LANES), F32)],
    )(l2, y)
    return c.reshape(T_PROMPT, HALF), s.reshape(T_PROMPT, HALF)


def _dft_sample_body(m_ref, x_ref, o_ref):
    for part in range(2):
        for r0 in range(0, SEQ_SAMPLE, DFT_ROWS):
            rows = m_ref[part * SEQ_SAMPLE + r0:part * SEQ_SAMPLE + r0 + DFT_ROWS, :]
            o_ref[part, r0:r0 + DFT_ROWS, :] = _dot(rows, x_ref[...]).astype(BF16)


def _dft_sample(f_all, m):
    first = T_PROMPT // SEQ_SAMPLE
    return pl.pallas_call(
        _dft_sample_body, name="dft_sample",
        grid=(T_SAMPLE // SEQ_SAMPLE,),
        in_specs=[pl.BlockSpec((2 * SEQ_SAMPLE, SEQ_SAMPLE), lambda b: (0, 0), pipeline_mode=pl.Buffered(1)),
                  pl.BlockSpec((SEQ_SAMPLE, HALF), lambda b: (first + b, 0))],
        out_specs=pl.BlockSpec((2, SEQ_SAMPLE, HALF), lambda b: (0, b, 0)),
        out_shape=jax.ShapeDtypeStruct((2, T_SAMPLE, HALF), BF16),
        compiler_params=pltpu.CompilerParams(vmem_limit_bytes=48 * 1024 * 1024),
    )(m, f_all)


def _cos_sin(num, den):
    ang = (2.0 * math.pi / den) * (num % den).astype(F32)
    return jnp.cos(ang), jnp.sin(ang)


def _dft_sample_table():
    n = jnp.arange(SEQ_SAMPLE, dtype=jnp.int32)
    lo = 64
    hi = SEQ_SAMPLE // lo
    ch, sh = _cos_sin(lo * jnp.arange(hi, dtype=jnp.int32)[:, None] * n[None, :], SEQ_SAMPLE)
    cl, sl = _cos_sin(jnp.arange(lo, dtype=jnp.int32)[:, None] * n[None, :], SEQ_SAMPLE)
    scale = SEQ_SAMPLE ** -0.5
    cos = (ch[:, None, :] * cl[None] - sh[:, None, :] * sl[None]) * scale
    sin = (sh[:, None, :] * cl[None] + ch[:, None, :] * sl[None]) * scale
    return jnp.concatenate([cos.reshape(SEQ_SAMPLE, SEQ_SAMPLE), sin.reshape(SEQ_SAMPLE, SEQ_SAMPLE)],
                           axis=0).astype(BF16)


def _dft_prompt_tables():
    n1 = FFT_N1_PROMPT
    n = n1 * FFT_N2
    k1 = jnp.arange(n1, dtype=jnp.int32)
    c1, s1 = _cos_sin(k1[:, None] * k1[None, :], n1)
    d1 = jnp.concatenate([c1, -s1], axis=0).astype(BF16)
    k2 = jnp.arange(FFT_N2, dtype=jnp.int32)
    ca, sa = _cos_sin(k1[:, None] * k2[None, :], n)
    cb, sb = _cos_sin(k2[:, None] * k2[None, :], FFT_N2)
    cr = (ca[:, None, :] * cb[None] - sa[:, None, :] * sb[None]) * (n ** -0.5)
    sr = (sa[:, None, :] * cb[None] + ca[:, None, :] * sb[None]) * (n ** -0.5)
    top = jnp.concatenate([cr, sr], axis=2)
    bot = jnp.concatenate([sr, -cr], axis=2)
    l2 = jnp.concatenate([top, bot], axis=1).astype(BF16)
    return d1, l2


def _fold_body(m_ref, w_ref, o_ref):
    o_ref[...] = jnp.dot(m_ref[...], w_ref[...], preferred_element_type=F32,
                         precision=lax.Precision.HIGHEST).astype(BF16)


def _fold_channel_dft(w_out_f):
    c = jnp.arange(HALF, dtype=jnp.int32)
    same = (c[:, None] // GROUP_CH) == (c[None, :] // GROUP_CH)
    ang = (2.0 * math.pi / GROUP_CH) * ((c[:, None] * c[None, :]) % GROUP_CH).astype(F32)
    scale = GROUP_CH ** -0.5
    cbd = jnp.where(same, jnp.cos(ang), 0.0) * scale
    sbd = jnp.where(same, jnp.sin(ang), 0.0) * scale
    m = jnp.concatenate([cbd, -sbd], axis=0)
    return pl.pallas_call(
        _fold_body, name="fold_channel_dft",
        out_shape=jax.ShapeDtypeStruct((2 * HALF, D), BF16),
    )(m, w_out_f)


def _route_and_store(x_new, g_ref, wr_ref, br_ref, x_out_ref, row_ref, meta_ref):
    x_out_ref[...] = x_new
    h = _rms(x_new, g_ref[...])
    h_hi = h.astype(BF16)
    h_lo = (h - h_hi.astype(F32)).astype(BF16)
    p = _dot(h_hi, wr_ref[...])
    q = _dot(h_lo, wr_ref[...])
    lg = p[:, :META] + p[:, META:] + q[:, :META] + q[:, META:] + br_ref[...]
    lane = lax.broadcasted_iota(jnp.int32, lg.shape, 1).astype(F32)
    neg = -jnp.inf
    lc = jnp.where(lane < N_GROUPS, lg, neg)
    mc = jnp.max(lc, axis=-1, keepdims=True)
    grp = jnp.min(jnp.where(lc == mc, lane, META), axis=-1, keepdims=True)
    p_grp = 1.0 / jnp.sum(jnp.exp(lc - mc), axis=-1, keepdims=True)
    lo = N_GROUPS + EPG * grp
    lf = jnp.where((lane >= lo) & (lane < lo + EPG), lg, neg)
    m1 = jnp.max(lf, axis=-1, keepdims=True)
    i1 = jnp.min(jnp.where(lf == m1, lane, META), axis=-1, keepdims=True)
    lf2 = jnp.where(lane == i1, neg, lf)
    m2 = jnp.max(lf2, axis=-1, keepdims=True)
    i2 = jnp.min(jnp.where(lf2 == m2, lane, META), axis=-1, keepdims=True)
    e = jnp.exp(m2 - m1)
    gate1 = p_grp / (1.0 + e)
    gate2 = p_grp * e / (1.0 + e)
    j1 = i1 - lo
    j2 = i2 - lo
    ja = jnp.minimum(j1, j2)
    jb = jnp.maximum(j1, j2)
    bucket = grp * N_PAIRS + (ja * (2 * EPG - 1 - ja)) * 0.5 + (jb - ja - 1)
    gate_a = jnp.where(j1 < j2, gate1, gate2)
    gate_b = jnp.where(j1 < j2, gate2, gate1)
    meta = jnp.where(lane == 0, gate_a,
                     jnp.where(lane == 1, gate_b,
                               jnp.where(lane == 2, bucket, 0.0)))
    for k in range(ROW_TILE):
        row_ref[pl.ds(k, TB, stride=ROUTED), :] = h[:, k * LANES:(k + 1) * LANES]
    row_ref[pl.ds(ROW_TILE, TB, stride=ROUTED), :] = meta
    meta_ref[...] = meta


def _router_operands(w_coarse, b_coarse, w_fine, b_fine):
    w = jnp.concatenate([w_coarse, w_fine.reshape(D, N_GROUPS * EPG)], axis=1)
    w = jnp.pad(w, ((0, 0), (0, META - w.shape[1])))
    w_hi = w.astype(BF16)
    w_lo = (w - w_hi.astype(F32)).astype(BF16)
    b = jnp.concatenate([b_coarse, b_fine.reshape(-1)])
    b = jnp.pad(b, (0, META - b.shape[0])).reshape(1, META)
    return jnp.concatenate([w_hi, w_lo], axis=1), b


def _l0_out_body(prev_ref, cur_ref, next_ref, cw_ref, cb_ref, lg_ref, lb_ref,
                 cp_ref, sp_ref, cs_ref, ss_ref, xp_ref, xs_ref, wa_ref, wc_ref, ws_ref,
                 g_ref, wr_ref, br_ref, x_out_ref, row_ref, meta_ref, buf, a2):
    in_prompt = pl.program_id(0) < N_PROMPT_TILES
    x = jnp.where(in_prompt, xp_ref[...], xs_ref[...])
    c = jnp.where(in_prompt, cp_ref[...].astype(BF16), cs_ref[...])
    s = jnp.where(in_prompt, sp_ref[...].astype(BF16), ss_ref[...])
    mix = _dot(c, wc_ref[...]) + _dot(s, ws_ref[...])
    _conv_module(prev_ref, cur_ref, next_ref, cw_ref, cb_ref, lg_ref, lb_ref, a2, buf)
    mix = mix + _dot(a2[...], wa_ref[...])
    _route_and_store(x + mix, g_ref, wr_ref, br_ref, x_out_ref, row_ref, meta_ref)


def _l0_out(a, conv_w, conv_b, ln_g, ln_b, cs_prompt, cs_sample, xp, xs, wa, wc, ws, g, wr, br):
    def sample_half(which):
        return pl.BlockSpec((None, TB, HALF), lambda i: (which, jnp.maximum(i - N_PROMPT_TILES, 0), 0))
    prev, nxt = _halo_specs(HALF)
    return pl.pallas_call(
        _l0_out_body, name="l0_out",
        grid=(N_TILES,),
        in_specs=[prev, _tile_spec(HALF), nxt, _full_spec((CONV_W + 1, HALF)),
                  _full_spec((1, HALF)), _full_spec((1, HALF)), _full_spec((1, HALF)),
                  _prompt_spec(HALF), _prompt_spec(HALF), sample_half(0), sample_half(1),
                  _prompt_spec(), _sample_spec(),
                  _full_spec((HALF, D)), _full_spec((HALF, D)), _full_spec((HALF, D)),
                  _full_spec((1, D)), _full_spec((D, 2 * META)), _full_spec((1, META))],
        out_specs=[_tile_spec(D), _row_tiles_spec(ROUTED), _tile_spec(META)],
        out_shape=[jax.ShapeDtypeStruct((T_ALL, D), F32), jax.ShapeDtypeStruct((T_ALL * ROUTED, LANES), F32),
                   jax.ShapeDtypeStruct((T_ALL, META), F32)],
        scratch_shapes=[pltpu.VMEM((8, TB + 2 * HALO, HALF), F32), pltpu.VMEM((TB, HALF), BF16)],
    )(a, a, a, conv_w, conv_b, ln_g, ln_b, *cs_prompt, cs_sample, cs_sample, xp, xs, wa, wc, ws, g, wr, br)


def _moe_plan(meta):
    bucket = meta[:, 2].astype(jnp.int32)
    counts = jnp.sum(bucket[None, :] == jnp.arange(N_BUCKETS, dtype=jnp.int32)[:, None], axis=1).astype(jnp.int32)
    nblk = (counts + MOE_ROWS - 1) // MOE_ROWS
    fill = jnp.cumsum(nblk * MOE_ROWS - counts)
    slots = N_MOE_BLOCKS * MOE_ROWS
    pad_id = jnp.arange(slots - T_ALL, dtype=jnp.int32)
    pad_key = jnp.sum(fill[None, :] <= pad_id[:, None], axis=1).astype(jnp.int32)
    keys, ids = lax.sort((jnp.concatenate([bucket, pad_key]), jnp.arange(slots, dtype=jnp.int32)),
                         num_keys=1, is_stable=True)
    keys = keys.reshape(N_MOE_BLOCKS, MOE_ROWS)
    ids = ids.reshape(N_MOE_BLOCKS, MOE_ROWS)
    real = ids < T_ALL
    cnt = jnp.sum(real, axis=1).astype(jnp.int32)
    j = jnp.arange(N_MOE_BLOCKS, dtype=jnp.int32)
    r = jnp.arange(MOE_ROWS, dtype=jnp.int32)[None, :]
    spare = T_ALL + (j % 2)[:, None] * MOE_ROWS + r
    src = (jnp.where(real, ids, ids - T_ALL) * ROUTED).reshape(N_MOE_BLOCKS, 1, MOE_ROWS)
    dst = (jnp.where(real, ids, spare) * ROW_TILE).reshape(N_MOE_BLOCKS, 1, MOE_ROWS)
    bj = keys[:, 0]
    last_used = lax.dynamic_index_in_dim(bj, jnp.sum(nblk) - 1, keepdims=False)
    bj = jnp.where(bj < N_BUCKETS, bj, last_used)
    pa, pb = np.triu_indices(EPG, k=1)
    grp = bj // N_PAIRS
    ea = grp * EPG + jnp.asarray(pa, jnp.int32)[bj % N_PAIRS]
    eb = grp * EPG + jnp.asarray(pb, jnp.int32)[bj % N_PAIRS]
    return ea.astype(jnp.int32), eb.astype(jnp.int32), cnt, src, dst


def _moe_body(ea_ref, eb_ref, cnt_ref, src_ref, dst_ref, rows_hbm,
              wga_ref, wua_ref, wda_ref, wgb_ref, wub_ref, wdb_ref,
              y_hbm, xbuf, ybuf, gsem, ssem):
    s = pl.program_id(0)
    tile = ROW_TILE
    buf_rows = MOE_ROWS * tile

    def rows_in(b):
        inside = (b >= 0) & (b < N_MOE_BLOCKS)
        return jnp.where(inside, cnt_ref[jnp.clip(b, 0, N_MOE_BLOCKS - 1)], 0)

    sizes = (MOE_ROWS // 2, MOE_ROWS)

    def size_is(b, n_tok):
        n = rows_in(b)
        return (n > 0) & ((n <= sizes[0]) == (n_tok == sizes[0]))

    def wait_fetch(p, n_tok):
        pltpu.make_async_copy(rows_hbm.at[pl.ds(0, n_tok * ROUTED)], xbuf.at[p, pl.ds(0, n_tok * ROUTED)],
                              gsem.at[p]).wait()

    def wait_send(p, n_tok):
        pltpu.make_async_copy(ybuf.at[p, pl.ds(0, n_tok * tile)], y_hbm.at[pl.ds(0, n_tok * tile)],
                              ssem.at[p]).wait()

    def fetch_rows(p, first, stop):
        for r in range(first, stop):
            pltpu.make_async_copy(rows_hbm.at[pl.ds(src_ref[0, 0, r], ROUTED)],
                                  xbuf.at[p, pl.ds(r * ROUTED, ROUTED)], gsem.at[p]).start(priority=r % 2)

    def compute_and_send(q, n_tok):
        wait_fetch(q, n_tok)
        x = jnp.concatenate([xbuf[q, pl.ds(k, n_tok, stride=ROUTED), :] for k in range(tile)],
                            axis=1).astype(BF16)
        record = xbuf[q, pl.ds(tile, n_tok, stride=ROUTED), :]
        gate_a = record[:, 0:1]
        gate_b = record[:, 1:2]

        def hidden(wg_ref, wu_ref, gate):
            g = _dot(x, wg_ref[0])
            u = _dot(x, wu_ref[0])
            return (g * jax.nn.sigmoid(g) * u * gate).astype(BF16)

        y = (_dot(hidden(wga_ref, wua_ref, gate_a), wda_ref[0])
             + _dot(hidden(wgb_ref, wub_ref, gate_b), wdb_ref[0]))
        for k in range(tile):
            ybuf[q, pl.ds(k, n_tok, stride=tile), :] = y[:, k * LANES:(k + 1) * LANES]
        for r in range(n_tok):
            start = pl.multiple_of(dst_ref[0, 0, r], tile)
            pltpu.make_async_copy(ybuf.at[q, pl.ds(r * tile, tile)], y_hbm.at[pl.ds(start, tile)],
                                  ssem.at[q]).start(priority=r % 2)

    @pl.when(s == 0)
    def _():
        ybuf[...] = jnp.zeros(ybuf.shape, ybuf.dtype)
        for p in range(2):
            clear = pltpu.make_async_copy(ybuf.at[p], y_hbm.at[pl.ds((T_ALL + p * MOE_ROWS) * tile, buf_rows)],
                                          ssem.at[p])
            clear.start()
            clear.wait()

    for p in range(2):
        q = 1 - p
        mine = (s % 2) == p

        for n_tok in sizes:
            @pl.when(mine & size_is(s - 3, n_tok))
            def _():
                wait_send(q, n_tok)

        @pl.when(mine & (rows_in(s) > 0))
        def _():
            fetch_rows(p, 0, sizes[0])

        @pl.when(mine & (rows_in(s) > sizes[0]))
        def _():
            fetch_rows(p, sizes[0], sizes[1])

        for n_tok in sizes:
            @pl.when(mine & size_is(s - 1, n_tok))
            def _():
                compute_and_send(q, n_tok)


def _moe(rows, plan, w_gate, w_up, w_down):
    ea, eb, cnt, src, dst = plan
    last = N_MOE_BLOCKS - 1

    def block_of(step, lag):
        return jnp.clip(step - lag, 0, last)

    src_spec = pl.BlockSpec((1, 1, MOE_ROWS), lambda s, *_: (block_of(s, 0), 0, 0), memory_space=pltpu.SMEM)
    dst_spec = pl.BlockSpec((1, 1, MOE_ROWS), lambda s, *_: (block_of(s, 1), 0, 0), memory_space=pltpu.SMEM)
    up_a = pl.BlockSpec((1, D, D_EXPERT), lambda s, ea, eb, cnt: (ea[block_of(s, 1)], 0, 0))
    up_b = pl.BlockSpec((1, D, D_EXPERT), lambda s, ea, eb, cnt: (eb[block_of(s, 1)], 0, 0))
    down_a = pl.BlockSpec((1, D_EXPERT, D), lambda s, ea, eb, cnt: (ea[block_of(s, 1)], 0, 0))
    down_b = pl.BlockSpec((1, D_EXPERT, D), lambda s, ea, eb, cnt: (eb[block_of(s, 1)], 0, 0))
    grid_spec = pltpu.PrefetchScalarGridSpec(
        num_scalar_prefetch=3,
        grid=(N_MOE_BLOCKS + 2,),
        in_specs=[src_spec, dst_spec, pl.BlockSpec(memory_space=pl.ANY),
                  up_a, up_a, down_a, up_b, up_b, down_b],
        out_specs=pl.BlockSpec(memory_space=pl.ANY),
        scratch_shapes=[pltpu.VMEM((2, MOE_ROWS * ROUTED, LANES), F32),
                        pltpu.VMEM((2, MOE_ROWS * ROW_TILE, LANES), F32),
                        pltpu.SemaphoreType.DMA((2,)), pltpu.SemaphoreType.DMA((2,))],
    )
    return pl.pallas_call(
        _moe_body, name="moe",
        grid_spec=grid_spec,
        out_shape=jax.ShapeDtypeStruct(((T_ALL + 2 * MOE_ROWS) * ROW_TILE, LANES), F32),
        compiler_params=pltpu.CompilerParams(dimension_semantics=("arbitrary",)),
    )(ea, eb, cnt, src, dst, rows, w_gate, w_up, w_down, w_gate, w_up, w_down)


def _l1_body(x_ref, xprev_ref, xnext_ref, y_ref, yprev_ref, ynext_ref, gm_ref, wi_ref, cw_ref, wo_ref,
             g_ref, wr_ref, br_ref, x_out_ref, row_ref, meta_ref, buf, ybuf):
    starts, ends = _seq_edges(pl.program_id(0))
    x = x_ref[...] + _read_row_tiles(y_ref, TB)
    x_edge = jnp.concatenate([xprev_ref[...] + _read_row_tiles(yprev_ref, EDGE),
                              xnext_ref[...] + _read_row_tiles(ynext_ref, EDGE)], axis=0)
    h = jnp.concatenate([_rms(x, gm_ref[...]), _rms(x_edge, gm_ref[...])], axis=0).astype(BF16)
    u = _dot(h, wi_ref[...])
    cv = u[:, D:2 * D] * u[:, 2 * D:]
    buf[HALO - EDGE:HALO, :] = jnp.where(starts, 0.0, cv[TB:TB + EDGE, :])
    buf[HALO:HALO + TB, :] = cv[:TB, :]
    buf[HALO + TB:HALO + TB + EDGE, :] = jnp.where(ends, 0.0, cv[TB + EDGE:, :])
    rows = CONV_ROWS // 2
    for r0 in range(0, TB, rows):
        conv = (cw_ref[0:1, :] * buf[r0 + HALO - 1:r0 + HALO - 1 + rows, :]
                + cw_ref[1:2, :] * buf[r0 + HALO:r0 + HALO + rows, :]
                + cw_ref[2:3, :] * buf[r0 + HALO + 1:r0 + HALO + 1 + rows, :])
        ybuf[r0:r0 + rows, :] = (u[r0:r0 + rows, :D] * conv).astype(BF16)
    _route_and_store(x + _dot(ybuf[...], wo_ref[...]), g_ref, wr_ref, br_ref, x_out_ref, row_ref, meta_ref)


def _l1(x, y, g_mix, w_in, conv_w, w_out, g_ffn, wr, br):
    per_tile = TB // EDGE
    last = T_ALL // EDGE - 1

    def prev_of(i):
        return jnp.maximum(i * per_tile - 1, 0)

    def next_of(i):
        return jnp.minimum((i + 1) * per_tile, last)

    def resident(shape):
        return pl.BlockSpec(shape, lambda *_: (0,) * len(shape), pipeline_mode=pl.Buffered(1))

    return pl.pallas_call(
        _l1_body, name="l1_mixer",
        grid=(N_TILES,),
        in_specs=[_tile_spec(D),
                  pl.BlockSpec((EDGE, D), lambda i: (prev_of(i), 0)),
                  pl.BlockSpec((EDGE, D), lambda i: (next_of(i), 0)),
                  _row_tiles_spec(),
                  pl.BlockSpec((EDGE * ROW_TILE, LANES), lambda i: (prev_of(i), 0)),
                  pl.BlockSpec((EDGE * ROW_TILE, LANES), lambda i: (next_of(i), 0)),
                  _full_spec((1, D)), resident((D, 3 * D)), _full_spec((8, D)), resident((D, D)),
                  _full_spec((1, D)), _full_spec((D, 2 * META)), _full_spec((1, META))],
        out_specs=[_tile_spec(D), _row_tiles_spec(ROUTED), _tile_spec(META)],
        out_shape=[jax.ShapeDtypeStruct((T_ALL, D), F32), jax.ShapeDtypeStruct((T_ALL * ROUTED, LANES), F32),
                   jax.ShapeDtypeStruct((T_ALL, META), F32)],
        scratch_shapes=[pltpu.VMEM((TB + 2 * HALO, D), F32), pltpu.VMEM((TB, D), BF16)],
        compiler_params=pltpu.CompilerParams(vmem_limit_bytes=56 * 1024 * 1024),
    )(x, x, x, y, y, y, g_mix, w_in, conv_w, w_out, g_ffn, wr, br)


def _final_body(x_ref, y_ref, g_ref, o_ref):
    o_ref[...] = _rms(x_ref[...] + _read_row_tiles(y_ref, TB_IO), g_ref[...])


def _final(x, y, g, *, token0, tokens):
    tile0 = token0 // TB_IO
    in_tile = pl.BlockSpec((TB_IO, D), lambda i: (i + tile0, 0))
    y_tile = pl.BlockSpec((TB_IO * ROW_TILE, LANES), lambda i: (i + tile0, 0))
    return pl.pallas_call(
        _final_body, name="final_norm",
        grid=(tokens // TB_IO,),
        in_specs=[in_tile, y_tile, _full_spec((1, D))],
        out_specs=pl.BlockSpec((TB_IO, D), lambda i: (i, 0)),
        out_shape=jax.ShapeDtypeStruct((tokens, D), F32),
    )(x, y, g)


def kernel(x_prompt, x_sample, l0_norm_mix, l0_w_in, l0_conv_w, l0_conv_b, l0_ln_g, l0_ln_b, l0_w_out, l0_norm_ffn, l0_w_coarse, l0_b_coarse, l0_w_fine, l0_b_fine, l0_w_gate, l0_w_up, l0_w_down, l1_norm_mix, l1_w_in, l1_conv_w, l1_w_out, l1_norm_ffn, l1_w_coarse, l1_b_coarse, l1_w_fine, l1_b_fine, l1_w_gate, l1_w_up, l1_w_down, final_norm):
    xp = x_prompt.reshape(T_PROMPT, D)
    xs = x_sample.reshape(T_SAMPLE, D)
    row = lambda v: v.reshape(1, -1)

    a, f = _l0_in(xp, xs, row(l0_norm_mix), l0_w_in.astype(BF16))
    conv_w = jnp.pad(l0_conv_w, ((0, 1), (0, 0)))
    d1, l2 = _dft_prompt_tables()
    cs_prompt = _dft2_prompt(_dft1_prompt(f, d1), l2)
    cs_sample = _dft_sample(f, _dft_sample_table())
    w_cs = _fold_channel_dft(l0_w_out[HALF:])
    wr0, br0 = _router_operands(l0_w_coarse, l0_b_coarse, l0_w_fine, l0_b_fine)
    x1, rows0, meta0 = _l0_out(a, conv_w, row(l0_conv_b), row(l0_ln_g), row(l0_ln_b), cs_prompt, cs_sample, xp, xs,
                               l0_w_out[:HALF].astype(BF16), w_cs[:HALF], w_cs[HALF:], row(l0_norm_ffn), wr0, br0)

    y0 = _moe(rows0, _moe_plan(meta0), l0_w_gate.astype(BF16), l0_w_up.astype(BF16), l0_w_down.astype(BF16))

    wr1, br1 = _router_operands(l1_w_coarse, l1_b_coarse, l1_w_fine, l1_b_fine)
    x3, rows1, meta1 = _l1(x1, y0, row(l1_norm_mix), l1_w_in.astype(BF16), jnp.pad(l1_conv_w, ((0, 5), (0, 0))),
                           l1_w_out.astype(BF16), row(l1_norm_ffn), wr1, br1)

    y1 = _moe(rows1, _moe_plan(meta1), l1_w_gate.astype(BF16), l1_w_up.astype(BF16), l1_w_down.astype(BF16))

    out_p = _final(x3, y1, row(final_norm), token0=0, tokens=T_PROMPT)
    out_s = _final(x3, y1, row(final_norm), token0=T_PROMPT, tokens=T_SAMPLE)
    return out_p.reshape(x_prompt.shape), out_s.reshape(x_sample.shape)
```

```python
import math

import jax
import jax.numpy as jnp
import numpy as np
from jax import lax
from jax.experimental import pallas as pl
from jax.experimental.pallas import tpu as pltpu

D = 1024
LANES = 128
T_PROMPT = 16384
SEQ_SAMPLE = 2048
T_SAMPLE = 32 * SEQ_SAMPLE
T_ALL = T_PROMPT + T_SAMPLE
HALF = 512
GROUP_CH = 64
CONV_W = 31
N_GROUPS = 4
EPG = 8
N_PAIRS = EPG * (EPG - 1) // 2
N_BUCKETS = N_GROUPS * N_PAIRS
D_EXPERT = 512
RMS_EPS = 1e-6
LN_EPS = 1e-5

TB = 512
TB_IO = 1024
N_TILES = T_ALL // TB
N_PROMPT_TILES = T_PROMPT // TB
HALO = 16
EDGE = 8
CONV_ROWS = 64
MOE_ROWS = 256
N_MOE_BLOCKS = T_ALL // MOE_ROWS + N_BUCKETS + 1
META = 128
ROW_TILE = D // LANES
ROUTED = ROW_TILE + 1
FFT_N2 = 128
FFT_N1_PROMPT = T_PROMPT // FFT_N2
FFT_G1 = 16
FFT_KB = 8

BF16 = jnp.bfloat16
F32 = jnp.float32


def _rms(x, g):
    return x * lax.rsqrt(jnp.mean(x * x, axis=-1, keepdims=True) + RMS_EPS) * g


def _dot(a, b):
    return jnp.dot(a, b, preferred_element_type=F32)


def _seq_edges(i):
    r0 = i * TB
    r1 = r0 + TB
    in_prompt = r0 < T_PROMPT
    starts = jnp.where(in_prompt, r0 == 0, (r0 - T_PROMPT) % SEQ_SAMPLE == 0)
    ends = jnp.where(in_prompt, r1 == T_PROMPT, (r1 - T_PROMPT) % SEQ_SAMPLE == 0)
    return starts, ends


def _tile_spec(width):
    return pl.BlockSpec((TB, width), lambda i: (i, 0))


def _row_tiles_spec(rows_per_token=ROW_TILE):
    return pl.BlockSpec((TB * rows_per_token, LANES), lambda i: (i, 0))


def _read_row_tiles(ref, tokens):
    return jnp.concatenate([ref[pl.ds(k, tokens, stride=ROW_TILE), :] for k in range(ROW_TILE)], axis=1)


def _full_spec(shape):
    return pl.BlockSpec(shape, lambda *_: (0,) * len(shape))


def _prompt_spec(width=D):
    return pl.BlockSpec((TB, width), lambda i: (jnp.minimum(i, N_PROMPT_TILES - 1), 0))


def _sample_spec(width=D):
    return pl.BlockSpec((TB, width), lambda i: (jnp.maximum(i - N_PROMPT_TILES, 0), 0))


def _halo_specs(width):
    per_tile = TB // HALO
    last = T_ALL // HALO - 1
    prev = pl.BlockSpec((HALO, width), lambda i: (jnp.maximum(i * per_tile - 1, 0), 0))
    nxt = pl.BlockSpec((HALO, width), lambda i: (jnp.minimum((i + 1) * per_tile, last), 0))
    return prev, nxt


def _l0_in_body(xp_ref, xs_ref, g_ref, w_ref, a_ref, f_ref):
    i = pl.program_id(0)
    x = jnp.where(i < T_PROMPT // TB_IO, xp_ref[...], xs_ref[...])
    h = _rms(x, g_ref[...])
    u = _dot(h.astype(BF16), w_ref[...])
    a_ref[...] = (u[:, :HALF] * jax.nn.sigmoid(u[:, HALF:2 * HALF])).astype(BF16)
    f_ref[...] = u[:, 2 * HALF:].astype(BF16)


def _l0_in(xp, xs, g, w_in):
    n_prompt = T_PROMPT // TB_IO
    half_tile = pl.BlockSpec((TB_IO, HALF), lambda i: (i, 0))
    return pl.pallas_call(
        _l0_in_body, name="l0_in",
        grid=(T_ALL // TB_IO,),
        in_specs=[pl.BlockSpec((TB_IO, D), lambda i: (jnp.minimum(i, n_prompt - 1), 0)),
                  pl.BlockSpec((TB_IO, D), lambda i: (jnp.maximum(i - n_prompt, 0), 0)),
                  _full_spec((1, D)), _full_spec((D, 3 * HALF))],
        out_specs=[half_tile, half_tile],
        out_shape=[jax.ShapeDtypeStruct((T_ALL, HALF), BF16), jax.ShapeDtypeStruct((T_ALL, HALF), BF16)],
    )(xp, xs, g, w_in)


def _conv_module(prev_ref, cur_ref, next_ref, w_ref, b_ref, g_ref, beta_ref, o_ref, buf):
    starts, ends = _seq_edges(pl.program_id(0))
    buf[0, 0:HALO, :] = jnp.where(starts, 0.0, prev_ref[...].astype(F32))
    buf[0, HALO:HALO + TB, :] = cur_ref[...].astype(F32)
    buf[0, HALO + TB:, :] = jnp.where(ends, 0.0, next_ref[...].astype(F32))
    rows = TB + 2 * HALO - 8
    for s in range(1, 8):
        for c0 in range(0, rows, 128):
            c1 = min(c0 + 128, rows)
            buf[s, c0:c1, :] = buf[0, c0 + s:c1 + s, :]
    first = HALO - CONV_W // 2
    for r0 in range(0, TB, CONV_ROWS):
        acc = jnp.broadcast_to(b_ref[...], (CONV_ROWS, HALF))
        for k in range(CONV_W):
            s = (first + k) % 8
            base = r0 + first + k - s
            acc = acc + w_ref[k:k + 1, :] * buf[s, base:base + CONV_ROWS, :]
        mu = jnp.mean(acc, axis=-1, keepdims=True)
        xc = acc - mu
        var = jnp.mean(xc * xc, axis=-1, keepdims=True)
        y = xc * lax.rsqrt(var + LN_EPS) * g_ref[...] + beta_ref[...]
        o_ref[r0:r0 + CONV_ROWS, :] = (y * jax.nn.sigmoid(y)).astype(BF16)


def _dft1_prompt_body(d_ref, x_ref, y_ref, xs, ys):
    n1 = FFT_N1_PROMPT
    x = x_ref[0].reshape(n1 * FFT_G1, HALF).astype(F32)
    for c in range(HALF // LANES):
        xs[c] = x[:, c * LANES:(c + 1) * LANES]
    for q in range(FFT_G1):
        rows = pl.ds(q, n1, stride=FFT_G1)
        xq = jnp.concatenate([xs[c, rows, :] for c in range(HALF // LANES)], axis=1)
        y = _dot(d_ref[...], xq.astype(BF16))
        for c in range(HALF // LANES):
            ys[0, c, rows, :] = y[:n1, c * LANES:(c + 1) * LANES]
            ys[1, c, rows, :] = y[n1:, c * LANES:(c + 1) * LANES]
    for p in range(2):
        for c in range(HALF // LANES):
            y_ref[p, :, :, c * LANES:(c + 1) * LANES] = ys[p, c].reshape(n1, FFT_G1, LANES)


def _dft1_prompt(f_all, d1):
    n1 = FFT_N1_PROMPT
    x = f_all.reshape(T_ALL // T_PROMPT, n1, FFT_N2, HALF)
    return pl.pallas_call(
        _dft1_prompt_body, name="dft1_prompt",
        grid=(FFT_N2 // FFT_G1,),
        in_specs=[_full_spec((2 * n1, n1)),
                  pl.BlockSpec((1, n1, FFT_G1, HALF), lambda j: (0, 0, j, 0))],
        out_specs=pl.BlockSpec((2, n1, FFT_G1, HALF), lambda j: (0, 0, j, 0)),
        out_shape=jax.ShapeDtypeStruct((2, n1, FFT_N2, HALF), F32),
        scratch_shapes=[pltpu.VMEM((HALF // LANES, n1 * FFT_G1, LANES), F32),
                        pltpu.VMEM((2, HALF // LANES, n1 * FFT_G1, LANES), F32)],
        compiler_params=pltpu.CompilerParams(vmem_limit_bytes=48 * 1024 * 1024),
    )(d1, x)


def _dft2_prompt_body(l_ref, y_ref, c_ref, s_ref, cs, ss):
    for q in range(FFT_KB):
        yq = y_ref[:, q].reshape(2 * FFT_N2, HALF).astype(BF16)
        r = _dot(l_ref[q], yq)
        rows = pl.ds(q, FFT_N2, stride=FFT_KB)
        for c in range(HALF // LANES):
            cs[c, rows, :] = r[:FFT_N2, c * LANES:(c + 1) * LANES]
            ss[c, rows, :] = r[FFT_N2:, c * LANES:(c + 1) * LANES]
    for c in range(HALF // LANES):
        c_ref[:, :, c * LANES:(c + 1) * LANES] = cs[c].reshape(FFT_N2, FFT_KB, LANES)
        s_ref[:, :, c * LANES:(c + 1) * LANES] = ss[c].reshape(FFT_N2, FFT_KB, LANES)


def _dft2_prompt(y, l2):
    n1 = FFT_N1_PROMPT
    out_sds = jax.ShapeDtypeStruct((FFT_N2, n1, HALF), F32)
    out_spec = pl.BlockSpec((FFT_N2, FFT_KB, HALF), lambda k: (0, k, 0))
    c, s = pl.pallas_call(
        _dft2_prompt_body, name="dft2_prompt",
        grid=(n1 // FFT_KB,),
        in_specs=[pl.BlockSpec((FFT_KB, 2 * FFT_N2, 2 * FFT_N2), lambda k: (k, 0, 0)),
                  pl.BlockSpec((2, FFT_KB, FFT_N2, HALF), lambda k: (0, k, 0, 0))],
        out_specs=[out_spec, out_spec],
        out_shape=[out_sds, out_sds],
        scratch_shapes=[pltpu.VMEM((HALF // LANES, FFT_N2 * FFT_KB, LANES), F32),
                        pltpu.VMEM((HALF // LANES, FFT_N2 * FFT_KB, LANES), F32)],
    )(l2, y)
    return c.reshape(T_PROMPT, HALF), s.reshape(T_PROMPT, HALF)


def _dft_sample_body(m_ref, x_ref, o_ref, xs):
    half = SEQ_SAMPLE // 2
    x = x_ref[...].astype(F32)
    for c in range(HALF // LANES):
        xs[c] = x[:, c * LANES:(c + 1) * LANES]

    def positions(parity):
        rows = pl.ds(parity, half, stride=2)
        return jnp.concatenate([xs[c, rows, :] for c in range(HALF // LANES)], axis=1).astype(BF16)

    x_even = positions(0)
    x_odd = positions(1)
    for part in range(2):
        e = _dot(m_ref[2 * part], x_even)
        o = _dot(m_ref[2 * part + 1], x_odd)
        o_ref[part, :half, :] = (e + o).astype(BF16)
        o_ref[part, half:, :] = (e - o).astype(BF16)


def _dft_sample(f_all, m):
    first = T_PROMPT // SEQ_SAMPLE
    return pl.pallas_call(
        _dft_sample_body, name="dft_sample",
        grid=(T_SAMPLE // SEQ_SAMPLE,),
        in_specs=[pl.BlockSpec((4, SEQ_SAMPLE // 2, SEQ_SAMPLE // 2), lambda b: (0, 0, 0),
                               pipeline_mode=pl.Buffered(1)),
                  pl.BlockSpec((SEQ_SAMPLE, HALF), lambda b: (first + b, 0))],
        out_specs=pl.BlockSpec((2, SEQ_SAMPLE, HALF), lambda b: (0, b, 0)),
        out_shape=jax.ShapeDtypeStruct((2, T_SAMPLE, HALF), BF16),
        scratch_shapes=[pltpu.VMEM((HALF // LANES, SEQ_SAMPLE, LANES), F32)],
        compiler_params=pltpu.CompilerParams(vmem_limit_bytes=48 * 1024 * 1024),
    )(m, f_all)


def _cos_sin(num, den):
    ang = (2.0 * math.pi / den) * (num % den).astype(F32)
    return jnp.cos(ang), jnp.sin(ang)


def _dft_sample_table():
    half = SEQ_SAMPLE // 2
    lo = 64
    hi = half // lo
    scale = SEQ_SAMPLE ** -0.5
    cos, sin = [], []
    for parity in range(2):
        n = 2 * jnp.arange(half, dtype=jnp.int32) + parity
        ch, sh = _cos_sin(lo * jnp.arange(hi, dtype=jnp.int32)[:, None] * n[None, :], SEQ_SAMPLE)
        cl, sl = _cos_sin(jnp.arange(lo, dtype=jnp.int32)[:, None] * n[None, :], SEQ_SAMPLE)
        cos.append(((ch[:, None, :] * cl[None] - sh[:, None, :] * sl[None]) * scale).reshape(half, half))
        sin.append(((sh[:, None, :] * cl[None] + ch[:, None, :] * sl[None]) * scale).reshape(half, half))
    return jnp.stack(cos + sin).astype(BF16)


def _dft_prompt_tables():
    n1 = FFT_N1_PROMPT
    n = n1 * FFT_N2
    k1 = jnp.arange(n1, dtype=jnp.int32)
    c1, s1 = _cos_sin(k1[:, None] * k1[None, :], n1)
    d1 = jnp.concatenate([c1, -s1], axis=0).astype(BF16)
    k2 = jnp.arange(FFT_N2, dtype=jnp.int32)
    ca, sa = _cos_sin(k1[:, None] * k2[None, :], n)
    cb, sb = _cos_sin(k2[:, None] * k2[None, :], FFT_N2)
    cr = (ca[:, None, :] * cb[None] - sa[:, None, :] * sb[None]) * (n ** -0.5)
    sr = (sa[:, None, :] * cb[None] + ca[:, None, :] * sb[None]) * (n ** -0.5)
    top = jnp.concatenate([cr, sr], axis=2)
    bot = jnp.concatenate([sr, -cr], axis=2)
    l2 = jnp.concatenate([top, bot], axis=1).astype(BF16)
    return d1, l2


def _fold_body(m_ref, w_ref, o_ref):
    o_ref[...] = jnp.dot(m_ref[...], w_ref[...], preferred_element_type=F32,
                         precision=lax.Precision.HIGHEST).astype(BF16)


def _fold_channel_dft(w_out_f):
    c = jnp.arange(HALF, dtype=jnp.int32)
    same = (c[:, None] // GROUP_CH) == (c[None, :] // GROUP_CH)
    ang = (2.0 * math.pi / GROUP_CH) * ((c[:, None] * c[None, :]) % GROUP_CH).astype(F32)
    scale = GROUP_CH ** -0.5
    cbd = jnp.where(same, jnp.cos(ang), 0.0) * scale
    sbd = jnp.where(same, jnp.sin(ang), 0.0) * scale
    m = jnp.concatenate([cbd, -sbd], axis=0)
    return pl.pallas_call(
        _fold_body, name="fold_channel_dft",
        out_shape=jax.ShapeDtypeStruct((2 * HALF, D), BF16),
    )(m, w_out_f)


def _route_and_store(x_new, g_ref, wr_ref, br_ref, x_out_ref, row_ref, meta_ref):
    x_out_ref[...] = x_new
    h = _rms(x_new, g_ref[...])
    h_hi = h.astype(BF16)
    h_lo = (h - h_hi.astype(F32)).astype(BF16)
    p = _dot(h_hi, wr_ref[...])
    q = _dot(h_lo, wr_ref[...])
    lg = p[:, :META] + p[:, META:] + q[:, :META] + q[:, META:] + br_ref[...]
    lane = lax.broadcasted_iota(jnp.int32, lg.shape, 1).astype(F32)
    neg = -jnp.inf
    lc = jnp.where(lane < N_GROUPS, lg, neg)
    mc = jnp.max(lc, axis=-1, keepdims=True)
    grp = jnp.min(jnp.where(lc == mc, lane, META), axis=-1, keepdims=True)
    p_grp = 1.0 / jnp.sum(jnp.exp(lc - mc), axis=-1, keepdims=True)
    lo = N_GROUPS + EPG * grp
    lf = jnp.where((lane >= lo) & (lane < lo + EPG), lg, neg)
    m1 = jnp.max(lf, axis=-1, keepdims=True)
    i1 = jnp.min(jnp.where(lf == m1, lane, META), axis=-1, keepdims=True)
    lf2 = jnp.where(lane == i1, neg, lf)
    m2 = jnp.max(lf2, axis=-1, keepdims=True)
    i2 = jnp.min(jnp.where(lf2 == m2, lane, META), axis=-1, keepdims=True)
    e = jnp.exp(m2 - m1)
    gate1 = p_grp / (1.0 + e)
    gate2 = p_grp * e / (1.0 + e)
    j1 = i1 - lo
    j2 = i2 - lo
    ja = jnp.minimum(j1, j2)
    jb = jnp.maximum(j1, j2)
    bucket = grp * N_PAIRS + (ja * (2 * EPG - 1 - ja)) * 0.5 + (jb - ja - 1)
    gate_a = jnp.where(j1 < j2, gate1, gate2)
    gate_b = jnp.where(j1 < j2, gate2, gate1)
    meta = jnp.where(lane == 0, gate_a,
                     jnp.where(lane == 1, gate_b,
                               jnp.where(lane == 2, bucket, 0.0)))
    for k in range(ROW_TILE):
        row_ref[pl.ds(k, TB, stride=ROUTED), :] = h[:, k * LANES:(k + 1) * LANES]
    row_ref[pl.ds(ROW_TILE, TB, stride=ROUTED), :] = meta
    meta_ref[...] = meta


def _router_operands(w_coarse, b_coarse, w_fine, b_fine):
    w = jnp.concatenate([w_coarse, w_fine.reshape(D, N_GROUPS * EPG)], axis=1)
    w = jnp.pad(w, ((0, 0), (0, META - w.shape[1])))
    w_hi = w.astype(BF16)
    w_lo = (w - w_hi.astype(F32)).astype(BF16)
    b = jnp.concatenate([b_coarse, b_fine.reshape(-1)])
    b = jnp.pad(b, (0, META - b.shape[0])).reshape(1, META)
    return jnp.concatenate([w_hi, w_lo], axis=1), b


def _l0_out_body(prev_ref, cur_ref, next_ref, cw_ref, cb_ref, lg_ref, lb_ref,
                 cp_ref, sp_ref, cs_ref, ss_ref, xp_ref, xs_ref, wa_ref, wc_ref, ws_ref,
                 g_ref, wr_ref, br_ref, x_out_ref, row_ref, meta_ref, buf, a2):
    in_prompt = pl.program_id(0) < N_PROMPT_TILES
    x = jnp.where(in_prompt, xp_ref[...], xs_ref[...])
    c = jnp.where(in_prompt, cp_ref[...].astype(BF16), cs_ref[...])
    s = jnp.where(in_prompt, sp_ref[...].astype(BF16), ss_ref[...])
    mix = _dot(c, wc_ref[...]) + _dot(s, ws_ref[...])
    _conv_module(prev_ref, cur_ref, next_ref, cw_ref, cb_ref, lg_ref, lb_ref, a2, buf)
    mix = mix + _dot(a2[...], wa_ref[...])
    _route_and_store(x + mix, g_ref, wr_ref, br_ref, x_out_ref, row_ref, meta_ref)


def _l0_out(a, conv_w, conv_b, ln_g, ln_b, cs_prompt, cs_sample, xp, xs, wa, wc, ws, g, wr, br):
    def sample_half(which):
        return pl.BlockSpec((None, TB, HALF), lambda i: (which, jnp.maximum(i - N_PROMPT_TILES, 0), 0))
    prev, nxt = _halo_specs(HALF)
    return pl.pallas_call(
        _l0_out_body, name="l0_out",
        grid=(N_TILES,),
        in_specs=[prev, _tile_spec(HALF), nxt, _full_spec((CONV_W + 1, HALF)),
                  _full_spec((1, HALF)), _full_spec((1, HALF)), _full_spec((1, HALF)),
                  _prompt_spec(HALF), _prompt_spec(HALF), sample_half(0), sample_half(1),
                  _prompt_spec(), _sample_spec(),
                  _full_spec((HALF, D)), _full_spec((HALF, D)), _full_spec((HALF, D)),
                  _full_spec((1, D)), _full_spec((D, 2 * META)), _full_spec((1, META))],
        out_specs=[_tile_spec(D), _row_tiles_spec(ROUTED), _tile_spec(META)],
        out_shape=[jax.ShapeDtypeStruct((T_ALL, D), F32), jax.ShapeDtypeStruct((T_ALL * ROUTED, LANES), F32),
                   jax.ShapeDtypeStruct((T_ALL, META), F32)],
        scratch_shapes=[pltpu.VMEM((8, TB + 2 * HALO, HALF), F32), pltpu.VMEM((TB, HALF), BF16)],
    )(a, a, a, conv_w, conv_b, ln_g, ln_b, *cs_prompt, cs_sample, cs_sample, xp, xs, wa, wc, ws, g, wr, br)


def _moe_plan(meta):
    bucket = meta[:, 2].astype(jnp.int32)
    counts = jnp.sum(bucket[None, :] == jnp.arange(N_BUCKETS, dtype=jnp.int32)[:, None], axis=1).astype(jnp.int32)
    nblk = (counts + MOE_ROWS - 1) // MOE_ROWS
    fill = jnp.cumsum(nblk * MOE_ROWS - counts)
    slots = N_MOE_BLOCKS * MOE_ROWS
    pad_id = jnp.arange(slots - T_ALL, dtype=jnp.int32)
    pad_key = jnp.sum(fill[None, :] <= pad_id[:, None], axis=1).astype(jnp.int32)
    keys, ids = lax.sort((jnp.concatenate([bucket, pad_key]), jnp.arange(slots, dtype=jnp.int32)),
                         num_keys=1, is_stable=True)
    keys = keys.reshape(N_MOE_BLOCKS, MOE_ROWS)
    ids = ids.reshape(N_MOE_BLOCKS, MOE_ROWS)
    real = ids < T_ALL
    cnt = jnp.sum(real, axis=1).astype(jnp.int32)
    j = jnp.arange(N_MOE_BLOCKS, dtype=jnp.int32)
    r = jnp.arange(MOE_ROWS, dtype=jnp.int32)[None, :]
    spare = T_ALL + (j % 2)[:, None] * MOE_ROWS + r
    src = (jnp.where(real, ids, ids - T_ALL) * ROUTED).reshape(N_MOE_BLOCKS, 1, MOE_ROWS)
    dst = (jnp.where(real, ids, spare) * ROW_TILE).reshape(N_MOE_BLOCKS, 1, MOE_ROWS)
    bj = keys[:, 0]
    last_used = lax.dynamic_index_in_dim(bj, jnp.sum(nblk) - 1, keepdims=False)
    bj = jnp.where(bj < N_BUCKETS, bj, last_used)
    pa, pb = np.triu_indices(EPG, k=1)
    grp = bj // N_PAIRS
    ea = grp * EPG + jnp.asarray(pa, jnp.int32)[bj % N_PAIRS]
    eb = grp * EPG + jnp.asarray(pb, jnp.int32)[bj % N_PAIRS]
    return ea.astype(jnp.int32), eb.astype(jnp.int32), cnt, src, dst


def _moe_body(ea_ref, eb_ref, cnt_ref, src_ref, dst_ref, rows_hbm,
              wga_ref, wua_ref, wda_ref, wgb_ref, wub_ref, wdb_ref,
              y_hbm, xbuf, ybuf, gsem, ssem):
    s = pl.program_id(0)
    tile = ROW_TILE
    buf_rows = MOE_ROWS * tile

    def rows_in(b):
        inside = (b >= 0) & (b < N_MOE_BLOCKS)
        return jnp.where(inside, cnt_ref[jnp.clip(b, 0, N_MOE_BLOCKS - 1)], 0)

    sizes = (MOE_ROWS // 2, MOE_ROWS)

    def size_is(b, n_tok):
        n = rows_in(b)
        return (n > 0) & ((n <= sizes[0]) == (n_tok == sizes[0]))

    def wait_fetch(p, n_tok):
        pltpu.make_async_copy(rows_hbm.at[pl.ds(0, n_tok * ROUTED)], xbuf.at[p, pl.ds(0, n_tok * ROUTED)],
                              gsem.at[p]).wait()

    def wait_send(p, n_tok):
        pltpu.make_async_copy(ybuf.at[p, pl.ds(0, n_tok * tile)], y_hbm.at[pl.ds(0, n_tok * tile)],
                              ssem.at[p]).wait()

    def fetch_rows(p, first, stop):
        for r in range(first, stop):
            pltpu.make_async_copy(rows_hbm.at[pl.ds(src_ref[0, 0, r], ROUTED)],
                                  xbuf.at[p, pl.ds(r * ROUTED, ROUTED)], gsem.at[p]).start(priority=r % 2)

    def compute_and_send(q, n_tok):
        wait_fetch(q, n_tok)
        x = jnp.concatenate([xbuf[q, pl.ds(k, n_tok, stride=ROUTED), :] for k in range(tile)],
                            axis=1).astype(BF16)
        record = xbuf[q, pl.ds(tile, n_tok, stride=ROUTED), :]
        gate_a = record[:, 0:1]
        gate_b = record[:, 1:2]

        def hidden(wg_ref, wu_ref, gate):
            g = _dot(x, wg_ref[0])
            u = _dot(x, wu_ref[0])
            return (g * jax.nn.sigmoid(g) * u * gate).astype(BF16)

        y = (_dot(hidden(wga_ref, wua_ref, gate_a), wda_ref[0])
             + _dot(hidden(wgb_ref, wub_ref, gate_b), wdb_ref[0]))
        for k in range(tile):
            ybuf[q, pl.ds(k, n_tok, stride=tile), :] = y[:, k * LANES:(k + 1) * LANES]
        for r in range(n_tok):
            start = pl.multiple_of(dst_ref[0, 0, r], tile)
            pltpu.make_async_copy(ybuf.at[q, pl.ds(r * tile, tile)], y_hbm.at[pl.ds(start, tile)],
                                  ssem.at[q]).start(priority=r % 2)

    @pl.when(s == 0)
    def _():
        ybuf[...] = jnp.zeros(ybuf.shape, ybuf.dtype)
        for p in range(2):
            clear = pltpu.make_async_copy(ybuf.at[p], y_hbm.at[pl.ds((T_ALL + p * MOE_ROWS) * tile, buf_rows)],
                                          ssem.at[p])
            clear.start()
            clear.wait()

    for p in range(2):
        q = 1 - p
        mine = (s % 2) == p

        for n_tok in sizes:
            @pl.when(mine & size_is(s - 3, n_tok))
            def _():
                wait_send(q, n_tok)

        @pl.when(mine & (rows_in(s) > 0))
        def _():
            fetch_rows(p, 0, sizes[0])

        @pl.when(mine & (rows_in(s) > sizes[0]))
        def _():
            fetch_rows(p, sizes[0], sizes[1])

        for n_tok in sizes:
            @pl.when(mine & size_is(s - 1, n_tok))
            def _():
                compute_and_send(q, n_tok)


def _moe(rows, plan, w_gate, w_up, w_down):
    ea, eb, cnt, src, dst = plan
    last = N_MOE_BLOCKS - 1

    def block_of(step, lag):
        return jnp.clip(step - lag, 0, last)

    src_spec = pl.BlockSpec((1, 1, MOE_ROWS), lambda s, *_: (block_of(s, 0), 0, 0), memory_space=pltpu.SMEM)
    dst_spec = pl.BlockSpec((1, 1, MOE_ROWS), lambda s, *_: (block_of(s, 1), 0, 0), memory_space=pltpu.SMEM)
    up_a = pl.BlockSpec((1, D, D_EXPERT), lambda s, ea, eb, cnt: (ea[block_of(s, 1)], 0, 0))
    up_b = pl.BlockSpec((1, D, D_EXPERT), lambda s, ea, eb, cnt: (eb[block_of(s, 1)], 0, 0))
    down_a = pl.BlockSpec((1, D_EXPERT, D), lambda s, ea, eb, cnt: (ea[block_of(s, 1)], 0, 0))
    down_b = pl.BlockSpec((1, D_EXPERT, D), lambda s, ea, eb, cnt: (eb[block_of(s, 1)], 0, 0))
    grid_spec = pltpu.PrefetchScalarGridSpec(
        num_scalar_prefetch=3,
        grid=(N_MOE_BLOCKS + 2,),
        in_specs=[src_spec, dst_spec, pl.BlockSpec(memory_space=pl.ANY),
                  up_a, up_a, down_a, up_b, up_b, down_b],
        out_specs=pl.BlockSpec(memory_space=pl.ANY),
        scratch_shapes=[pltpu.VMEM((2, MOE_ROWS * ROUTED, LANES), F32),
                        pltpu.VMEM((2, MOE_ROWS * ROW_TILE, LANES), F32),
                        pltpu.SemaphoreType.DMA((2,)), pltpu.SemaphoreType.DMA((2,))],
    )
    return pl.pallas_call(
        _moe_body, name="moe",
        grid_spec=grid_spec,
        out_shape=jax.ShapeDtypeStruct(((T_ALL + 2 * MOE_ROWS) * ROW_TILE, LANES), F32),
        compiler_params=pltpu.CompilerParams(dimension_semantics=("arbitrary",)),
    )(ea, eb, cnt, src, dst, rows, w_gate, w_up, w_down, w_gate, w_up, w_down)


def _l1_body(x_ref, xprev_ref, xnext_ref, y_ref, yprev_ref, ynext_ref, gm_ref, wi_ref, cw_ref, wo_ref,
             g_ref, wr_ref, br_ref, x_out_ref, row_ref, meta_ref, buf, ybuf):
    starts, ends = _seq_edges(pl.program_id(0))
    x = x_ref[...] + _read_row_tiles(y_ref, TB)
    x_edge = jnp.concatenate([xprev_ref[...] + _read_row_tiles(yprev_ref, EDGE),
                              xnext_ref[...] + _read_row_tiles(ynext_ref, EDGE)], axis=0)
    h = jnp.concatenate([_rms(x, gm_ref[...]), _rms(x_edge, gm_ref[...])], axis=0).astype(BF16)
    u = _dot(h, wi_ref[...])
    cv = u[:, D:2 * D] * u[:, 2 * D:]
    buf[HALO - EDGE:HALO, :] = jnp.where(starts, 0.0, cv[TB:TB + EDGE, :])
    buf[HALO:HALO + TB, :] = cv[:TB, :]
    buf[HALO + TB:HALO + TB + EDGE, :] = jnp.where(ends, 0.0, cv[TB + EDGE:, :])
    rows = CONV_ROWS // 2
    for r0 in range(0, TB, rows):
        conv = (cw_ref[0:1, :] * buf[r0 + HALO - 1:r0 + HALO - 1 + rows, :]
                + cw_ref[1:2, :] * buf[r0 + HALO:r0 + HALO + rows, :]
                + cw_ref[2:3, :] * buf[r0 + HALO + 1:r0 + HALO + 1 + rows, :])
        ybuf[r0:r0 + rows, :] = (u[r0:r0 + rows, :D] * conv).astype(BF16)
    _route_and_store(x + _dot(ybuf[...], wo_ref[...]), g_ref, wr_ref, br_ref, x_out_ref, row_ref, meta_ref)


def _l1(x, y, g_mix, w_in, conv_w, w_out, g_ffn, wr, br):
    per_tile = TB // EDGE
    last = T_ALL // EDGE - 1

    def prev_of(i):
        return jnp.maximum(i * per_tile - 1, 0)

    def next_of(i):
        return jnp.minimum((i + 1) * per_tile, last)

    def resident(shape):
        return pl.BlockSpec(shape, lambda *_: (0,) * len(shape), pipeline_mode=pl.Buffered(1))

    return pl.pallas_call(
        _l1_body, name="l1_mixer",
        grid=(N_TILES,),
        in_specs=[_tile_spec(D),
                  pl.BlockSpec((EDGE, D), lambda i: (prev_of(i), 0)),
                  pl.BlockSpec((EDGE, D), lambda i: (next_of(i), 0)),
                  _row_tiles_spec(),
                  pl.BlockSpec((EDGE * ROW_TILE, LANES), lambda i: (prev_of(i), 0)),
                  pl.BlockSpec((EDGE * ROW_TILE, LANES), lambda i: (next_of(i), 0)),
                  _full_spec((1, D)), resident((D, 3 * D)), _full_spec((8, D)), resident((D, D)),
                  _full_spec((1, D)), _full_spec((D, 2 * META)), _full_spec((1, META))],
        out_specs=[_tile_spec(D), _row_tiles_spec(ROUTED), _tile_spec(META)],
        out_shape=[jax.ShapeDtypeStruct((T_ALL, D), F32), jax.ShapeDtypeStruct((T_ALL * ROUTED, LANES), F32),
                   jax.ShapeDtypeStruct((T_ALL, META), F32)],
        scratch_shapes=[pltpu.VMEM((TB + 2 * HALO, D), F32), pltpu.VMEM((TB, D), BF16)],
        compiler_params=pltpu.CompilerParams(vmem_limit_bytes=56 * 1024 * 1024),
    )(x, x, x, y, y, y, g_mix, w_in, conv_w, w_out, g_ffn, wr, br)


def _final_body(x_ref, y_ref, g_ref, o_ref):
    o_ref[...] = _rms(x_ref[...] + _read_row_tiles(y_ref, TB_IO), g_ref[...])


def _final(x, y, g, *, token0, tokens):
    tile0 = token0 // TB_IO
    in_tile = pl.BlockSpec((TB_IO, D), lambda i: (i + tile0, 0))
    y_tile = pl.BlockSpec((TB_IO * ROW_TILE, LANES), lambda i: (i + tile0, 0))
    return pl.pallas_call(
        _final_body, name="final_norm",
        grid=(tokens // TB_IO,),
        in_specs=[in_tile, y_tile, _full_spec((1, D))],
        out_specs=pl.BlockSpec((TB_IO, D), lambda i: (i, 0)),
        out_shape=jax.ShapeDtypeStruct((tokens, D), F32),
    )(x, y, g)


def kernel(x_prompt, x_sample, l0_norm_mix, l0_w_in, l0_conv_w, l0_conv_b, l0_ln_g, l0_ln_b, l0_w_out, l0_norm_ffn, l0_w_coarse, l0_b_coarse, l0_w_fine, l0_b_fine, l0_w_gate, l0_w_up, l0_w_down, l1_norm_mix, l1_w_in, l1_conv_w, l1_w_out, l1_norm_ffn, l1_w_coarse, l1_b_coarse, l1_w_fine, l1_b_fine, l1_w_gate, l1_w_up, l1_w_down, final_norm):
    xp = x_prompt.reshape(T_PROMPT, D)
    xs = x_sample.reshape(T_SAMPLE, D)
    row = lambda v: v.reshape(1, -1)

    a, f = _l0_in(xp, xs, row(l0_norm_mix), l0_w_in.astype(BF16))
    conv_w = jnp.pad(l0_conv_w, ((0, 1), (0, 0)))
    d1, l2 = _dft_prompt_tables()
    cs_prompt = _dft2_prompt(_dft1_prompt(f, d1), l2)
    cs_sample = _dft_sample(f, _dft_sample_table())
    w_cs = _fold_channel_dft(l0_w_out[HALF:])
    wr0, br0 = _router_operands(l0_w_coarse, l0_b_coarse, l0_w_fine, l0_b_fine)
    x1, rows0, meta0 = _l0_out(a, conv_w, row(l0_conv_b), row(l0_ln_g), row(l0_ln_b), cs_prompt, cs_sample, xp, xs,
                               l0_w_out[:HALF].astype(BF16), w_cs[:HALF], w_cs[HALF:], row(l0_norm_ffn), wr0, br0)

    y0 = _moe(rows0, _moe_plan(meta0), l0_w_gate.astype(BF16), l0_w_up.astype(BF16), l0_w_down.astype(BF16))

    wr1, br1 = _router_operands(l1_w_coarse, l1_b_coarse, l1_w_fine, l1_b_fine)
    x3, rows1, meta1 = _l1(x1, y0, row(l1_norm_mix), l1_w_in.astype(BF16), jnp.pad(l1_conv_w, ((0, 5), (0, 0))),
                           l1_w_out.astype(BF16), row(l1_norm_ffn), wr1, br1)

    y1 = _moe(rows1, _moe_plan(meta1), l1_w_gate.astype(BF16), l1_w_up.astype(BF16), l1_w_down.astype(BF16))

    out_p = _final(x3, y1, row(final_norm), token0=0, tokens=T_PROMPT)
    out_s = _final(x3, y1, row(final_norm), token0=T_PROMPT, tokens=T_SAMPLE)
    return out_p.reshape(x_prompt.shape), out_s.reshape(x_sample.shape)
```

```python
import math

import jax
import jax.numpy as jnp
import numpy as np
from jax import lax
from jax.experimental import pallas as pl
from jax.experimental.pallas import tpu as pltpu

D = 1024
LANES = 128
T_PROMPT = 16384
SEQ_SAMPLE = 2048
T_SAMPLE = 32 * SEQ_SAMPLE
T_ALL = T_PROMPT + T_SAMPLE
HALF = 512
GROUP_CH = 64
CONV_W = 31
N_GROUPS = 4
EPG = 8
N_PAIRS = EPG * (EPG - 1) // 2
N_BUCKETS = N_GROUPS * N_PAIRS
D_EXPERT = 512
RMS_EPS = 1e-6
LN_EPS = 1e-5

TB = 512
TB_IO = 1024
N_TILES = T_ALL // TB
N_PROMPT_TILES = T_PROMPT // TB
HALO = 16
EDGE = 8
CONV_ROWS = 64
MOE_ROWS = 256
N_MOE_BLOCKS = T_ALL // MOE_ROWS + N_BUCKETS + 1
META = 128
ROW_TILE = D // LANES
ROUTED = ROW_TILE + 1
FFT_N2 = 128
FFT_N1_PROMPT = T_PROMPT // FFT_N2
FFT_G1 = 16
FFT_KB = 8
DFT_RADIX = 4

BF16 = jnp.bfloat16
F32 = jnp.float32


def _rms(x, g):
    return x * lax.rsqrt(jnp.mean(x * x, axis=-1, keepdims=True) + RMS_EPS) * g


def _dot(a, b):
    return jnp.dot(a, b, preferred_element_type=F32)


def _seq_edges(i):
    r0 = i * TB
    r1 = r0 + TB
    in_prompt = r0 < T_PROMPT
    starts = jnp.where(in_prompt, r0 == 0, (r0 - T_PROMPT) % SEQ_SAMPLE == 0)
    ends = jnp.where(in_prompt, r1 == T_PROMPT, (r1 - T_PROMPT) % SEQ_SAMPLE == 0)
    return starts, ends


def _tile_spec(width):
    return pl.BlockSpec((TB, width), lambda i: (i, 0))


def _row_tiles_spec(rows_per_token=ROW_TILE):
    return pl.BlockSpec((TB * rows_per_token, LANES), lambda i: (i, 0))


def _read_row_tiles(ref, tokens):
    return jnp.concatenate([ref[pl.ds(k, tokens, stride=ROW_TILE), :] for k in range(ROW_TILE)], axis=1)


def _full_spec(shape):
    return pl.BlockSpec(shape, lambda *_: (0,) * len(shape))


def _prompt_spec(width=D):
    return pl.BlockSpec((TB, width), lambda i: (jnp.minimum(i, N_PROMPT_TILES - 1), 0))


def _sample_spec(width=D):
    return pl.BlockSpec((TB, width), lambda i: (jnp.maximum(i - N_PROMPT_TILES, 0), 0))


def _halo_specs(width):
    per_tile = TB // HALO
    last = T_ALL // HALO - 1
    prev = pl.BlockSpec((HALO, width), lambda i: (jnp.maximum(i * per_tile - 1, 0), 0))
    nxt = pl.BlockSpec((HALO, width), lambda i: (jnp.minimum((i + 1) * per_tile, last), 0))
    return prev, nxt


def _l0_in_body(xp_ref, xs_ref, g_ref, w_ref, a_ref, f_ref):
    i = pl.program_id(0)
    x = jnp.where(i < T_PROMPT // TB_IO, xp_ref[...], xs_ref[...])
    h = _rms(x, g_ref[...])
    u = _dot(h.astype(BF16), w_ref[...])
    a_ref[...] = (u[:, :HALF] * jax.nn.sigmoid(u[:, HALF:2 * HALF])).astype(BF16)
    f_ref[...] = u[:, 2 * HALF:].astype(BF16)


def _l0_in(xp, xs, g, w_in):
    n_prompt = T_PROMPT // TB_IO
    half_tile = pl.BlockSpec((TB_IO, HALF), lambda i: (i, 0))
    return pl.pallas_call(
        _l0_in_body, name="l0_in",
        grid=(T_ALL // TB_IO,),
        in_specs=[pl.BlockSpec((TB_IO, D), lambda i: (jnp.minimum(i, n_prompt - 1), 0)),
                  pl.BlockSpec((TB_IO, D), lambda i: (jnp.maximum(i - n_prompt, 0), 0)),
                  _full_spec((1, D)), _full_spec((D, 3 * HALF))],
        out_specs=[half_tile, half_tile],
        out_shape=[jax.ShapeDtypeStruct((T_ALL, HALF), BF16), jax.ShapeDtypeStruct((T_ALL, HALF), BF16)],
    )(xp, xs, g, w_in)


def _conv_module(prev_ref, cur_ref, next_ref, w_ref, b_ref, g_ref, beta_ref, o_ref, buf):
    starts, ends = _seq_edges(pl.program_id(0))
    buf[0, 0:HALO, :] = jnp.where(starts, 0.0, prev_ref[...].astype(F32))
    buf[0, HALO:HALO + TB, :] = cur_ref[...].astype(F32)
    buf[0, HALO + TB:, :] = jnp.where(ends, 0.0, next_ref[...].astype(F32))
    rows = TB + 2 * HALO - 8
    for s in range(1, 8):
        for c0 in range(0, rows, 128):
            c1 = min(c0 + 128, rows)
            buf[s, c0:c1, :] = buf[0, c0 + s:c1 + s, :]
    first = HALO - CONV_W // 2
    for r0 in range(0, TB, CONV_ROWS):
        acc = jnp.broadcast_to(b_ref[...], (CONV_ROWS, HALF))
        for k in range(CONV_W):
            s = (first + k) % 8
            base = r0 + first + k - s
            acc = acc + w_ref[k:k + 1, :] * buf[s, base:base + CONV_ROWS, :]
        mu = jnp.mean(acc, axis=-1, keepdims=True)
        xc = acc - mu
        var = jnp.mean(xc * xc, axis=-1, keepdims=True)
        y = xc * lax.rsqrt(var + LN_EPS) * g_ref[...] + beta_ref[...]
        o_ref[r0:r0 + CONV_ROWS, :] = (y * jax.nn.sigmoid(y)).astype(BF16)


def _dft1_prompt_body(d_ref, x_ref, y_ref, xs, ys):
    n1 = FFT_N1_PROMPT
    x = x_ref[0].reshape(n1 * FFT_G1, HALF).astype(F32)
    for c in range(HALF // LANES):
        xs[c] = x[:, c * LANES:(c + 1) * LANES]
    for q in range(FFT_G1):
        rows = pl.ds(q, n1, stride=FFT_G1)
        xq = jnp.concatenate([xs[c, rows, :] for c in range(HALF // LANES)], axis=1)
        y = _dot(d_ref[...], xq.astype(BF16))
        for c in range(HALF // LANES):
            ys[0, c, rows, :] = y[:n1, c * LANES:(c + 1) * LANES]
            ys[1, c, rows, :] = y[n1:, c * LANES:(c + 1) * LANES]
    for p in range(2):
        for c in range(HALF // LANES):
            y_ref[p, :, :, c * LANES:(c + 1) * LANES] = ys[p, c].reshape(n1, FFT_G1, LANES)


def _dft1_prompt(f_all, d1):
    n1 = FFT_N1_PROMPT
    x = f_all.reshape(T_ALL // T_PROMPT, n1, FFT_N2, HALF)
    return pl.pallas_call(
        _dft1_prompt_body, name="dft1_prompt",
        grid=(FFT_N2 // FFT_G1,),
        in_specs=[_full_spec((2 * n1, n1)),
                  pl.BlockSpec((1, n1, FFT_G1, HALF), lambda j: (0, 0, j, 0))],
        out_specs=pl.BlockSpec((2, n1, FFT_G1, HALF), lambda j: (0, 0, j, 0)),
        out_shape=jax.ShapeDtypeStruct((2, n1, FFT_N2, HALF), F32),
        scratch_shapes=[pltpu.VMEM((HALF // LANES, n1 * FFT_G1, LANES), F32),
                        pltpu.VMEM((2, HALF // LANES, n1 * FFT_G1, LANES), F32)],
        compiler_params=pltpu.CompilerParams(vmem_limit_bytes=48 * 1024 * 1024),
    )(d1, x)


def _dft2_prompt_body(l_ref, y_ref, c_ref, s_ref, cs, ss):
    for q in range(FFT_KB):
        yq = y_ref[:, q].reshape(2 * FFT_N2, HALF).astype(BF16)
        r = _dot(l_ref[q], yq)
        rows = pl.ds(q, FFT_N2, stride=FFT_KB)
        for c in range(HALF // LANES):
            cs[c, rows, :] = r[:FFT_N2, c * LANES:(c + 1) * LANES]
            ss[c, rows, :] = r[FFT_N2:, c * LANES:(c + 1) * LANES]
    for c in range(HALF // LANES):
        c_ref[:, :, c * LANES:(c + 1) * LANES] = cs[c].reshape(FFT_N2, FFT_KB, LANES)
        s_ref[:, :, c * LANES:(c + 1) * LANES] = ss[c].reshape(FFT_N2, FFT_KB, LANES)


def _dft2_prompt(y, l2):
    n1 = FFT_N1_PROMPT
    out_sds = jax.ShapeDtypeStruct((FFT_N2, n1, HALF), F32)
    out_spec = pl.BlockSpec((FFT_N2, FFT_KB, HALF), lambda k: (0, k, 0))
    c, s = pl.pallas_call(
        _dft2_prompt_body, name="dft2_prompt",
        grid=(n1 // FFT_KB,),
        in_specs=[pl.BlockSpec((FFT_KB, 2 * FFT_N2, 2 * FFT_N2), lambda k: (k, 0, 0)),
                  pl.BlockSpec((2, FFT_KB, FFT_N2, HALF), lambda k: (0, k, 0, 0))],
        out_specs=[out_spec, out_spec],
        out_shape=[out_sds, out_sds],
        scratch_shapes=[pltpu.VMEM((HALF // LANES, FFT_N2 * FFT_KB, LANES), F32),
                        pltpu.VMEM((HALF // LANES, FFT_N2 * FFT_KB, LANES), F32)],
    )(l2, y)
    return c.reshape(T_PROMPT, HALF), s.reshape(T_PROMPT, HALF)


def _dft_sample_body(m_ref, x_ref, o_ref, xs):
    quarter = SEQ_SAMPLE // DFT_RADIX
    x = x_ref[...].astype(F32)
    for c in range(HALF // LANES):
        xs[c] = x[:, c * LANES:(c + 1) * LANES]
    ec, es = [], []
    for r in range(DFT_RADIX):
        rows = pl.ds(r, quarter, stride=DFT_RADIX)
        xr = jnp.concatenate([xs[c, rows, :] for c in range(HALF // LANES)], axis=1).astype(BF16)
        ec.append(_dot(m_ref[2 * r], xr))
        es.append(_dot(m_ref[2 * r + 1], xr))
    for q in range(DFT_RADIX):
        c_sum, s_sum = ec[0], es[0]
        for r in range(1, DFT_RADIX):
            phase = (r * q) % 4
            if phase == 0:
                c_sum, s_sum = c_sum + ec[r], s_sum + es[r]
            elif phase == 1:
                c_sum, s_sum = c_sum - es[r], s_sum + ec[r]
            elif phase == 2:
                c_sum, s_sum = c_sum - ec[r], s_sum - es[r]
            else:
                c_sum, s_sum = c_sum + es[r], s_sum - ec[r]
        o_ref[0, q * quarter:(q + 1) * quarter, :] = c_sum.astype(BF16)
        o_ref[1, q * quarter:(q + 1) * quarter, :] = s_sum.astype(BF16)


def _dft_sample(f_all, m):
    first = T_PROMPT // SEQ_SAMPLE
    return pl.pallas_call(
        _dft_sample_body, name="dft_sample",
        grid=(T_SAMPLE // SEQ_SAMPLE,),
        in_specs=[pl.BlockSpec((2 * DFT_RADIX, SEQ_SAMPLE // DFT_RADIX, SEQ_SAMPLE // DFT_RADIX),
                               lambda b: (0, 0, 0), pipeline_mode=pl.Buffered(1)),
                  pl.BlockSpec((SEQ_SAMPLE, HALF), lambda b: (first + b, 0))],
        out_specs=pl.BlockSpec((2, SEQ_SAMPLE, HALF), lambda b: (0, b, 0)),
        out_shape=jax.ShapeDtypeStruct((2, T_SAMPLE, HALF), BF16),
        scratch_shapes=[pltpu.VMEM((HALF // LANES, SEQ_SAMPLE, LANES), F32)],
        compiler_params=pltpu.CompilerParams(vmem_limit_bytes=48 * 1024 * 1024),
    )(m, f_all)


def _cos_sin(num, den):
    ang = (2.0 * math.pi / den) * (num % den).astype(F32)
    return jnp.cos(ang), jnp.sin(ang)


def _dft_sample_table():
    part = SEQ_SAMPLE // DFT_RADIX
    lo = 64
    hi = part // lo
    scale = SEQ_SAMPLE ** -0.5
    planes = []
    for r in range(DFT_RADIX):
        n = DFT_RADIX * jnp.arange(part, dtype=jnp.int32) + r
        ch, sh = _cos_sin(lo * jnp.arange(hi, dtype=jnp.int32)[:, None] * n[None, :], SEQ_SAMPLE)
        cl, sl = _cos_sin(jnp.arange(lo, dtype=jnp.int32)[:, None] * n[None, :], SEQ_SAMPLE)
        planes.append(((ch[:, None, :] * cl[None] - sh[:, None, :] * sl[None]) * scale).reshape(part, part))
        planes.append(((sh[:, None, :] * cl[None] + ch[:, None, :] * sl[None]) * scale).reshape(part, part))
    return jnp.stack(planes).astype(BF16)


def _dft_prompt_tables():
    n1 = FFT_N1_PROMPT
    n = n1 * FFT_N2
    k1 = jnp.arange(n1, dtype=jnp.int32)
    c1, s1 = _cos_sin(k1[:, None] * k1[None, :], n1)
    d1 = jnp.concatenate([c1, -s1], axis=0).astype(BF16)
    k2 = jnp.arange(FFT_N2, dtype=jnp.int32)
    ca, sa = _cos_sin(k1[:, None] * k2[None, :], n)
    cb, sb = _cos_sin(k2[:, None] * k2[None, :], FFT_N2)
    cr = (ca[:, None, :] * cb[None] - sa[:, None, :] * sb[None]) * (n ** -0.5)
    sr = (sa[:, None, :] * cb[None] + ca[:, None, :] * sb[None]) * (n ** -0.5)
    top = jnp.concatenate([cr, sr], axis=2)
    bot = jnp.concatenate([sr, -cr], axis=2)
    l2 = jnp.concatenate([top, bot], axis=1).astype(BF16)
    return d1, l2


def _fold_body(m_ref, w_ref, o_ref):
    o_ref[...] = jnp.dot(m_ref[...], w_ref[...], preferred_element_type=F32,
                         precision=lax.Precision.HIGHEST).astype(BF16)


def _fold_channel_dft(w_out_f):
    c = jnp.arange(HALF, dtype=jnp.int32)
    same = (c[:, None] // GROUP_CH) == (c[None, :] // GROUP_CH)
    ang = (2.0 * math.pi / GROUP_CH) * ((c[:, None] * c[None, :]) % GROUP_CH).astype(F32)
    scale = GROUP_CH ** -0.5
    cbd = jnp.where(same, jnp.cos(ang), 0.0) * scale
    sbd = jnp.where(same, jnp.sin(ang), 0.0) * scale
    m = jnp.concatenate([cbd, -sbd], axis=0)
    return pl.pallas_call(
        _fold_body, name="fold_channel_dft",
        out_shape=jax.ShapeDtypeStruct((2 * HALF, D), BF16),
    )(m, w_out_f)


def _route_and_store(x_new, g_ref, wr_ref, br_ref, x_out_ref, row_ref, meta_ref):
    x_out_ref[...] = x_new
    h = _rms(x_new, g_ref[...])
    h_hi = h.astype(BF16)
    h_lo = (h - h_hi.astype(F32)).astype(BF16)
    p = _dot(h_hi, wr_ref[...])
    q = _dot(h_lo, wr_ref[...])
    lg = p[:, :META] + p[:, META:] + q[:, :META] + q[:, META:] + br_ref[...]
    lane = lax.broadcasted_iota(jnp.int32, lg.shape, 1).astype(F32)
    neg = -jnp.inf
    lc = jnp.where(lane < N_GROUPS, lg, neg)
    mc = jnp.max(lc, axis=-1, keepdims=True)
    grp = jnp.min(jnp.where(lc == mc, lane, META), axis=-1, keepdims=True)
    p_grp = 1.0 / jnp.sum(jnp.exp(lc - mc), axis=-1, keepdims=True)
    lo = N_GROUPS + EPG * grp
    lf = jnp.where((lane >= lo) & (lane < lo + EPG), lg, neg)
    m1 = jnp.max(lf, axis=-1, keepdims=True)
    i1 = jnp.min(jnp.where(lf == m1, lane, META), axis=-1, keepdims=True)
    lf2 = jnp.where(lane == i1, neg, lf)
    m2 = jnp.max(lf2, axis=-1, keepdims=True)
    i2 = jnp.min(jnp.where(lf2 == m2, lane, META), axis=-1, keepdims=True)
    e = jnp.exp(m2 - m1)
    gate1 = p_grp / (1.0 + e)
    gate2 = p_grp * e / (1.0 + e)
    j1 = i1 - lo
    j2 = i2 - lo
    ja = jnp.minimum(j1, j2)
    jb = jnp.maximum(j1, j2)
    bucket = grp * N_PAIRS + (ja * (2 * EPG - 1 - ja)) * 0.5 + (jb - ja - 1)
    gate_a = jnp.where(j1 < j2, gate1, gate2)
    gate_b = jnp.where(j1 < j2, gate2, gate1)
    meta = jnp.where(lane == 0, gate_a,
                     jnp.where(lane == 1, gate_b,
                               jnp.where(lane == 2, bucket, 0.0)))
    for k in range(ROW_TILE):
        row_ref[pl.ds(k, TB, stride=ROUTED), :] = h[:, k * LANES:(k + 1) * LANES]
    row_ref[pl.ds(ROW_TILE, TB, stride=ROUTED), :] = meta
    meta_ref[...] = meta


def _router_operands(w_coarse, b_coarse, w_fine, b_fine):
    w = jnp.concatenate([w_coarse, w_fine.reshape(D, N_GROUPS * EPG)], axis=1)
    w = jnp.pad(w, ((0, 0), (0, META - w.shape[1])))
    w_hi = w.astype(BF16)
    w_lo = (w - w_hi.astype(F32)).astype(BF16)
    b = jnp.concatenate([b_coarse, b_fine.reshape(-1)])
    b = jnp.pad(b, (0, META - b.shape[0])).reshape(1, META)
    return jnp.concatenate([w_hi, w_lo], axis=1), b


def _l0_out_body(prev_ref, cur_ref, next_ref, cw_ref, cb_ref, lg_ref, lb_ref,
                 cp_ref, sp_ref, cs_ref, ss_ref, xp_ref, xs_ref, wa_ref, wc_ref, ws_ref,
                 g_ref, wr_ref, br_ref, x_out_ref, row_ref, meta_ref, buf, a2):
    in_prompt = pl.program_id(0) < N_PROMPT_TILES
    x = jnp.where(in_prompt, xp_ref[...], xs_ref[...])
    c = jnp.where(in_prompt, cp_ref[...].astype(BF16), cs_ref[...])
    s = jnp.where(in_prompt, sp_ref[...].astype(BF16), ss_ref[...])
    mix = _dot(c, wc_ref[...]) + _dot(s, ws_ref[...])
    _conv_module(prev_ref, cur_ref, next_ref, cw_ref, cb_ref, lg_ref, lb_ref, a2, buf)
    mix = mix + _dot(a2[...], wa_ref[...])
    _route_and_store(x + mix, g_ref, wr_ref, br_ref, x_out_ref, row_ref, meta_ref)


def _l0_out(a, conv_w, conv_b, ln_g, ln_b, cs_prompt, cs_sample, xp, xs, wa, wc, ws, g, wr, br):
    def sample_half(which):
        return pl.BlockSpec((None, TB, HALF), lambda i: (which, jnp.maximum(i - N_PROMPT_TILES, 0), 0))
    prev, nxt = _halo_specs(HALF)
    return pl.pallas_call(
        _l0_out_body, name="l0_out",
        grid=(N_TILES,),
        in_specs=[prev, _tile_spec(HALF), nxt, _full_spec((CONV_W + 1, HALF)),
                  _full_spec((1, HALF)), _full_spec((1, HALF)), _full_spec((1, HALF)),
                  _prompt_spec(HALF), _prompt_spec(HALF), sample_half(0), sample_half(1),
                  _prompt_spec(), _sample_spec(),
                  _full_spec((HALF, D)), _full_spec((HALF, D)), _full_spec((HALF, D)),
                  _full_spec((1, D)), _full_spec((D, 2 * META)), _full_spec((1, META))],
        out_specs=[_tile_spec(D), _row_tiles_spec(ROUTED), _tile_spec(META)],
        out_shape=[jax.ShapeDtypeStruct((T_ALL, D), F32), jax.ShapeDtypeStruct((T_ALL * ROUTED, LANES), F32),
                   jax.ShapeDtypeStruct((T_ALL, META), F32)],
        scratch_shapes=[pltpu.VMEM((8, TB + 2 * HALO, HALF), F32), pltpu.VMEM((TB, HALF), BF16)],
    )(a, a, a, conv_w, conv_b, ln_g, ln_b, *cs_prompt, cs_sample, cs_sample, xp, xs, wa, wc, ws, g, wr, br)


def _moe_plan(meta):
    bucket = meta[:, 2].astype(jnp.int32)
    counts = jnp.sum(bucket[None, :] == jnp.arange(N_BUCKETS, dtype=jnp.int32)[:, None], axis=1).astype(jnp.int32)
    nblk = (counts + MOE_ROWS - 1) // MOE_ROWS
    fill = jnp.cumsum(nblk * MOE_ROWS - counts)
    slots = N_MOE_BLOCKS * MOE_ROWS
    pad_id = jnp.arange(slots - T_ALL, dtype=jnp.int32)
    pad_key = jnp.sum(fill[None, :] <= pad_id[:, None], axis=1).astype(jnp.int32)
    keys, ids = lax.sort((jnp.concatenate([bucket, pad_key]), jnp.arange(slots, dtype=jnp.int32)),
                         num_keys=1, is_stable=True)
    keys = keys.reshape(N_MOE_BLOCKS, MOE_ROWS)
    ids = ids.reshape(N_MOE_BLOCKS, MOE_ROWS)
    real = ids < T_ALL
    cnt = jnp.sum(real, axis=1).astype(jnp.int32)
    j = jnp.arange(N_MOE_BLOCKS, dtype=jnp.int32)
    r = jnp.arange(MOE_ROWS, dtype=jnp.int32)[None, :]
    spare = T_ALL + (j % 2)[:, None] * MOE_ROWS + r
    src = (jnp.where(real, ids, ids - T_ALL) * ROUTED).reshape(N_MOE_BLOCKS, 1, MOE_ROWS)
    dst = (jnp.where(real, ids, spare) * ROW_TILE).reshape(N_MOE_BLOCKS, 1, MOE_ROWS)
    bj = keys[:, 0]
    last_used = lax.dynamic_index_in_dim(bj, jnp.sum(nblk) - 1, keepdims=False)
    bj = jnp.where(bj < N_BUCKETS, bj, last_used)
    pa, pb = np.triu_indices(EPG, k=1)
    grp = bj // N_PAIRS
    ea = grp * EPG + jnp.asarray(pa, jnp.int32)[bj % N_PAIRS]
    eb = grp * EPG + jnp.asarray(pb, jnp.int32)[bj % N_PAIRS]
    return ea.astype(jnp.int32), eb.astype(jnp.int32), cnt, src, dst


def _moe_body(ea_ref, eb_ref, cnt_ref, src_ref, dst_ref, rows_hbm,
              wga_ref, wua_ref, wda_ref, wgb_ref, wub_ref, wdb_ref,
              y_hbm, xbuf, ybuf, gsem, ssem):
    s = pl.program_id(0)
    tile = ROW_TILE
    buf_rows = MOE_ROWS * tile

    def rows_in(b):
        inside = (b >= 0) & (b < N_MOE_BLOCKS)
        return jnp.where(inside, cnt_ref[jnp.clip(b, 0, N_MOE_BLOCKS - 1)], 0)

    sizes = (MOE_ROWS // 2, MOE_ROWS)

    def size_is(b, n_tok):
        n = rows_in(b)
        return (n > 0) & ((n <= sizes[0]) == (n_tok == sizes[0]))

    def wait_fetch(p, n_tok):
        pltpu.make_async_copy(rows_hbm.at[pl.ds(0, n_tok * ROUTED)], xbuf.at[p, pl.ds(0, n_tok * ROUTED)],
                              gsem.at[p]).wait()

    def wait_send(p, n_tok):
        pltpu.make_async_copy(ybuf.at[p, pl.ds(0, n_tok * tile)], y_hbm.at[pl.ds(0, n_tok * tile)],
                              ssem.at[p]).wait()

    def fetch_rows(p, first, stop):
        for r in range(first, stop):
            pltpu.make_async_copy(rows_hbm.at[pl.ds(src_ref[0, 0, r], ROUTED)],
                                  xbuf.at[p, pl.ds(r * ROUTED, ROUTED)], gsem.at[p]).start(priority=r % 2)

    def compute_and_send(q, n_tok):
        wait_fetch(q, n_tok)
        x = jnp.concatenate([xbuf[q, pl.ds(k, n_tok, stride=ROUTED), :] for k in range(tile)],
                            axis=1).astype(BF16)
        record = xbuf[q, pl.ds(tile, n_tok, stride=ROUTED), :]
        gate_a = record[:, 0:1]
        gate_b = record[:, 1:2]

        def hidden(wg_ref, wu_ref, gate):
            g = _dot(x, wg_ref[0])
            u = _dot(x, wu_ref[0])
            return (g * jax.nn.sigmoid(g) * u * gate).astype(BF16)

        y = (_dot(hidden(wga_ref, wua_ref, gate_a), wda_ref[0])
             + _dot(hidden(wgb_ref, wub_ref, gate_b), wdb_ref[0]))
        for k in range(tile):
            ybuf[q, pl.ds(k, n_tok, stride=tile), :] = y[:, k * LANES:(k + 1) * LANES]
        for r in range(n_tok):
            start = pl.multiple_of(dst_ref[0, 0, r], tile)
            pltpu.make_async_copy(ybuf.at[q, pl.ds(r * tile, tile)], y_hbm.at[pl.ds(start, tile)],
                                  ssem.at[q]).start(priority=r % 2)

    @pl.when(s == 0)
    def _():
        ybuf[...] = jnp.zeros(ybuf.shape, ybuf.dtype)
        for p in range(2):
            clear = pltpu.make_async_copy(ybuf.at[p], y_hbm.at[pl.ds((T_ALL + p * MOE_ROWS) * tile, buf_rows)],
                                          ssem.at[p])
            clear.start()
            clear.wait()

    for p in range(2):
        q = 1 - p
        mine = (s % 2) == p

        for n_tok in sizes:
            @pl.when(mine & size_is(s - 3, n_tok))
            def _():
                wait_send(q, n_tok)

        @pl.when(mine & (rows_in(s) > 0))
        def _():
            fetch_rows(p, 0, sizes[0])

        @pl.when(mine & (rows_in(s) > sizes[0]))
        def _():
            fetch_rows(p, sizes[0], sizes[1])

        for n_tok in sizes:
            @pl.when(mine & size_is(s - 1, n_tok))
            def _():
                compute_and_send(q, n_tok)


def _moe(rows, plan, w_gate, w_up, w_down):
    ea, eb, cnt, src, dst = plan
    last = N_MOE_BLOCKS - 1

    def block_of(step, lag):
        return jnp.clip(step - lag, 0, last)

    src_spec = pl.BlockSpec((1, 1, MOE_ROWS), lambda s, *_: (block_of(s, 0), 0, 0), memory_space=pltpu.SMEM)
    dst_spec = pl.BlockSpec((1, 1, MOE_ROWS), lambda s, *_: (block_of(s, 1), 0, 0), memory_space=pltpu.SMEM)
    up_a = pl.BlockSpec((1, D, D_EXPERT), lambda s, ea, eb, cnt: (ea[block_of(s, 1)], 0, 0))
    up_b = pl.BlockSpec((1, D, D_EXPERT), lambda s, ea, eb, cnt: (eb[block_of(s, 1)], 0, 0))
    down_a = pl.BlockSpec((1, D_EXPERT, D), lambda s, ea, eb, cnt: (ea[block_of(s, 1)], 0, 0))
    down_b = pl.BlockSpec((1, D_EXPERT, D), lambda s, ea, eb, cnt: (eb[block_of(s, 1)], 0, 0))
    grid_spec = pltpu.PrefetchScalarGridSpec(
        num_scalar_prefetch=3,
        grid=(N_MOE_BLOCKS + 2,),
        in_specs=[src_spec, dst_spec, pl.BlockSpec(memory_space=pl.ANY),
                  up_a, up_a, down_a, up_b, up_b, down_b],
        out_specs=pl.BlockSpec(memory_space=pl.ANY),
        scratch_shapes=[pltpu.VMEM((2, MOE_ROWS * ROUTED, LANES), F32),
                        pltpu.VMEM((2, MOE_ROWS * ROW_TILE, LANES), F32),
                        pltpu.SemaphoreType.DMA((2,)), pltpu.SemaphoreType.DMA((2,))],
    )
    return pl.pallas_call(
        _moe_body, name="moe",
        grid_spec=grid_spec,
        out_shape=jax.ShapeDtypeStruct(((T_ALL + 2 * MOE_ROWS) * ROW_TILE, LANES), F32),
        compiler_params=pltpu.CompilerParams(dimension_semantics=("arbitrary",)),
    )(ea, eb, cnt, src, dst, rows, w_gate, w_up, w_down, w_gate, w_up, w_down)


def _l1_body(x_ref, xprev_ref, xnext_ref, y_ref, yprev_ref, ynext_ref, gm_ref, wi_ref, cw_ref, wo_ref,
             g_ref, wr_ref, br_ref, x_out_ref, row_ref, meta_ref, buf, ybuf):
    starts, ends = _seq_edges(pl.program_id(0))
    x = x_ref[...] + _read_row_tiles(y_ref, TB)
    x_edge = jnp.concatenate([xprev_ref[...] + _read_row_tiles(yprev_ref, EDGE),
                              xnext_ref[...] + _read_row_tiles(ynext_ref, EDGE)], axis=0)
    h = jnp.concatenate([_rms(x, gm_ref[...]), _rms(x_edge, gm_ref[...])], axis=0).astype(BF16)
    u = _dot(h, wi_ref[...])
    cv = u[:, D:2 * D] * u[:, 2 * D:]
    buf[HALO - EDGE:HALO, :] = jnp.where(starts, 0.0, cv[TB:TB + EDGE, :])
    buf[HALO:HALO + TB, :] = cv[:TB, :]
    buf[HALO + TB:HALO + TB + EDGE, :] = jnp.where(ends, 0.0, cv[TB + EDGE:, :])
    rows = CONV_ROWS // 2
    for r0 in range(0, TB, rows):
        conv = (cw_ref[0:1, :] * buf[r0 + HALO - 1:r0 + HALO - 1 + rows, :]
                + cw_ref[1:2, :] * buf[r0 + HALO:r0 + HALO + rows, :]
                + cw_ref[2:3, :] * buf[r0 + HALO + 1:r0 + HALO + 1 + rows, :])
        ybuf[r0:r0 + rows, :] = (u[r0:r0 + rows, :D] * conv).astype(BF16)
    _route_and_store(x + _dot(ybuf[...], wo_ref[...]), g_ref, wr_ref, br_ref, x_out_ref, row_ref, meta_ref)


def _l1(x, y, g_mix, w_in, conv_w, w_out, g_ffn, wr, br):
    per_tile = TB // EDGE
    last = T_ALL // EDGE - 1

    def prev_of(i):
        return jnp.maximum(i * per_tile - 1, 0)

    def next_of(i):
        return jnp.minimum((i + 1) * per_tile, last)

    def resident(shape):
        return pl.BlockSpec(shape, lambda *_: (0,) * len(shape), pipeline_mode=pl.Buffered(1))

    return pl.pallas_call(
        _l1_body, name="l1_mixer",
        grid=(N_TILES,),
        in_specs=[_tile_spec(D),
                  pl.BlockSpec((EDGE, D), lambda i: (prev_of(i), 0)),
                  pl.BlockSpec((EDGE, D), lambda i: (next_of(i), 0)),
                  _row_tiles_spec(),
                  pl.BlockSpec((EDGE * ROW_TILE, LANES), lambda i: (prev_of(i), 0)),
                  pl.BlockSpec((EDGE * ROW_TILE, LANES), lambda i: (next_of(i), 0)),
                  _full_spec((1, D)), resident((D, 3 * D)), _full_spec((8, D)), resident((D, D)),
                  _full_spec((1, D)), _full_spec((D, 2 * META)), _full_spec((1, META))],
        out_specs=[_tile_spec(D), _row_tiles_spec(ROUTED), _tile_spec(META)],
        out_shape=[jax.ShapeDtypeStruct((T_ALL, D), F32), jax.ShapeDtypeStruct((T_ALL * ROUTED, LANES), F32),
                   jax.ShapeDtypeStruct((T_ALL, META), F32)],
        scratch_shapes=[pltpu.VMEM((TB + 2 * HALO, D), F32), pltpu.VMEM((TB, D), BF16)],
        compiler_params=pltpu.CompilerParams(vmem_limit_bytes=56 * 1024 * 1024),
    )(x, x, x, y, y, y, g_mix, w_in, conv_w, w_out, g_ffn, wr, br)


def _final_body(x_ref, y_ref, g_ref, o_ref):
    o_ref[...] = _rms(x_ref[...] + _read_row_tiles(y_ref, TB_IO), g_ref[...])


def _final(x, y, g, *, token0, tokens):
    tile0 = token0 // TB_IO
    in_tile = pl.BlockSpec((TB_IO, D), lambda i: (i + tile0, 0))
    y_tile = pl.BlockSpec((TB_IO * ROW_TILE, LANES), lambda i: (i + tile0, 0))
    return pl.pallas_call(
        _final_body, name="final_norm",
        grid=(tokens // TB_IO,),
        in_specs=[in_tile, y_tile, _full_spec((1, D))],
        out_specs=pl.BlockSpec((TB_IO, D), lambda i: (i, 0)),
        out_shape=jax.ShapeDtypeStruct((tokens, D), F32),
    )(x, y, g)


def kernel(x_prompt, x_sample, l0_norm_mix, l0_w_in, l0_conv_w, l0_conv_b, l0_ln_g, l0_ln_b, l0_w_out, l0_norm_ffn, l0_w_coarse, l0_b_coarse, l0_w_fine, l0_b_fine, l0_w_gate, l0_w_up, l0_w_down, l1_norm_mix, l1_w_in, l1_conv_w, l1_w_out, l1_norm_ffn, l1_w_coarse, l1_b_coarse, l1_w_fine, l1_b_fine, l1_w_gate, l1_w_up, l1_w_down, final_norm):
    xp = x_prompt.reshape(T_PROMPT, D)
    xs = x_sample.reshape(T_SAMPLE, D)
    row = lambda v: v.reshape(1, -1)

    a, f = _l0_in(xp, xs, row(l0_norm_mix), l0_w_in.astype(BF16))
    conv_w = jnp.pad(l0_conv_w, ((0, 1), (0, 0)))
    d1, l2 = _dft_prompt_tables()
    cs_prompt = _dft2_prompt(_dft1_prompt(f, d1), l2)
    cs_sample = _dft_sample(f, _dft_sample_table())
    w_cs = _fold_channel_dft(l0_w_out[HALF:])
    wr0, br0 = _router_operands(l0_w_coarse, l0_b_coarse, l0_w_fine, l0_b_fine)
    x1, rows0, meta0 = _l0_out(a, conv_w, row(l0_conv_b), row(l0_ln_g), row(l0_ln_b), cs_prompt, cs_sample, xp, xs,
                               l0_w_out[:HALF].astype(BF16), w_cs[:HALF], w_cs[HALF:], row(l0_norm_ffn), wr0, br0)

    y0 = _moe(rows0, _moe_plan(meta0), l0_w_gate.astype(BF16), l0_w_up.astype(BF16), l0_w_down.astype(BF16))

    wr1, br1 = _router_operands(l1_w_coarse, l1_b_coarse, l1_w_fine, l1_b_fine)
    x3, rows1, meta1 = _l1(x1, y0, row(l1_norm_mix), l1_w_in.astype(BF16), jnp.pad(l1_conv_w, ((0, 5), (0, 0))),
                           l1_w_out.astype(BF16), row(l1_norm_ffn), wr1, br1)

    y1 = _moe(rows1, _moe_plan(meta1), l1_w_gate.astype(BF16), l1_w_up.astype(BF16), l1_w_down.astype(BF16))

    out_p = _final(x3, y1, row(final_norm), token0=0, tokens=T_PROMPT)
    out_s = _final(x3, y1, row(final_norm), token0=T_PROMPT, tokens=T_SAMPLE)
    return out_p.reshape(x_prompt.shape), out_s.reshape(x_sample.shape)
```
